```python
import math
import jax, jax.numpy as jnp
from jax import lax
import numpy as np

D_MODEL = 2048
BATCH = 4
SEQ = 2048
DEPTH = 4
DEC_BATCH = 8
DEC_SEQ = 8
PAST_LEN = 16384
PAGE_SIZE = 128

N_EVEN = (DEPTH + 1) // 2
N_ODD = DEPTH // 2
S5_WIDTH = D_MODEL // 2
S5_GROUP = 16
S5_GROUPS = S5_WIDTH // S5_GROUP
S5_STATE = 64
GM_WIDTH = D_MODEL // 2
GM_CHUNK = 128
GM_HEADS = 8
GM_HEAD_DIM = GM_WIDTH // GM_HEADS
N_HEADS = 16
HEAD_DIM = D_MODEL // N_HEADS
Q_BLOCK = 128
ATTN_SCALE = HEAD_DIM ** -0.5
FORGET_BIAS_INIT = 8.0
PEER_HEADS = 8
PEER_KEYS = 128
PEER_EXPERTS = PEER_KEYS * PEER_KEYS
PEER_TOPK = 16
PEER_DK = 256
PEER_CHUNK = 128
DN_ALPHA = (2.0 * DEPTH) ** 0.25
DN_BETA = (8.0 * DEPTH) ** -0.25
LN_EPS = 1e-5

kernel_name = "s5_gmlp_fox_peer_deepnorm_step"


def layer_norm(x, g, b):
    xf = x.astype(jnp.float32)
    mu = jnp.mean(xf, axis=-1, keepdims=True)
    var = jnp.mean(jnp.square(xf - mu), axis=-1, keepdims=True)
    return ((xf - mu) * lax.rsqrt(var + LN_EPS) * g.astype(jnp.float32) + b.astype(jnp.float32)).astype(x.dtype)


def s5_scan(u, h0, lam_re, lam_im, log_dt, b_re, b_im, c_re, c_im, d_skip):
    bsz, seq = u.shape[0], u.shape[1]
    f32 = jnp.float32
    uf = u.astype(f32).reshape(bsz, seq, S5_GROUPS, S5_GROUP)
    lam = lax.complex(lam_re.astype(f32), lam_im.astype(f32))
    dt = jnp.exp(log_dt.astype(f32))[:, None]
    lam_bar = jnp.exp(lam * dt)
    b_mat = lax.complex(b_re.astype(f32), b_im.astype(f32))
    b_bar = ((lam_bar - 1.0) / lam)[..., None] * b_mat
    c_mat = lax.complex(c_re.astype(f32), c_im.astype(f32))
    bu = jnp.einsum('gpc,blgc->blgp', b_bar, uf.astype(jnp.complex64))
    bu = bu.at[:, 0].add(lam_bar * h0)
    a = jnp.broadcast_to(lam_bar, bu.shape)

    def combine(left, right):
        a1, b1 = left
        a2, b2 = right
        return a1 * a2, a2 * b1 + b2

    _, h = lax.associative_scan(combine, (a, bu), axis=1)
    y = jnp.einsum('gcp,blgp->blgc', c_mat, h).real.reshape(bsz, seq, S5_WIDTH)
    y = y + d_skip.astype(f32) * uf.reshape(bsz, seq, S5_WIDTH)
    return y.astype(u.dtype), h[:, -1]


def chunk_gmlp(z, ln_g, ln_b, w_s, b_s):
    z = jax.nn.gelu(z)
    u, v = jnp.split(z, 2, axis=-1)
    v = layer_norm(v, ln_g, ln_b)
    bsz, seq = v.shape[0], v.shape[1]
    n_chunks = -(-seq // GM_CHUNK)
    pad = n_chunks * GM_CHUNK - seq
    vp = jnp.pad(v, ((0, 0), (0, pad), (0, 0))).reshape(bsz, n_chunks, GM_CHUNK, GM_HEADS, GM_HEAD_DIM)
    causal = jnp.tril(jnp.ones((GM_CHUNK, GM_CHUNK), dtype=bool))
    w = jnp.where(causal, w_s, 0)
    mixed = jnp.einsum('hts,bnshd->bnthd', w, vp) + b_s.T[:, :, None]
    mixed = mixed.reshape(bsz, n_chunks * GM_CHUNK, GM_WIDTH)[:, :seq]
    return u * mixed, v


def even_mixer(h, h0, w_in, lam_re, lam_im, log_dt, b_re, b_im, c_re, c_im, d_skip,
               w_glu, b_glu, gm_g, gm_b, w_s, b_s, w_out):
    proj = h @ w_in
    a_in = proj[..., :S5_WIDTH]
    g_in = proj[..., S5_WIDTH:]
    y, h_last = s5_scan(a_in, h0, lam_re, lam_im, log_dt, b_re, b_im, c_re, c_im, d_skip)
    y = jax.nn.gelu(y)
    a_out = y * jax.nn.sigmoid(y @ w_glu + b_glu)
    g_out, v_rows = chunk_gmlp(g_in, gm_g, gm_b, w_s, b_s)
    out = jnp.concatenate([a_out, g_out], axis=-1) @ w_out
    return out, h_last, v_rows


def fox_project(h, w_in, b_f):
    bsz, seq = h.shape[0], h.shape[1]
    proj = h @ w_in
    q = proj[..., :D_MODEL].reshape(bsz, seq, N_HEADS, HEAD_DIM)
    k = proj[..., D_MODEL:2 * D_MODEL].reshape(bsz, seq, N_HEADS, HEAD_DIM)
    v = proj[..., 2 * D_MODEL:3 * D_MODEL].reshape(bsz, seq, N_HEADS, HEAD_DIM)
    logf = jax.nn.log_sigmoid(proj[..., 3 * D_MODEL:].astype(jnp.float32) + b_f.astype(jnp.float32))
    return q, k, v, logf


def fox_prompt(q, k, v, logf):
    bsz, seq = q.shape[0], q.shape[1]
    nb = seq // Q_BLOCK
    F = jnp.cumsum(logf, axis=1).transpose(0, 2, 1)
    qb = q.reshape(bsz, nb, Q_BLOCK, N_HEADS, HEAD_DIM).transpose(1, 0, 2, 3, 4)
    Fb = F.reshape(bsz, N_HEADS, nb, Q_BLOCK).transpose(2, 0, 1, 3)
    key_pos = jnp.arange(seq)

    def block(args):
        bi, qi, fi = args
        s = jnp.einsum('bqhd,bkhd->bhqk', qi, k).astype(jnp.float32) * ATTN_SCALE
        s = s + fi[..., :, None] - F[:, :, None, :]
        q_pos = bi * Q_BLOCK + jnp.arange(Q_BLOCK)
        s = jnp.where(q_pos[:, None] >= key_pos[None, :], s, -jnp.inf)
        p = jax.nn.softmax(s, axis=-1).astype(v.dtype)
        return jnp.einsum('bhqk,bkhd->bqhd', p, v)

    out = lax.map(block, (jnp.arange(nb), qb, Fb))
    return out.transpose(1, 0, 2, 3, 4).reshape(bsz, seq, D_MODEL)


def fox_sample(q, k, v, logf, cache_k, cache_v, cache_logf, page_table, layer):
    n_new = q.shape[1]
    g_new = jnp.cumsum(logf, axis=1)
    causal = jnp.tril(jnp.ones((n_new, n_new), dtype=bool))

    def one_seq(args):
        qs, ks, vs, gs, pages = args
        kp = cache_k[layer, pages].reshape(-1, N_HEADS, HEAD_DIM)
        vp = cache_v[layer, pages].reshape(-1, N_HEADS, HEAD_DIM)
        lp = cache_logf[layer, pages].reshape(-1, N_HEADS).astype(jnp.float32)
        suffix = lax.cumsum(lp, axis=0, reverse=True) - lp
        gt = gs.T
        s_past = (jnp.einsum('thd,phd->htp', qs, kp).astype(jnp.float32) * ATTN_SCALE
                  + gt[:, :, None] + suffix.T[:, None, :])
        s_new = (jnp.einsum('thd,shd->hts', qs, ks).astype(jnp.float32) * ATTN_SCALE
                 + gt[:, :, None] - gt[:, None, :])
        s_new = jnp.where(causal, s_new, -jnp.inf)
        p = jax.nn.softmax(jnp.concatenate([s_past, s_new], axis=-1), axis=-1).astype(vs.dtype)
        n_past = kp.shape[0]
        return (jnp.einsum('htp,phd->thd', p[..., :n_past], vp)
                + jnp.einsum('hts,shd->thd', p[..., n_past:], vs))

    out = lax.map(one_seq, (q, k, v, g_new, page_table))
    return out.reshape(q.shape[0], n_new, D_MODEL)


def peer(x, w_q, sub_keys, u_tab, v_tab):
    bsz, seq = x.shape[0], x.shape[1]
    n_tok = bsz * seq
    xt = x.reshape(n_tok, D_MODEL)
    q = (xt @ w_q).reshape(n_tok, PEER_HEADS, 2, PEER_DK // 2)
    s = jnp.einsum('thcd,hcnd->thcn', q, sub_keys).astype(jnp.float32)
    s1, i1 = lax.top_k(s[:, :, 0], PEER_TOPK)
    s2, i2 = lax.top_k(s[:, :, 1], PEER_TOPK)
    cand_s = (s1[..., :, None] + s2[..., None, :]).reshape(n_tok, PEER_HEADS, PEER_TOPK * PEER_TOPK)
    cand_i = (i1[..., :, None] * PEER_KEYS + i2[..., None, :]).reshape(n_tok, PEER_HEADS, PEER_TOPK * PEER_TOPK)
    top_s, top_pos = lax.top_k(cand_s, PEER_TOPK)
    idx = jnp.take_along_axis(cand_i, top_pos, axis=-1)
    gate = jax.nn.softmax(top_s, axis=-1)
    n_sel = PEER_HEADS * PEER_TOPK
    n_blk = -(-n_tok // PEER_CHUNK)
    pad = n_blk * PEER_CHUNK - n_tok
    xb = jnp.pad(xt, ((0, pad), (0, 0))).reshape(n_blk, PEER_CHUNK, D_MODEL)
    ib = jnp.pad(idx.reshape(n_tok, n_sel), ((0, pad), (0, 0))).reshape(n_blk, PEER_CHUNK, n_sel)
    gb = jnp.pad(gate.reshape(n_tok, n_sel), ((0, pad), (0, 0))).reshape(n_blk, PEER_CHUNK, n_sel)

    def blk(args):
        xc, ic, gc = args
        ue = u_tab[ic]
        act = jax.nn.gelu(jnp.einsum('cd,ced->ce', xc, ue).astype(jnp.float32))
        ve = v_tab[ic]
        return jnp.einsum('ce,ced->cd', (gc * act).astype(ve.dtype), ve)

    out = lax.map(blk, (xb, ib, gb)).reshape(n_blk * PEER_CHUNK, D_MODEL)[:n_tok]
    return out.reshape(bsz, seq, D_MODEL).astype(x.dtype)


def setup_inputs(seed: int = 0) -> dict:
    key = jax.random.key(seed)
    ks = list(jax.random.split(key, 40))
    f32 = jnp.float32

    def nrm(k, shape, scale):
        return jax.random.normal(k, shape, f32) * scale

    n_pages = PAST_LEN // PAGE_SIZE
    n_used = DEC_BATCH * n_pages
    n_phys = (5 * n_used + 3) // 4
    inp = {}
    inp['x_prompt'] = nrm(ks[0], (BATCH, SEQ, D_MODEL), 1.0)
    inp['x_sample'] = nrm(ks[1], (DEC_BATCH, DEC_SEQ, D_MODEL), 1.0)
    inp['cache_k'] = nrm(ks[2], (N_ODD, n_phys, PAGE_SIZE, N_HEADS, HEAD_DIM), 1.0)
    inp['cache_v'] = nrm(ks[3], (N_ODD, n_phys, PAGE_SIZE, N_HEADS, HEAD_DIM), 1.0)
    inp['cache_logf'] = jax.nn.log_sigmoid(FORGET_BIAS_INIT + nrm(ks[4], (N_ODD, n_phys, PAGE_SIZE, N_HEADS), 1.0))
    inp['page_table'] = jax.random.permutation(ks[5], n_phys)[:n_used].reshape(DEC_BATCH, n_pages).astype(jnp.int32)
    inp['state_s5_re'] = nrm(ks[6], (N_EVEN, DEC_BATCH, S5_GROUPS, S5_STATE), 0.3)
    inp['state_s5_im'] = nrm(ks[7], (N_EVEN, DEC_BATCH, S5_GROUPS, S5_STATE), 0.3)
    inp['w_in_even'] = nrm(ks[8], (N_EVEN, D_MODEL, S5_WIDTH + 2 * GM_WIDTH), D_MODEL ** -0.5)
    inp['s5_lambda_re'] = -0.5 + nrm(ks[9], (N_EVEN, S5_GROUPS, S5_STATE), 0.01)
    inp['s5_lambda_im'] = (math.pi * jnp.arange(S5_STATE, dtype=f32)) + nrm(ks[10], (N_EVEN, S5_GROUPS, S5_STATE), 0.01)
    inp['s5_log_dt'] = jax.random.uniform(ks[11], (N_EVEN, S5_GROUPS), f32, math.log(1e-3), math.log(1e-1))
    inp['s5_b_re'] = nrm(ks[12], (N_EVEN, S5_GROUPS, S5_STATE, S5_GROUP), S5_GROUP ** -0.5)
    inp['s5_b_im'] = nrm(ks[13], (N_EVEN, S5_GROUPS, S5_STATE, S5_GROUP), S5_GROUP ** -0.5)
    inp['s5_c_re'] = nrm(ks[14], (N_EVEN, S5_GROUPS, S5_GROUP, S5_STATE), S5_STATE ** -0.5)
    inp['s5_c_im'] = nrm(ks[15], (N_EVEN, S5_GROUPS, S5_GROUP, S5_STATE), S5_STATE ** -0.5)
    inp['s5_d'] = nrm(ks[16], (N_EVEN, S5_WIDTH), 1.0)
    inp['s5_w_glu'] = nrm(ks[17], (N_EVEN, S5_WIDTH, S5_WIDTH), S5_WIDTH ** -0.5)
    inp['s5_b_glu'] = nrm(ks[18], (N_EVEN, S5_WIDTH), 0.01)
    inp['gm_ln_g'] = 1.0 + nrm(ks[19], (N_EVEN, GM_WIDTH), 0.02)
    inp['gm_ln_b'] = nrm(ks[20], (N_EVEN, GM_WIDTH), 0.02)
    inp['gm_w_s'] = nrm(ks[21], (N_EVEN, GM_HEADS, GM_CHUNK, GM_CHUNK), GM_CHUNK ** -0.5)
    inp['gm_b_s'] = nrm(ks[22], (N_EVEN, GM_HEADS, GM_CHUNK), 0.02)
    inp['w_out_even'] = nrm(ks[23], (N_EVEN, D_MODEL, D_MODEL), DN_BETA * D_MODEL ** -0.5)
    inp['w_in_odd'] = nrm(ks[24], (N_ODD, D_MODEL, 3 * D_MODEL + N_HEADS), D_MODEL ** -0.5)
    inp['b_f'] = FORGET_BIAS_INIT + nrm(ks[25], (N_ODD, N_HEADS), 0.1)
    inp['w_out_odd'] = nrm(ks[26], (N_ODD, D_MODEL, D_MODEL), DN_BETA * D_MODEL ** -0.5)
    inp['ln1_g'] = 1.0 + nrm(ks[27], (DEPTH, D_MODEL), 0.02)
    inp['ln1_b'] = nrm(ks[28], (DEPTH, D_MODEL), 0.02)
    inp['ln2_g'] = 1.0 + nrm(ks[29], (DEPTH, D_MODEL), 0.02)
    inp['ln2_b'] = nrm(ks[30], (DEPTH, D_MODEL), 0.02)
    inp['peer_w_q'] = nrm(ks[31], (DEPTH, D_MODEL, PEER_HEADS * PEER_DK), D_MODEL ** -0.5)
    inp['peer_sub_keys'] = nrm(ks[32], (DEPTH, PEER_HEADS, 2, PEER_KEYS, PEER_DK // 2), (PEER_DK // 2) ** -0.5)
    inp['peer_u'] = nrm(ks[33], (DEPTH, PEER_EXPERTS, D_MODEL), D_MODEL ** -0.5)
    inp['peer_v'] = nrm(ks[34], (DEPTH, PEER_EXPERTS, D_MODEL), DN_BETA * PEER_HEADS ** -0.5)
    return inp


def reference(x_prompt, x_sample, cache_k, cache_v, cache_logf, page_table, state_s5_re, state_s5_im,
              w_in_even, s5_lambda_re, s5_lambda_im, s5_log_dt, s5_b_re, s5_b_im, s5_c_re, s5_c_im,
              s5_d, s5_w_glu, s5_b_glu, gm_ln_g, gm_ln_b, gm_w_s, gm_b_s, w_out_even,
              w_in_odd, b_f, w_out_odd, ln1_g, ln1_b, ln2_g, ln2_b,
              peer_w_q, peer_sub_keys, peer_u, peer_v):
    f32 = jnp.float32
    xp, xs = x_prompt, x_sample
    h0_prompt = jnp.zeros((xp.shape[0], S5_GROUPS, S5_STATE), jnp.complex64)
    kp_l, vp_l, lp_l, ks_l, vs_l, ls_l = [], [], [], [], [], []
    s5p_l, s5s_l, gmv_l = [], [], []
    for li in range(DEPTH):
        j = li // 2
        if li % 2 == 0:
            ew = (w_in_even[j], s5_lambda_re[j], s5_lambda_im[j], s5_log_dt[j], s5_b_re[j], s5_b_im[j],
                  s5_c_re[j], s5_c_im[j], s5_d[j], s5_w_glu[j], s5_b_glu[j], gm_ln_g[j], gm_ln_b[j],
                  gm_w_s[j], gm_b_s[j], w_out_even[j])
            mp, hT_p, _ = even_mixer(xp, h0_prompt, *ew)
            h0_s = lax.complex(state_s5_re[j].astype(f32), state_s5_im[j].astype(f32))
            ms, hT_s, v_rows = even_mixer(xs, h0_s, *ew)
            s5p_l.append(hT_p)
            s5s_l.append(hT_s)
            gmv_l.append(v_rows)
        else:
            qp, kp, vp, lfp = fox_project(xp, w_in_odd[j], b_f[j])
            mp = fox_prompt(qp, kp, vp, lfp) @ w_out_odd[j]
            qs, kss, vss, lfs = fox_project(xs, w_in_odd[j], b_f[j])
            ms = fox_sample(qs, kss, vss, lfs, cache_k, cache_v, cache_logf, page_table, j) @ w_out_odd[j]
            kp_l.append(kp)
            vp_l.append(vp)
            lp_l.append(lfp)
            ks_l.append(kss)
            vs_l.append(vss)
            ls_l.append(lfs)
        xp = layer_norm(DN_ALPHA * xp + mp, ln1_g[li], ln1_b[li])
        xs = layer_norm(DN_ALPHA * xs + ms, ln1_g[li], ln1_b[li])
        pw = (peer_w_q[li], peer_sub_keys[li], peer_u[li], peer_v[li])
        xp = layer_norm(DN_ALPHA * xp + peer(xp, *pw), ln2_g[li], ln2_b[li])
        xs = layer_norm(DN_ALPHA * xs + peer(xs, *pw), ln2_g[li], ln2_b[li])
    new_k_prompt = jnp.stack(kp_l).astype(cache_k.dtype)
    new_v_prompt = jnp.stack(vp_l).astype(cache_v.dtype)
    new_logf_prompt = jnp.stack(lp_l).astype(cache_logf.dtype)
    new_k_sample = jnp.stack(ks_l).astype(cache_k.dtype)
    new_v_sample = jnp.stack(vs_l).astype(cache_v.dtype)
    new_logf_sample = jnp.stack(ls_l).astype(cache_logf.dtype)
    s5p = jnp.stack(s5p_l)
    s5s = jnp.stack(s5s_l)
    new_s5_re_prompt = s5p.real.astype(state_s5_re.dtype)
    new_s5_im_prompt = s5p.imag.astype(state_s5_im.dtype)
    new_s5_re_sample = s5s.real.astype(state_s5_re.dtype)
    new_s5_im_sample = s5s.imag.astype(state_s5_im.dtype)
    new_gmlp_v_sample = jnp.stack(gmv_l)
    return (xp, xs, new_k_prompt, new_v_prompt, new_logf_prompt, new_k_sample, new_v_sample, new_logf_sample,
            new_s5_re_prompt, new_s5_im_prompt, new_s5_re_sample, new_s5_im_sample, new_gmlp_v_sample)
```

```python
import functools
import math

import jax
import jax.numpy as jnp
from jax import lax
from jax.experimental import pallas as pl
from jax.experimental.pallas import tpu as pltpu

F32 = jnp.float32
BF16 = jnp.bfloat16
HIGHEST = lax.Precision.HIGHEST

D_MODEL = 2048
DEPTH = 4
PAGE_SIZE = 128
S5_WIDTH = 1024
S5_GROUP = 16
S5_GROUPS = 64
S5_STATE = 64
S5_GROUP_BLOCK = 8
S5_BLOCKS = S5_GROUPS // S5_GROUP_BLOCK
S5_BLOCK_STATE = S5_GROUP_BLOCK * S5_STATE
GM_WIDTH = 1024
GM_CHUNK = 128
GM_HEADS = 8
GM_HEAD_DIM = 128
N_HEADS = 16
HEAD_DIM = 128
ATTN_SCALE = HEAD_DIM ** -0.5
PEER_HEADS = 8
PEER_KEYS = 128
PEER_TOPK = 16
DN_ALPHA = (2.0 * DEPTH) ** 0.25
LN_EPS = 1e-5
SUBLANES = 8
LANES = 128
VMEM_LIMIT = 56 * 1024 * 1024
NEG_INF = float("-inf")


def _params(*sem):
    return pltpu.CompilerParams(dimension_semantics=sem, vmem_limit_bytes=VMEM_LIMIT)


def _layer_norm(y, g, b):
    mu = jnp.mean(y, axis=-1, keepdims=True)
    yc = y - mu
    var = jnp.mean(yc * yc, axis=-1, keepdims=True)
    return yc * lax.rsqrt(var + LN_EPS) * g + b


def _mm_kernel(x_ref, w_ref, o_ref):
    o_ref[...] = jnp.dot(x_ref[...].astype(BF16), w_ref[...].astype(BF16), preferred_element_type=F32)


def matmul(x, w, col_block0, n_out, tn, tm):
    t, k = x.shape
    assert t % tm == 0 and n_out % tn == 0
    return pl.pallas_call(
        _mm_kernel,
        grid=(n_out // tn, t // tm),
        in_specs=[pl.BlockSpec((tm, k), lambda j, i: (i, 0)),
                  pl.BlockSpec((k, tn), lambda j, i: (0, j + col_block0))],
        out_specs=pl.BlockSpec((tm, tn), lambda j, i: (i, j)),
        out_shape=jax.ShapeDtypeStruct((t, n_out), F32),
        compiler_params=_params("parallel", "parallel"),
        name="matmul",
    )(x, w)


def _mm2_dnln_kernel(a_ref, b_ref, wa_ref, wb_ref, x_ref, g_ref, beta_ref, o_ref):
    m = jnp.dot(a_ref[...].astype(BF16), wa_ref[...].astype(BF16), preferred_element_type=F32)
    if b_ref is not None:
        m += jnp.dot(b_ref[...].astype(BF16), wb_ref[...].astype(BF16), preferred_element_type=F32)
    o_ref[...] = _layer_norm(DN_ALPHA * x_ref[...] + m, g_ref[...], beta_ref[...])


def out_proj_dnln(a, b, w, x, g, beta, tm):
    t, ka = a.shape
    d = w.shape[1]
    row = lambda i: (i, 0)
    fixed = lambda i: (0, 0)
    if b is None:
        kern = lambda a_ref, wa_ref, x_ref, g_ref, beta_ref, o_ref: _mm2_dnln_kernel(
            a_ref, None, wa_ref, None, x_ref, g_ref, beta_ref, o_ref)
        ins = [a, w, x, g, beta]
        specs = [pl.BlockSpec((tm, ka), row), pl.BlockSpec((ka, d), fixed),
                 pl.BlockSpec((tm, d), row), pl.BlockSpec((1, d), fixed), pl.BlockSpec((1, d), fixed)]
    else:
        kb = b.shape[1]
        kern = _mm2_dnln_kernel
        ins = [a, b, w, w, x, g, beta]
        specs = [pl.BlockSpec((tm, ka), row), pl.BlockSpec((tm, kb), row),
                 pl.BlockSpec((ka, d), fixed), pl.BlockSpec((kb, d), lambda i: (ka // kb, 0)),
                 pl.BlockSpec((tm, d), row), pl.BlockSpec((1, d), fixed), pl.BlockSpec((1, d), fixed)]
    return pl.pallas_call(
        kern, grid=(t // tm,), in_specs=specs,
        out_specs=pl.BlockSpec((tm, d), row),
        out_shape=jax.ShapeDtypeStruct((t, d), F32),
        compiler_params=_params("parallel"),
        name="out_proj_dnln",
    )(*ins)


def _dnln_kernel(x_ref, m_ref, g_ref, b_ref, o_ref):
    o_ref[...] = _layer_norm(DN_ALPHA * x_ref[...] + m_ref[...], g_ref[...], b_ref[...])


def dn_layer_norm(x, m, g, b, tm):
    t, d = x.shape
    row = lambda i: (i, 0)
    fixed = lambda i: (0, 0)
    return pl.pallas_call(
        _dnln_kernel, grid=(t // tm,),
        in_specs=[pl.BlockSpec((tm, d), row), pl.BlockSpec((tm, d), row),
                  pl.BlockSpec((1, d), fixed), pl.BlockSpec((1, d), fixed)],
        out_specs=pl.BlockSpec((tm, d), row),
        out_shape=jax.ShapeDtypeStruct((t, d), F32),
        compiler_params=_params("parallel"),
        name="dn_layer_norm",
    )(x, m, g, b)


def _s5_kernel(u_ref, bd_ref, cd_ref, tab_ref, d_ref, h0_ref, y_ref, hl_ref, hs_sc, carry_sc, *, lc):
    ns = S5_BLOCK_STATE

    @pl.when(pl.program_id(2) == 0)
    def _():
        carry_sc[...] = h0_ref[...]

    u = u_ref[...]
    hs_sc[...] = jnp.dot(u, bd_ref[...], precision=HIGHEST, preferred_element_type=F32)

    def row_group(r, carry):
        cr, ci = carry
        rows = pl.ds(pl.multiple_of(r * SUBLANES, SUBLANES), SUBLANES)
        xr = hs_sc[rows, :ns]
        xi = hs_sc[rows, ns:]
        for k in range(3):
            ar = tab_ref[2 * k]
            ai = tab_ref[2 * k + 1]
            sr = pltpu.roll(xr, 1 << k, 0)
            si = pltpu.roll(xi, 1 << k, 0)
            xr, xi = xr + (ar * sr - ai * si), xi + (ar * si + ai * sr)
        pr = tab_ref[6]
        pi = tab_ref[7]
        xr, xi = xr + (pr * cr - pi * ci), xi + (pr * ci + pi * cr)
        hs_sc[rows, :ns] = xr
        hs_sc[rows, ns:] = xi
        return xr[SUBLANES - 1:, :], xi[SUBLANES - 1:, :]

    cr, ci = lax.fori_loop(0, lc // SUBLANES, row_group, (carry_sc[:, :ns], carry_sc[:, ns:]))
    carry_sc[:, :ns] = cr
    carry_sc[:, ns:] = ci
    hl_ref[...] = carry_sc[...]
    y = jnp.dot(hs_sc[...], cd_ref[...], precision=HIGHEST, preferred_element_type=F32)
    y_ref[...] = y + d_ref[...] * u


def _s5_tables(lam_re, lam_im, log_dt, b_re, b_im, c_re, c_im):
    dt = jnp.exp(log_dt.astype(F32))[:, None]
    mag = jnp.exp(lam_re * dt)
    lbr = mag * jnp.cos(lam_im * dt)
    lbi = mag * jnp.sin(lam_im * dt)
    den = lam_re * lam_re + lam_im * lam_im
    qr = ((lbr - 1.0) * lam_re + lbi * lam_im) / den
    qi = (lbi * lam_re - (lbr - 1.0) * lam_im) / den
    bbr = qr[..., None] * b_re - qi[..., None] * b_im
    bbi = qr[..., None] * b_im + qi[..., None] * b_re
    nb, gb = S5_BLOCKS, S5_GROUP_BLOCK
    eye = jnp.eye(gb, dtype=F32)

    def bdiag(m):
        m = m.reshape(nb, gb, S5_STATE, S5_GROUP)
        return jnp.einsum('ngpc,gh->ngchp', m, eye).reshape(nb, gb * S5_GROUP, gb * S5_STATE)

    def cdiag(m):
        m = m.reshape(nb, gb, S5_GROUP, S5_STATE)
        return jnp.einsum('ngcp,gh->ngphc', m, eye).reshape(nb, gb * S5_STATE, gb * S5_GROUP)

    bd = jnp.concatenate([bdiag(bbr), bdiag(bbi)], axis=-1)
    cd = jnp.concatenate([cdiag(c_re), cdiag(-c_im)], axis=1)

    def cmul(a, b):
        return a[0] * b[0] - a[1] * b[1], a[0] * b[1] + a[1] * b[0]

    powers = [(lbr, lbi)]
    for _ in range(SUBLANES - 1):
        powers.append(cmul(powers[-1], (lbr, lbi)))
    row = jnp.arange(SUBLANES)[:, None, None]
    tabs = []
    for k in range(3):
        sh = 1 << k
        for part in powers[sh - 1]:
            tabs.append(jnp.where(row >= sh, part[None], 0.0))
    tabs.append(jnp.stack([p[0] for p in powers]))
    tabs.append(jnp.stack([p[1] for p in powers]))
    tab = jnp.stack(tabs)
    tab = tab.reshape(8, SUBLANES, nb, S5_BLOCK_STATE).transpose(2, 0, 1, 3)
    return bd, cd, tab


def s5_scan(proj, h0_re, h0_im, tables, d_skip, lc):
    bd, cd, tab = tables
    bsz, seq = proj.shape[0], proj.shape[1]
    nb, ns = S5_BLOCKS, S5_BLOCK_STATE
    h0 = jnp.concatenate([h0_re.reshape(bsz, nb, 1, ns), h0_im.reshape(bsz, nb, 1, ns)], axis=-1)
    y, hl = pl.pallas_call(
        functools.partial(_s5_kernel, lc=lc),
        grid=(bsz, nb, seq // lc),
        in_specs=[pl.BlockSpec((None, lc, LANES), lambda b, g, l: (b, l, g)),
                  pl.BlockSpec((None, LANES, 2 * ns), lambda b, g, l: (g, 0, 0)),
                  pl.BlockSpec((None, 2 * ns, LANES), lambda b, g, l: (g, 0, 0)),
                  pl.BlockSpec((None, 8, SUBLANES, ns), lambda b, g, l: (g, 0, 0, 0)),
                  pl.BlockSpec((1, LANES), lambda b, g, l: (0, g)),
                  pl.BlockSpec((None, None, 1, 2 * ns), lambda b, g, l: (b, g, 0, 0))],
        out_specs=[pl.BlockSpec((None, lc, LANES), lambda b, g, l: (b, l, g)),
                   pl.BlockSpec((None, None, 1, 2 * ns), lambda b, g, l: (b, g, 0, 0))],
        out_shape=[jax.ShapeDtypeStruct((bsz, seq, S5_WIDTH), F32),
                   jax.ShapeDtypeStruct((bsz, nb, 1, 2 * ns), F32)],
        scratch_shapes=[pltpu.VMEM((lc, 2 * ns), F32), pltpu.VMEM((1, 2 * ns), F32)],
        compiler_params=_params("parallel", "parallel", "arbitrary"),
        name="s5_scan",
    )(proj, bd, cd, tab, d_skip.reshape(1, S5_WIDTH), h0)
    hl_re = hl[..., :ns].reshape(bsz, S5_GROUPS, S5_STATE)
    hl_im = hl[..., ns:].reshape(bsz, S5_GROUPS, S5_STATE)
    return y, hl_re, hl_im


def _glu_kernel(y_ref, w_ref, b_ref, o_ref):
    y = jax.nn.gelu(y_ref[...])
    z = jnp.dot(y.astype(BF16), w_ref[...].astype(BF16), preferred_element_type=F32) + b_ref[...]
    o_ref[...] = y * jax.nn.sigmoid(z)


def s5_glu(y, w, b, tm):
    t, k = y.shape
    return pl.pallas_call(
        _glu_kernel, grid=(t // tm,),
        in_specs=[pl.BlockSpec((tm, k), lambda i: (i, 0)), pl.BlockSpec((k, k), lambda i: (0, 0)),
                  pl.BlockSpec((1, k), lambda i: (0, 0))],
        out_specs=pl.BlockSpec((tm, k), lambda i: (i, 0)),
        out_shape=jax.ShapeDtypeStruct((t, k), F32),
        compiler_params=_params("parallel"),
        name="s5_glu",
    )(y, w, b.reshape(1, k))


def _gmlp_kernel(zu_ref, zv_ref, g_ref, b_ref, w_ref, bs_ref, o_ref, v_ref, *, lc):
    u = jax.nn.gelu(zu_ref[...])
    v = _layer_norm(jax.nn.gelu(zv_ref[...]), g_ref[...], b_ref[...])
    v_ref[...] = v
    row = lax.broadcasted_iota(jnp.int32, (lc, lc), 0)
    col = lax.broadcasted_iota(jnp.int32, (lc, lc), 1)
    causal = row >= col
    vb = v.astype(BF16)
    bs = bs_ref[...]
    for h in range(GM_HEADS):
        cols = slice(h * GM_HEAD_DIM, (h + 1) * GM_HEAD_DIM)
        w = jnp.where(causal, w_ref[h], 0.0).astype(BF16)
        mixed = jnp.dot(w, vb[:, cols], preferred_element_type=F32) + bs[:, h:h + 1]
        o_ref[:, cols] = u[:, cols] * mixed


def chunk_gmlp(proj, ln_g, ln_b, w_s, b_s, lc):
    bsz, seq = proj.shape[0], proj.shape[1]
    w = w_s[:, :lc, :lc]
    bs = b_s[:, :lc].T
    fixed2 = lambda b, n: (0, 0)
    return pl.pallas_call(
        functools.partial(_gmlp_kernel, lc=lc),
        grid=(bsz, seq // lc),
        in_specs=[pl.BlockSpec((None, lc, GM_WIDTH), lambda b, n: (b, n, 1)),
                  pl.BlockSpec((None, lc, GM_WIDTH), lambda b, n: (b, n, 2)),
                  pl.BlockSpec((1, GM_WIDTH), fixed2), pl.BlockSpec((1, GM_WIDTH), fixed2),
                  pl.BlockSpec((GM_HEADS, lc, lc), lambda b, n: (0, 0, 0)),
                  pl.BlockSpec((lc, GM_HEADS), fixed2)],
        out_specs=[pl.BlockSpec((None, lc, GM_WIDTH), lambda b, n: (b, n, 0)),
                   pl.BlockSpec((None, lc, GM_WIDTH), lambda b, n: (b, n, 0))],
        out_shape=[jax.ShapeDtypeStruct((bsz, seq, GM_WIDTH), F32),
                   jax.ShapeDtypeStruct((bsz, seq, GM_WIDTH), F32)],
        compiler_params=_params("parallel", "parallel"),
        name="chunk_gmlp",
    )(proj, proj, ln_g.reshape(1, GM_WIDTH), ln_b.reshape(1, GM_WIDTH), w, bs)


def _logf_kernel(x_ref, w_ref, b_ref, o_ref):
    z = jnp.dot(x_ref[...].astype(BF16), w_ref[...].astype(BF16), preferred_element_type=F32)
    o_ref[...] = jax.nn.log_sigmoid(z + b_ref[...])


def forget_log_gate(x, w_f, b_f, tm):
    t, k = x.shape
    return pl.pallas_call(
        _logf_kernel, grid=(t // tm,),
        in_specs=[pl.BlockSpec((tm, k), lambda i: (i, 0)), pl.BlockSpec((k, N_HEADS), lambda i: (0, 0)),
                  pl.BlockSpec((1, N_HEADS), lambda i: (0, 0))],
        out_specs=pl.BlockSpec((tm, N_HEADS), lambda i: (i, 0)),
        out_shape=jax.ShapeDtypeStruct((t, N_HEADS), F32),
        compiler_params=_params("parallel"),
        name="forget_log_gate",
    )(x, w_f, b_f.reshape(1, N_HEADS))


def _fox_prompt_kernel(q_ref, k_ref, v_ref, fq_ref, fk_ref, o_ref, m_sc, l_sc, acc_sc, *, tq):
    h = pl.program_id(1)
    qi = pl.program_id(2)
    ki = pl.program_id(3)

    @pl.when(ki == 0)
    def _():
        m_sc[...] = jnp.full_like(m_sc, NEG_INF)
        l_sc[...] = jnp.zeros_like(l_sc)
        acc_sc[...] = jnp.zeros_like(acc_sc)

    @pl.when(ki <= qi)
    def _():
        s = lax.dot_general(q_ref[...].astype(BF16), k_ref[...].astype(BF16), (((1,), (1,)), ((), ())),
                            preferred_element_type=F32) * ATTN_SCALE
        head = lax.broadcasted_iota(jnp.int32, (tq, N_HEADS), 1) == h
        fq = jnp.sum(jnp.where(head, fq_ref[...], 0.0), axis=-1, keepdims=True)
        s = s + fq - fk_ref[...]
        q_pos = qi * tq + lax.broadcasted_iota(jnp.int32, (tq, tq), 0)
        k_pos = ki * tq + lax.broadcasted_iota(jnp.int32, (tq, tq), 1)
        s = jnp.where(q_pos >= k_pos, s, NEG_INF)
        m_old = m_sc[...]
        m_new = jnp.maximum(m_old, jnp.max(s, axis=-1, keepdims=True))
        alpha = jnp.exp(m_old - m_new)
        p = jnp.exp(s - m_new)
        l_sc[...] = alpha * l_sc[...] + jnp.sum(p, axis=-1, keepdims=True)
        acc_sc[...] = alpha * acc_sc[...] + jnp.dot(p.astype(BF16), v_ref[...].astype(BF16),
                                                    preferred_element_type=F32)
        m_sc[...] = m_new

    @pl.when(ki == pl.num_programs(3) - 1)
    def _():
        o_ref[...] = acc_sc[...] / l_sc[...]


def fox_prompt(q, k, v, f_cum, tq):
    bsz, seq = q.shape[0], q.shape[1]
    nq = seq // tq
    f_rows = f_cum.transpose(0, 2, 1).reshape(bsz, N_HEADS, 1, seq)
    kv_map = lambda b, h, i, j: (b, jnp.minimum(j, i), h)
    return pl.pallas_call(
        functools.partial(_fox_prompt_kernel, tq=tq),
        grid=(bsz, N_HEADS, nq, nq),
        in_specs=[pl.BlockSpec((None, tq, HEAD_DIM), lambda b, h, i, j: (b, i, h)),
                  pl.BlockSpec((None, tq, HEAD_DIM), kv_map),
                  pl.BlockSpec((None, tq, HEAD_DIM), kv_map),
                  pl.BlockSpec((None, tq, N_HEADS), lambda b, h, i, j: (b, i, 0)),
                  pl.BlockSpec((None, None, 1, tq), lambda b, h, i, j: (b, h, 0, jnp.minimum(j, i)))],
        out_specs=pl.BlockSpec((None, tq, HEAD_DIM), lambda b, h, i, j: (b, i, h)),
        out_shape=jax.ShapeDtypeStruct((bsz, seq, D_MODEL), F32),
        scratch_shapes=[pltpu.VMEM((tq, 1), F32), pltpu.VMEM((tq, 1), F32), pltpu.VMEM((tq, HEAD_DIM), F32)],
        compiler_params=_params("parallel", "parallel", "parallel", "arbitrary"),
        name="fox_prompt",
    )(q, k, v, f_cum, f_rows)


def _fox_sample_kernel(pt_ref, q_ref, kn_ref, vn_ref, g_ref, gr_ref, kp_ref, vp_ref, lp_ref, o_ref,
                       m_sc, l_sc, acc_sc, carry_sc, *, n_new):
    step = pl.program_id(1)
    g = g_ref[...]

    def update(h, s, v_h):
        cols = slice(h * HEAD_DIM, (h + 1) * HEAD_DIM)
        m_old = m_sc[h]
        m_new = jnp.maximum(m_old, jnp.max(s, axis=-1, keepdims=True))
        alpha = jnp.exp(m_old - m_new)
        p = jnp.exp(s - m_new)
        l_sc[h] = alpha * l_sc[h] + jnp.sum(p, axis=-1, keepdims=True)
        acc_sc[:, cols] = alpha * acc_sc[:, cols] + jnp.dot(p.astype(BF16), v_h.astype(BF16),
                                                             preferred_element_type=F32)
        m_sc[h] = m_new

    def scores(h, keys):
        cols = slice(h * HEAD_DIM, (h + 1) * HEAD_DIM)
        return lax.dot_general(q_ref[:, cols].astype(BF16), keys[:, cols].astype(BF16),
                               (((1,), (1,)), ((), ())), preferred_element_type=F32) * ATTN_SCALE

    @pl.when(step == 0)
    def _():
        m_sc[...] = jnp.full_like(m_sc, NEG_INF)
        l_sc[...] = jnp.zeros_like(l_sc)
        acc_sc[...] = jnp.zeros_like(acc_sc)
        carry_sc[...] = jnp.zeros_like(carry_sc)
        causal = (lax.broadcasted_iota(jnp.int32, (n_new, n_new), 0)
                  >= lax.broadcasted_iota(jnp.int32, (n_new, n_new), 1))
        g_rows = gr_ref[...]
        for h in range(N_HEADS):
            s = scores(h, kn_ref) + g[:, h:h + 1] - g_rows[h:h + 1, :]
            update(h, jnp.where(causal, s, NEG_INF), vn_ref[:, h * HEAD_DIM:(h + 1) * HEAD_DIM])

    @pl.when(step > 0)
    def _():
        lp = lp_ref[...]
        later = (lax.broadcasted_iota(jnp.int32, (PAGE_SIZE, PAGE_SIZE), 0)
                 > lax.broadcasted_iota(jnp.int32, (PAGE_SIZE, PAGE_SIZE), 1)).astype(F32)
        suffix = jnp.dot(lp, later, precision=HIGHEST, preferred_element_type=F32) + carry_sc[...]
        carry_sc[...] += jnp.sum(lp, axis=-1, keepdims=True)
        for h in range(N_HEADS):
            s = scores(h, kp_ref) + g[:, h:h + 1] + suffix[h:h + 1, :]
            update(h, s, vp_ref[:, h * HEAD_DIM:(h + 1) * HEAD_DIM])

    @pl.when(step == pl.num_programs(1) - 1)
    def _():
        for h in range(N_HEADS):
            cols = slice(h * HEAD_DIM, (h + 1) * HEAD_DIM)
            o_ref[:, cols] = acc_sc[:, cols] / l_sc[h]


def fox_sample(q, k, v, g_new, cache_k, cache_v, cache_logf_t, page_table, layer):
    dbs, n_new = q.shape[0], q.shape[1]
    n_pages = page_table.shape[1]

    def page_map(b, s, pt):
        return (layer, pt[b, jnp.maximum(n_pages - s, 0) % n_pages], 0, 0)

    seq_map = lambda b, s, pt: (b, 0, 0)
    grid_spec = pltpu.PrefetchScalarGridSpec(
        num_scalar_prefetch=1,
        grid=(dbs, n_pages + 1),
        in_specs=[pl.BlockSpec((None, n_new, D_MODEL), seq_map),
                  pl.BlockSpec((None, n_new, D_MODEL), seq_map),
                  pl.BlockSpec((None, n_new, D_MODEL), seq_map),
                  pl.BlockSpec((None, n_new, N_HEADS), seq_map),
                  pl.BlockSpec((None, N_HEADS, n_new), seq_map),
                  pl.BlockSpec((None, None, PAGE_SIZE, D_MODEL), page_map),
                  pl.BlockSpec((None, None, PAGE_SIZE, D_MODEL), page_map),
                  pl.BlockSpec((None, None, N_HEADS, PAGE_SIZE), page_map)],
        out_specs=pl.BlockSpec((None, n_new, D_MODEL), seq_map),
        scratch_shapes=[pltpu.VMEM((N_HEADS, n_new, 1), F32), pltpu.VMEM((N_HEADS, n_new, 1), F32),
                        pltpu.VMEM((n_new, D_MODEL), F32), pltpu.VMEM((N_HEADS, 1), F32)],
    )
    return pl.pallas_call(
        functools.partial(_fox_sample_kernel, n_new=n_new),
        grid_spec=grid_spec,
        out_shape=jax.ShapeDtypeStruct((dbs, n_new, D_MODEL), F32),
        compiler_params=_params("parallel", "arbitrary"),
        name="fox_sample",
    )(page_table, q, k, v, g_new, g_new.transpose(0, 2, 1), cache_k, cache_v, cache_logf_t)


def _take_top(s, count):
    tops = []
    for i in range(count):
        m = jnp.max(s, axis=0, keepdims=True)
        tops.append(m)
        if i + 1 < count:
            s = jnp.where(s == m, NEG_INF, s)
    return tops


def _peer_route_kernel(q_ref, keys_ref, s2_ref, e2_ref, c2_ref, e1_ref, *, n_sub):
    nt = (((1,), (1,)), ((), ()))
    for sub in range(n_sub):
        tok = slice(sub * LANES, (sub + 1) * LANES) if n_sub > 1 else slice(None)
        q = q_ref[tok, :].astype(BF16)
        s1 = lax.dot_general(keys_ref[0].astype(BF16), q[:, :PEER_KEYS], nt, preferred_element_type=F32)
        s2 = lax.dot_general(keys_ref[1].astype(BF16), q[:, PEER_KEYS:], nt, preferred_element_type=F32)
        top1 = _take_top(s1, PEER_TOPK)
        top2 = _take_top(s2, PEER_TOPK)
        t2 = jnp.concatenate(top2, axis=0)
        cand = jnp.concatenate([t1 + t2 for t1 in top1], axis=0)
        best = _take_top(cand, PEER_TOPK + 1)
        z = jnp.ones_like(best[0])
        for m in best[1:PEER_TOPK]:
            z += jnp.exp(m - best[0])
        cut = 0.5 * (best[PEER_TOPK - 1] + best[PEER_TOPK])
        s2_ref[:, tok] = s2
        e2_ref[:, tok] = jnp.where(s2 >= top2[PEER_TOPK - 1], jnp.exp(s2 - top2[0]), 0.0)
        c2_ref[:, tok] = cut - s1
        e1_ref[:, tok] = jnp.where(s1 >= top1[PEER_TOPK - 1], jnp.exp(s1 - top1[0]) / z, 0.0)


def peer_route(q, sub_keys, tm):
    t = q.shape[0]
    n_sub = max(tm // LANES, 1)
    out = jax.ShapeDtypeStruct((PEER_HEADS, PEER_KEYS, t), F32)
    ospec = pl.BlockSpec((None, PEER_KEYS, tm), lambda i, h: (h, 0, i))
    return pl.pallas_call(
        functools.partial(_peer_route_kernel, n_sub=n_sub),
        grid=(t // tm, PEER_HEADS),
        in_specs=[pl.BlockSpec((tm, 2 * PEER_KEYS), lambda i, h: (i, h)),
                  pl.BlockSpec((None, 2, PEER_KEYS, PEER_KEYS), lambda i, h: (h, 0, 0, 0))],
        out_specs=[ospec, ospec, ospec, ospec],
        out_shape=[out, out, out, out],
        compiler_params=_params("parallel", "parallel"),
        name="peer_route",
    )(q, sub_keys)


def _peer_dense_kernel(x_ref, u_ref, vt_ref, s2_ref, e2_ref, c2_ref, e1_ref, o_ref, acc_sc, z_sc, g_sc, *, te, tm):
    e = pl.program_id(1)
    a_per_tile = te // PEER_KEYS
    rows_per_chunk = 2 * SUBLANES

    @pl.when(e == 0)
    def _():
        acc_sc[...] = jnp.zeros_like(acc_sc)

    z_sc[...] = lax.dot_general(u_ref[...], x_ref[...], (((1,), (1,)), ((), ())), preferred_element_type=F32)
    for al in range(a_per_tile):
        a = e * a_per_tile + al

        def chunk(c, carry, al=al, a=a):
            rows = pl.ds(pl.multiple_of(c * rows_per_chunk, rows_per_chunk), rows_per_chunk)
            gate = jnp.zeros((rows_per_chunk, tm), F32)
            for h in range(PEER_HEADS):
                keep = s2_ref[h, rows, :] >= c2_ref[h, pl.ds(a, 1), :]
                gate += jnp.where(keep, e2_ref[h, rows, :] * e1_ref[h, pl.ds(a, 1), :], 0.0)
            zrows = pl.ds(pl.multiple_of(al * PEER_KEYS + c * rows_per_chunk, rows_per_chunk), rows_per_chunk)
            g_sc[zrows, :] = (gate * jax.nn.gelu(z_sc[zrows, :])).astype(BF16)
            return carry

        lax.fori_loop(0, PEER_KEYS // rows_per_chunk, chunk, 0)
    acc_sc[...] += jnp.dot(vt_ref[...], g_sc[...], preferred_element_type=F32)

    @pl.when(e == pl.num_programs(1) - 1)
    def _():
        o_ref[...] = acc_sc[...].T


def peer_dense(x_bf, u_bf, vt_bf, routes, tm, te):
    t, d = x_bf.shape
    n_exp = u_bf.shape[0]
    rspec = pl.BlockSpec((PEER_HEADS, PEER_KEYS, tm), lambda i, e: (0, 0, i))
    return pl.pallas_call(
        functools.partial(_peer_dense_kernel, te=te, tm=tm),
        grid=(t // tm, n_exp // te),
        in_specs=[pl.BlockSpec((tm, d), lambda i, e: (i, 0)),
                  pl.BlockSpec((te, d), lambda i, e: (e, 0)),
                  pl.BlockSpec((d, te), lambda i, e: (0, e)),
                  rspec, rspec, rspec, rspec],
        out_specs=pl.BlockSpec((tm, d), lambda i, e: (i, 0)),
        out_shape=jax.ShapeDtypeStruct((t, d), F32),
        scratch_shapes=[pltpu.VMEM((d, tm), F32), pltpu.VMEM((te, tm), F32), pltpu.VMEM((te, tm), BF16)],
        compiler_params=_params("parallel", "arbitrary"),
        name="peer_dense",
    )(x_bf, u_bf, vt_bf, *routes)


def peer(x, w_q, sub_keys, u_bf, vt_bf, tm_mm, tm_route, tm_dense, te):
    q = matmul(x, w_q, 0, w_q.shape[1], 512, tm_mm)
    routes = peer_route(q, sub_keys, tm_route)
    return peer_dense(x.astype(BF16), u_bf, vt_bf, routes, tm_dense, te)


def _row_tile(t, cap):
    tm = min(t, cap)
    assert t % tm == 0
    return tm


def kernel(x_prompt, x_sample, cache_k, cache_v, cache_logf, page_table, state_s5_re, state_s5_im, w_in_even, s5_lambda_re, s5_lambda_im, s5_log_dt, s5_b_re, s5_b_im, s5_c_re, s5_c_im, s5_d, s5_w_glu, s5_b_glu, gm_ln_g, gm_ln_b, gm_w_s, gm_b_s, w_out_even, w_in_odd, b_f, w_out_odd, ln1_g, ln1_b, ln2_g, ln2_b, peer_w_q, peer_sub_keys, peer_u, peer_v):
    bsz, seq, d = x_prompt.shape
    dbs, dseq, _ = x_sample.shape
    n_odd, n_phys = cache_k.shape[0], cache_k.shape[1]
    tp, ts = bsz * seq, dbs * dseq
    xp = x_prompt.reshape(tp, d)
    xs = x_sample.reshape(ts, d)
    tm_p = _row_tile(tp, 512)
    tm_s = _row_tile(ts, 512)
    ck = cache_k.reshape(n_odd, n_phys, PAGE_SIZE, d)
    cv = cache_v.reshape(n_odd, n_phys, PAGE_SIZE, d)
    clf = cache_logf.transpose(0, 1, 3, 2)
    zero_state = jnp.zeros((bsz, S5_GROUPS, S5_STATE), F32)
    peer_rep = max(LANES // ts, 1)

    outs = {k: [] for k in ("kp", "vp", "lp", "ks", "vs", "ls", "s5p_re", "s5p_im", "s5s_re", "s5s_im", "gmv")}
    for li in range(DEPTH):
        j = li // 2
        g1, b1 = ln1_g[li].reshape(1, d), ln1_b[li].reshape(1, d)
        g2, b2 = ln2_g[li].reshape(1, d), ln2_b[li].reshape(1, d)
        if li % 2 == 0:
            tables = _s5_tables(s5_lambda_re[j], s5_lambda_im[j], s5_log_dt[j], s5_b_re[j], s5_b_im[j],
                                s5_c_re[j], s5_c_im[j])
            n_in = w_in_even.shape[2]

            def even(x, nb, nl, h0_re, h0_im, tm):
                proj = matmul(x, w_in_even[j], 0, n_in, 512, tm).reshape(nb, nl, n_in)
                lc = min(nl, 256)
                y, hl_re, hl_im = s5_scan(proj, h0_re, h0_im, tables, s5_d[j], lc)
                a_out = s5_glu(y.reshape(nb * nl, S5_WIDTH), s5_w_glu[j], s5_b_glu[j], tm)
                g_out, v_rows = chunk_gmlp(proj, gm_ln_g[j], gm_ln_b[j], gm_w_s[j], gm_b_s[j], min(nl, GM_CHUNK))
                x = out_proj_dnln(a_out, g_out.reshape(nb * nl, GM_WIDTH), w_out_even[j], x, g1, b1, min(tm, 256))
                return x, hl_re, hl_im, v_rows

            xp, hp_re, hp_im, _ = even(xp, bsz, seq, zero_state, zero_state, tm_p)
            xs, hs_re, hs_im, v_rows = even(xs, dbs, dseq, state_s5_re[j].astype(F32),
                                            state_s5_im[j].astype(F32), tm_s)
            outs["s5p_re"].append(hp_re)
            outs["s5p_im"].append(hp_im)
            outs["s5s_re"].append(hs_re)
            outs["s5s_im"].append(hs_im)
            outs["gmv"].append(v_rows)
        else:
            w_f = w_in_odd[j][:, 3 * d:]

            def project(x, tm):
                q = matmul(x, w_in_odd[j], 0, d, 512, tm)
                k = matmul(x, w_in_odd[j], d // 512, d, 512, tm)
                v = matmul(x, w_in_odd[j], 2 * d // 512, d, 512, tm)
                return q, k, v, forget_log_gate(x, w_f, b_f[j], tm)

            qp, kp, vp, lfp = project(xp, tm_p)
            f_cum = jnp.cumsum(lfp.reshape(bsz, seq, N_HEADS), axis=1)
            ap = fox_prompt(qp.reshape(bsz, seq, d), kp.reshape(bsz, seq, d), vp.reshape(bsz, seq, d), f_cum, 512)
            xp = out_proj_dnln(ap.reshape(tp, d), None, w_out_odd[j], xp, g1, b1, min(tm_p, 256))
            qs, kss, vss, lfs = project(xs, tm_s)
            g_new = jnp.cumsum(lfs.reshape(dbs, dseq, N_HEADS), axis=1)
            a_s = fox_sample(qs.reshape(dbs, dseq, d), kss.reshape(dbs, dseq, d), vss.reshape(dbs, dseq, d),
                             g_new, ck, cv, clf, page_table, j)
            xs = out_proj_dnln(a_s.reshape(ts, d), None, w_out_odd[j], xs, g1, b1, min(tm_s, 256))
            outs["kp"].append(kp.reshape(bsz, seq, N_HEADS, HEAD_DIM))
            outs["vp"].append(vp.reshape(bsz, seq, N_HEADS, HEAD_DIM))
            outs["lp"].append(lfp.reshape(bsz, seq, N_HEADS))
            outs["ks"].append(kss.reshape(dbs, dseq, N_HEADS, HEAD_DIM))
            outs["vs"].append(vss.reshape(dbs, dseq, N_HEADS, HEAD_DIM))
            outs["ls"].append(lfs.reshape(dbs, dseq, N_HEADS))
        u_bf = peer_u[li].astype(BF16)
        vt_bf = peer_v[li].T.astype(BF16)
        mp = peer(xp, peer_w_q[li], peer_sub_keys[li], u_bf, vt_bf, tm_p, 256, 512, 512)
        xp = dn_layer_norm(xp, mp, g2, b2, tm_p)
        xs_rep = jnp.tile(xs, (peer_rep, 1))
        t_rep = xs_rep.shape[0]
        ms = peer(xs_rep, peer_w_q[li], peer_sub_keys[li], u_bf, vt_bf, t_rep, t_rep, t_rep, 512)[:ts]
        xs = dn_layer_norm(xs, ms, g2, b2, tm_s)

    stack = lambda name, dtype: jnp.stack(outs[name]).astype(dtype)
    return (xp.reshape(bsz, seq, d), xs.reshape(dbs, dseq, d),
            stack("kp", cache_k.dtype), stack("vp", cache_v.dtype), stack("lp", cache_logf.dtype),
            stack("ks", cache_k.dtype), stack("vs", cache_v.dtype), stack("ls", cache_logf.dtype),
            stack("s5p_re", state_s5_re.dtype), stack("s5p_im", state_s5_im.dtype),
            stack("s5s_re", state_s5_re.dtype), stack("s5s_im", state_s5_im.dtype),
            jnp.stack(outs["gmv"]))
```

```python
import functools
import math

import jax
import jax.numpy as jnp
from jax import lax
from jax.experimental import pallas as pl
from jax.experimental.pallas import tpu as pltpu

F32 = jnp.float32
BF16 = jnp.bfloat16
HIGHEST = lax.Precision.HIGHEST

D_MODEL = 2048
DEPTH = 4
PAGE_SIZE = 128
S5_WIDTH = 1024
S5_GROUP = 16
S5_GROUPS = 64
S5_STATE = 64
S5_GROUP_BLOCK = 8
S5_BLOCKS = S5_GROUPS // S5_GROUP_BLOCK
S5_BLOCK_STATE = S5_GROUP_BLOCK * S5_STATE
GM_WIDTH = 1024
GM_CHUNK = 128
GM_HEADS = 8
GM_HEAD_DIM = 128
N_HEADS = 16
HEAD_DIM = 128
ATTN_SCALE = HEAD_DIM ** -0.5
TOKEN_TILE = 512
S5_CHUNK = 256
PEER_EXPERT_TILE = 512
PEER_ROUTE_TILE = 256
PEER_STAGE_PIECES = 8
FOX_BLOCK = 512
FOX_ROW_CHUNK = 128
FOX_PAGES_PER_STEP = 4
PEER_HEADS = 8
PEER_KEYS = 128
PEER_TOPK = 16
DN_ALPHA = (2.0 * DEPTH) ** 0.25
LN_EPS = 1e-5
SUBLANES = 8
LANES = 128
VMEM_LIMIT = 56 * 1024 * 1024
NEG_INF = float("-inf")


def _params(*sem):
    return pltpu.CompilerParams(dimension_semantics=sem, vmem_limit_bytes=VMEM_LIMIT)


def _layer_norm(y, g, b):
    mu = jnp.mean(y, axis=-1, keepdims=True)
    yc = y - mu
    var = jnp.mean(yc * yc, axis=-1, keepdims=True)
    return yc * lax.rsqrt(var + LN_EPS) * g + b


def _mm_kernel(x_ref, w_ref, o_ref):
    o_ref[...] = jnp.dot(x_ref[...].astype(BF16), w_ref[...], preferred_element_type=F32)


def matmul(x, w_bf, col_block, n_out, tm):
    t, k = x.shape
    assert t % tm == 0
    return pl.pallas_call(
        _mm_kernel,
        grid=(t // tm,),
        in_specs=[pl.BlockSpec((tm, k), lambda i: (i, 0)),
                  pl.BlockSpec((k, n_out), lambda i: (0, col_block))],
        out_specs=pl.BlockSpec((tm, n_out), lambda i: (i, 0)),
        out_shape=jax.ShapeDtypeStruct((t, n_out), F32),
        compiler_params=_params("parallel"),
        name="matmul",
    )(x, w_bf)


def _store_norm(y, g_ref, beta_ref, o_ref, ob_ref):
    out = _layer_norm(y, g_ref[...], beta_ref[...])
    o_ref[...] = out
    ob_ref[...] = out.astype(BF16)


def _mm1_dnln_kernel(a_ref, w_ref, x_ref, g_ref, beta_ref, o_ref, ob_ref):
    m = jnp.dot(a_ref[...].astype(BF16), w_ref[...], preferred_element_type=F32)
    _store_norm(DN_ALPHA * x_ref[...] + m, g_ref, beta_ref, o_ref, ob_ref)


def _mm2_dnln_kernel(a_ref, b_ref, wa_ref, wb_ref, x_ref, g_ref, beta_ref, o_ref, ob_ref):
    m = jnp.dot(a_ref[...].astype(BF16), wa_ref[...], preferred_element_type=F32)
    m += jnp.dot(b_ref[...].astype(BF16), wb_ref[...], preferred_element_type=F32)
    _store_norm(DN_ALPHA * x_ref[...] + m, g_ref, beta_ref, o_ref, ob_ref)


def _norm_outputs(t, d, tm):
    row = lambda i: (i, 0)
    return ([pl.BlockSpec((tm, d), row), pl.BlockSpec((tm, d), row)],
            [jax.ShapeDtypeStruct((t, d), F32), jax.ShapeDtypeStruct((t, d), BF16)])


def out_proj_dnln(a, b, w_bf, x, g, beta, tm):
    t, ka = a.shape
    d = w_bf.shape[1]
    row = lambda i: (i, 0)
    fixed = lambda i: (0, 0)
    tail = [pl.BlockSpec((tm, d), row), pl.BlockSpec((1, d), fixed), pl.BlockSpec((1, d), fixed)]
    if b is None:
        kern = _mm1_dnln_kernel
        ins = [a, w_bf, x, g, beta]
        specs = [pl.BlockSpec((tm, ka), row), pl.BlockSpec((ka, d), fixed)] + tail
    else:
        assert b.shape[1] == ka
        kern = _mm2_dnln_kernel
        ins = [a, b, w_bf, w_bf, x, g, beta]
        specs = [pl.BlockSpec((tm, ka), row), pl.BlockSpec((tm, ka), row),
                 pl.BlockSpec((ka, d), fixed), pl.BlockSpec((ka, d), lambda i: (1, 0))] + tail
    out_specs, out_shape = _norm_outputs(t, d, tm)
    return pl.pallas_call(
        kern, grid=(t // tm,), in_specs=specs, out_specs=out_specs, out_shape=out_shape,
        compiler_params=_params("parallel"),
        name="out_proj_dnln",
    )(*ins)


def _dnln_kernel(x_ref, m_ref, g_ref, b_ref, o_ref, ob_ref):
    _store_norm(DN_ALPHA * x_ref[...] + m_ref[...], g_ref, b_ref, o_ref, ob_ref)


def dn_layer_norm(x, m, g, b, tm):
    t, d = x.shape
    row = lambda i: (i, 0)
    fixed = lambda i: (0, 0)
    out_specs, out_shape = _norm_outputs(t, d, tm)
    return pl.pallas_call(
        _dnln_kernel, grid=(t // tm,),
        in_specs=[pl.BlockSpec((tm, d), row), pl.BlockSpec((tm, d), row),
                  pl.BlockSpec((1, d), fixed), pl.BlockSpec((1, d), fixed)],
        out_specs=out_specs, out_shape=out_shape,
        compiler_params=_params("parallel"),
        name="dn_layer_norm",
    )(x, m, g, b)


def _split_bf16(x):
    hi = lax.bitcast_convert_type(lax.bitcast_convert_type(x, jnp.uint32) & jnp.uint32(0xFFFF0000), F32)
    return hi.astype(BF16), (x - hi).astype(BF16)


def _dot_split(a, b_ref):
    a_hi, a_lo = _split_bf16(a)
    dot = functools.partial(jnp.dot, preferred_element_type=F32)
    return dot(a_hi, b_ref[0]) + (dot(a_lo, b_ref[0]) + dot(a_hi, b_ref[1]))


def _s5_kernel(u_ref, bd_ref, cd_ref, tab_ref, d_ref, h0_ref, y_ref, hl_ref, hs_sc, carry_sc, *, lc):
    ns = S5_BLOCK_STATE

    @pl.when(pl.program_id(2) == 0)
    def _():
        carry_sc[...] = h0_ref[...]

    u = u_ref[...]
    hs_sc[...] = _dot_split(u, bd_ref)

    def row_group(r, carry):
        cr, ci = carry
        rows = pl.ds(pl.multiple_of(r * SUBLANES, SUBLANES), SUBLANES)
        xr = hs_sc[rows, :ns]
        xi = hs_sc[rows, ns:]
        for k in range(3):
            ar = tab_ref[2 * k]
            ai = tab_ref[2 * k + 1]
            sr = pltpu.roll(xr, 1 << k, 0)
            si = pltpu.roll(xi, 1 << k, 0)
            xr, xi = xr + (ar * sr - ai * si), xi + (ar * si + ai * sr)
        pr = tab_ref[6]
        pi = tab_ref[7]
        xr, xi = xr + (pr * cr - pi * ci), xi + (pr * ci + pi * cr)
        hs_sc[rows, :ns] = xr
        hs_sc[rows, ns:] = xi
        return xr[SUBLANES - 1:, :], xi[SUBLANES - 1:, :]

    cr, ci = lax.fori_loop(0, lc // SUBLANES, row_group, (carry_sc[:, :ns], carry_sc[:, ns:]))
    carry_sc[:, :ns] = cr
    carry_sc[:, ns:] = ci
    hl_ref[...] = carry_sc[...]
    y_ref[...] = _dot_split(hs_sc[...], cd_ref) + d_ref[...] * u


def _s5_tables(lam_re, lam_im, log_dt, b_re, b_im, c_re, c_im):
    dt = jnp.exp(log_dt.astype(F32))[:, None]
    mag = jnp.exp(lam_re * dt)
    lbr = mag * jnp.cos(lam_im * dt)
    lbi = mag * jnp.sin(lam_im * dt)
    den = lam_re * lam_re + lam_im * lam_im
    qr = ((lbr - 1.0) * lam_re + lbi * lam_im) / den
    qi = (lbi * lam_re - (lbr - 1.0) * lam_im) / den
    bbr = qr[..., None] * b_re - qi[..., None] * b_im
    bbi = qr[..., None] * b_im + qi[..., None] * b_re
    nb, gb = S5_BLOCKS, S5_GROUP_BLOCK
    eye = jnp.eye(gb, dtype=F32)

    def bdiag(m):
        m = m.reshape(nb, gb, S5_STATE, S5_GROUP)
        return jnp.einsum('ngpc,gh->ngchp', m, eye).reshape(nb, gb * S5_GROUP, gb * S5_STATE)

    def cdiag(m):
        m = m.reshape(nb, gb, S5_GROUP, S5_STATE)
        return jnp.einsum('ngcp,gh->ngphc', m, eye).reshape(nb, gb * S5_STATE, gb * S5_GROUP)

    bd = jnp.stack(_split_bf16(jnp.concatenate([bdiag(bbr), bdiag(bbi)], axis=-1)), axis=1)
    cd = jnp.stack(_split_bf16(jnp.concatenate([cdiag(c_re), cdiag(-c_im)], axis=1)), axis=1)

    def cmul(a, b):
        return a[0] * b[0] - a[1] * b[1], a[0] * b[1] + a[1] * b[0]

    powers = [(lbr, lbi)]
    for _ in range(SUBLANES - 1):
        powers.append(cmul(powers[-1], (lbr, lbi)))
    row = jnp.arange(SUBLANES)[:, None, None]
    tabs = []
    for k in range(3):
        sh = 1 << k
        for part in powers[sh - 1]:
            tabs.append(jnp.where(row >= sh, part[None], 0.0))
    tabs.append(jnp.stack([p[0] for p in powers]))
    tabs.append(jnp.stack([p[1] for p in powers]))
    tab = jnp.stack(tabs)
    tab = tab.reshape(8, SUBLANES, nb, S5_BLOCK_STATE).transpose(2, 0, 1, 3)
    return bd, cd, tab


def s5_scan(proj, h0_re, h0_im, tables, d_skip, lc):
    bd, cd, tab = tables
    bsz, seq = proj.shape[0], proj.shape[1]
    nb, ns = S5_BLOCKS, S5_BLOCK_STATE
    h0 = jnp.concatenate([h0_re.reshape(bsz, nb, 1, ns), h0_im.reshape(bsz, nb, 1, ns)], axis=-1)
    y, hl = pl.pallas_call(
        functools.partial(_s5_kernel, lc=lc),
        grid=(bsz, nb, seq // lc),
        in_specs=[pl.BlockSpec((None, lc, LANES), lambda b, g, l: (b, l, g)),
                  pl.BlockSpec((None, 2, LANES, 2 * ns), lambda b, g, l: (g, 0, 0, 0)),
                  pl.BlockSpec((None, 2, 2 * ns, LANES), lambda b, g, l: (g, 0, 0, 0)),
                  pl.BlockSpec((None, 8, SUBLANES, ns), lambda b, g, l: (g, 0, 0, 0)),
                  pl.BlockSpec((1, LANES), lambda b, g, l: (0, g)),
                  pl.BlockSpec((None, None, 1, 2 * ns), lambda b, g, l: (b, g, 0, 0))],
        out_specs=[pl.BlockSpec((None, lc, LANES), lambda b, g, l: (b, l, g)),
                   pl.BlockSpec((None, None, 1, 2 * ns), lambda b, g, l: (b, g, 0, 0))],
        out_shape=[jax.ShapeDtypeStruct((bsz, seq, S5_WIDTH), F32),
                   jax.ShapeDtypeStruct((bsz, nb, 1, 2 * ns), F32)],
        scratch_shapes=[pltpu.VMEM((lc, 2 * ns), F32), pltpu.VMEM((1, 2 * ns), F32)],
        compiler_params=_params("parallel", "parallel", "arbitrary"),
        name="s5_scan",
    )(proj, bd, cd, tab, d_skip.reshape(1, S5_WIDTH), h0)
    hl_re = hl[..., :ns].reshape(bsz, S5_GROUPS, S5_STATE)
    hl_im = hl[..., ns:].reshape(bsz, S5_GROUPS, S5_STATE)
    return y, hl_re, hl_im


def _glu_kernel(y_ref, w_ref, b_ref, o_ref):
    y = jax.nn.gelu(y_ref[...])
    z = jnp.dot(y.astype(BF16), w_ref[...], preferred_element_type=F32) + b_ref[...]
    o_ref[...] = y * jax.nn.sigmoid(z)


def s5_glu(y, w, b, tm):
    t, k = y.shape
    return pl.pallas_call(
        _glu_kernel, grid=(t // tm,),
        in_specs=[pl.BlockSpec((tm, k), lambda i: (i, 0)), pl.BlockSpec((k, k), lambda i: (0, 0)),
                  pl.BlockSpec((1, k), lambda i: (0, 0))],
        out_specs=pl.BlockSpec((tm, k), lambda i: (i, 0)),
        out_shape=jax.ShapeDtypeStruct((t, k), F32),
        compiler_params=_params("parallel"),
        name="s5_glu",
    )(y, w, b.reshape(1, k))


def _gmlp_kernel(zu_ref, zv_ref, g_ref, b_ref, w_ref, bs_ref, o_ref, v_ref, *, lc):
    u = jax.nn.gelu(zu_ref[...])
    v = _layer_norm(jax.nn.gelu(zv_ref[...]), g_ref[...], b_ref[...])
    v_ref[...] = v
    row = lax.broadcasted_iota(jnp.int32, (lc, lc), 0)
    col = lax.broadcasted_iota(jnp.int32, (lc, lc), 1)
    causal = row >= col
    vb = v.astype(BF16)
    bs = bs_ref[...]
    for h in range(GM_HEADS):
        cols = slice(h * GM_HEAD_DIM, (h + 1) * GM_HEAD_DIM)
        w = jnp.where(causal, w_ref[h], 0.0).astype(BF16)
        mixed = jnp.dot(w, vb[:, cols], preferred_element_type=F32) + bs[:, h:h + 1]
        o_ref[:, cols] = u[:, cols] * mixed


def chunk_gmlp(proj, ln_g, ln_b, w_s, b_s, lc):
    bsz, seq = proj.shape[0], proj.shape[1]
    w = w_s[:, :lc, :lc]
    bs = b_s[:, :lc].T
    fixed2 = lambda b, n: (0, 0)
    return pl.pallas_call(
        functools.partial(_gmlp_kernel, lc=lc),
        grid=(bsz, seq // lc),
        in_specs=[pl.BlockSpec((None, lc, GM_WIDTH), lambda b, n: (b, n, 1)),
                  pl.BlockSpec((None, lc, GM_WIDTH), lambda b, n: (b, n, 2)),
                  pl.BlockSpec((1, GM_WIDTH), fixed2), pl.BlockSpec((1, GM_WIDTH), fixed2),
                  pl.BlockSpec((GM_HEADS, lc, lc), lambda b, n: (0, 0, 0)),
                  pl.BlockSpec((lc, GM_HEADS), fixed2)],
        out_specs=[pl.BlockSpec((None, lc, GM_WIDTH), lambda b, n: (b, n, 0)),
                   pl.BlockSpec((None, lc, GM_WIDTH), lambda b, n: (b, n, 0))],
        out_shape=[jax.ShapeDtypeStruct((bsz, seq, GM_WIDTH), F32),
                   jax.ShapeDtypeStruct((bsz, seq, GM_WIDTH), F32)],
        compiler_params=_params("parallel", "parallel"),
        name="chunk_gmlp",
    )(proj, proj, ln_g.reshape(1, GM_WIDTH), ln_b.reshape(1, GM_WIDTH), w, bs)


def _logf_kernel(x_ref, w_ref, b_ref, o_ref):
    z = jnp.dot(x_ref[...].astype(BF16), w_ref[...].astype(BF16), preferred_element_type=F32)
    o_ref[...] = jax.nn.log_sigmoid(z + b_ref[...])


def forget_log_gate(x, w_f, b_f, tm):
    t, k = x.shape
    return pl.pallas_call(
        _logf_kernel, grid=(t // tm,),
        in_specs=[pl.BlockSpec((tm, k), lambda i: (i, 0)), pl.BlockSpec((k, N_HEADS), lambda i: (0, 0)),
                  pl.BlockSpec((1, N_HEADS), lambda i: (0, 0))],
        out_specs=pl.BlockSpec((tm, N_HEADS), lambda i: (i, 0)),
        out_shape=jax.ShapeDtypeStruct((t, N_HEADS), F32),
        compiler_params=_params("parallel"),
        name="forget_log_gate",
    )(x, w_f, b_f.reshape(1, N_HEADS))


def _fox_prompt_kernel(q_ref, k_ref, v_ref, fq_ref, fk_ref, o_ref, m_sc, l_sc, acc_sc, *, tq):
    h = pl.program_id(1)
    qi = pl.program_id(2)
    ki = pl.program_id(3)

    @pl.when(ki == 0)
    def _():
        m_sc[...] = jnp.full_like(m_sc, NEG_INF)
        l_sc[...] = jnp.zeros_like(l_sc)
        acc_sc[...] = jnp.zeros_like(acc_sc)

    def block(on_diagonal):
        kb = k_ref[...].astype(BF16)
        vb = v_ref[...].astype(BF16)
        fk = fk_ref[...]
        chunk_rows = [slice(c * FOX_ROW_CHUNK, (c + 1) * FOX_ROW_CHUNK) for c in range(tq // FOX_ROW_CHUNK)]
        scores = [lax.dot_general((q_ref[rows, :] * ATTN_SCALE).astype(BF16), kb, (((1,), (1,)), ((), ())),
                                  preferred_element_type=F32) for rows in chunk_rows]
        for c, rows in enumerate(chunk_rows):
            s = scores[c]
            head = lax.broadcasted_iota(jnp.int32, (FOX_ROW_CHUNK, N_HEADS), 1) == h
            fq = jnp.sum(jnp.where(head, fq_ref[rows, :], 0.0), axis=-1, keepdims=True)
            s = s + fq - fk
            if on_diagonal:
                causal = (c * FOX_ROW_CHUNK + lax.broadcasted_iota(jnp.int32, (FOX_ROW_CHUNK, tq), 0)
                          >= lax.broadcasted_iota(jnp.int32, (FOX_ROW_CHUNK, tq), 1))
                s = jnp.where(causal, s, NEG_INF)
            m_old = m_sc[rows, :]
            m_new = jnp.maximum(m_old, jnp.max(s, axis=-1, keepdims=True))
            alpha = jnp.exp(m_old - m_new)
            p = jnp.exp(s - m_new)
            l_sc[rows, :] = alpha * l_sc[rows, :] + jnp.sum(p, axis=-1, keepdims=True)
            acc_sc[rows, :] = alpha * acc_sc[rows, :] + jnp.dot(p.astype(BF16), vb, preferred_element_type=F32)
            m_sc[rows, :] = m_new

    @pl.when(ki < qi)
    def _():
        block(False)

    @pl.when(ki == qi)
    def _():
        block(True)

    @pl.when(ki == pl.num_programs(3) - 1)
    def _():
        o_ref[...] = acc_sc[...] / l_sc[...]


def fox_prompt(q, k, v, f_cum, tq):
    bsz, seq = q.shape[0], q.shape[1]
    nq = seq // tq
    f_rows = f_cum.transpose(0, 2, 1).reshape(bsz, N_HEADS, 1, seq)
    kv_map = lambda b, h, i, j: (b, jnp.minimum(j, i), h)
    return pl.pallas_call(
        functools.partial(_fox_prompt_kernel, tq=tq),
        grid=(bsz, N_HEADS, nq, nq),
        in_specs=[pl.BlockSpec((None, tq, HEAD_DIM), lambda b, h, i, j: (b, i, h)),
                  pl.BlockSpec((None, tq, HEAD_DIM), kv_map),
                  pl.BlockSpec((None, tq, HEAD_DIM), kv_map),
                  pl.BlockSpec((None, tq, N_HEADS), lambda b, h, i, j: (b, i, 0)),
                  pl.BlockSpec((None, None, 1, tq), lambda b, h, i, j: (b, h, 0, jnp.minimum(j, i)))],
        out_specs=pl.BlockSpec((None, tq, HEAD_DIM), lambda b, h, i, j: (b, i, h)),
        out_shape=jax.ShapeDtypeStruct((bsz, seq, D_MODEL), F32),
        scratch_shapes=[pltpu.VMEM((tq, 1), F32), pltpu.VMEM((tq, 1), F32), pltpu.VMEM((tq, HEAD_DIM), F32)],
        compiler_params=_params("parallel", "parallel", "parallel", "arbitrary"),
        name="fox_prompt",
    )(q, k, v, f_cum, f_rows)


def _fox_sample_kernel(pt_ref, q_ref, kn_ref, vn_ref, gc_ref, gr_ref, *rest, n_new, pps):
    kp_refs, vp_refs, lp_refs = rest[:pps], rest[pps:2 * pps], rest[2 * pps:3 * pps]
    o_ref, m_sc, l_sc, acc_sc, carry_sc = rest[3 * pps:]
    step = pl.program_id(1)
    nt = (((1,), (1,)), ((), ()))
    head_rows = lambda h: slice(h * n_new, (h + 1) * n_new)
    head_cols = lambda h: slice(h * HEAD_DIM, (h + 1) * HEAD_DIM)
    q_heads = [(q_ref[:, head_cols(h)] * ATTN_SCALE).astype(BF16) for h in range(N_HEADS)]

    def online_update(s, values):
        m_old = m_sc[...]
        m_new = jnp.maximum(m_old, jnp.max(s, axis=-1, keepdims=True))
        alpha = jnp.exp(m_old - m_new)
        p = jnp.exp(s - m_new)
        l_sc[...] = alpha * l_sc[...] + jnp.sum(p, axis=-1, keepdims=True)
        pv = [jnp.dot(p[head_rows(h), :].astype(BF16), values(h), preferred_element_type=F32)
              for h in range(N_HEADS)]
        acc_sc[...] = alpha * acc_sc[...] + jnp.concatenate(pv, axis=0)
        m_sc[...] = m_new

    @pl.when(step == 0)
    def _():
        m_sc[...] = jnp.full_like(m_sc, NEG_INF)
        l_sc[...] = jnp.zeros_like(l_sc)
        acc_sc[...] = jnp.zeros_like(acc_sc)
        carry_sc[...] = jnp.zeros_like(carry_sc)
        s = jnp.concatenate([lax.dot_general(q_heads[h], kn_ref[:, head_cols(h)].astype(BF16), nt,
                                             preferred_element_type=F32) for h in range(N_HEADS)], axis=0)
        s = s + gc_ref[...] - gr_ref[...]
        rows = N_HEADS * n_new
        causal = (lax.broadcasted_iota(jnp.int32, (rows, n_new), 0) % n_new
                  >= lax.broadcasted_iota(jnp.int32, (rows, n_new), 1))
        online_update(jnp.where(causal, s, NEG_INF), lambda h: vn_ref[:, head_cols(h)].astype(BF16))

    @pl.when(step > 0)
    def _():
        later = (lax.broadcasted_iota(jnp.int32, (PAGE_SIZE, PAGE_SIZE), 0)
                 > lax.broadcasted_iota(jnp.int32, (PAGE_SIZE, PAGE_SIZE), 1)).astype(F32)
        carry = carry_sc[...]
        page_scores = []
        for pg in range(pps):
            lp = lp_refs[pg][...]
            suffix = jnp.dot(lp, later, precision=HIGHEST, preferred_element_type=F32) + carry
            carry = carry + jnp.sum(lp, axis=-1, keepdims=True)
            qk = [lax.dot_general(q_heads[h], kp_refs[pg][pl.ds(h, PAGE_SIZE, stride=N_HEADS), :].astype(BF16),
                                  nt, preferred_element_type=F32)
                  + jnp.broadcast_to(suffix[h:h + 1, :], (n_new, PAGE_SIZE)) for h in range(N_HEADS)]
            page_scores.append(jnp.concatenate(qk, axis=0))
        carry_sc[...] = carry
        s = jnp.concatenate(page_scores, axis=1) + gc_ref[...]

        def values(h):
            return jnp.concatenate([vp_refs[pg][pl.ds(h, PAGE_SIZE, stride=N_HEADS), :].astype(BF16)
                                    for pg in range(pps)], axis=0)

        online_update(s, values)

    @pl.when(step == pl.num_programs(1) - 1)
    def _():
        out = acc_sc[...] / l_sc[...]
        for h in range(N_HEADS):
            o_ref[:, head_cols(h)] = out[head_rows(h), :]


def fox_sample(q, k, v, g_new, cache_k, cache_v, cache_logf_t, page_table, layer):
    dbs, n_new = q.shape[0], q.shape[1]
    n_pages = page_table.shape[1]
    pps = math.gcd(n_pages, FOX_PAGES_PER_STEP)
    rows = N_HEADS * n_new
    g_t = g_new.transpose(0, 2, 1)
    g_col = g_t.reshape(dbs, rows, 1)
    g_row = jnp.repeat(g_t, n_new, axis=1)

    def page_map(pg):
        def index(b, s, pt):
            first = jnp.maximum(s - 1, 0) * pps
            return (layer, pt[b, n_pages - 1 - (first + pg)], 0, 0)
        return index

    seq_map = lambda b, s, pt: (b, 0, 0)
    page_specs = lambda shape: [pl.BlockSpec((None, None) + shape, page_map(pg)) for pg in range(pps)]
    grid_spec = pltpu.PrefetchScalarGridSpec(
        num_scalar_prefetch=1,
        grid=(dbs, n_pages // pps + 1),
        in_specs=([pl.BlockSpec((None, n_new, D_MODEL), seq_map)] * 3
                  + [pl.BlockSpec((None, rows, 1), seq_map), pl.BlockSpec((None, rows, n_new), seq_map)]
                  + page_specs((PAGE_SIZE * N_HEADS, HEAD_DIM)) * 2
                  + page_specs((N_HEADS, PAGE_SIZE))),
        out_specs=pl.BlockSpec((None, n_new, D_MODEL), seq_map),
        scratch_shapes=[pltpu.VMEM((rows, 1), F32), pltpu.VMEM((rows, 1), F32),
                        pltpu.VMEM((rows, HEAD_DIM), F32), pltpu.VMEM((N_HEADS, 1), F32)],
    )
    return pl.pallas_call(
        functools.partial(_fox_sample_kernel, n_new=n_new, pps=pps),
        grid_spec=grid_spec,
        out_shape=jax.ShapeDtypeStruct((dbs, n_new, D_MODEL), F32),
        compiler_params=_params("parallel", "arbitrary"),
        name="fox_sample",
    )(page_table, q, k, v, g_col, g_row, *([cache_k] * pps), *([cache_v] * pps), *([cache_logf_t] * pps))


def _take_top(s, count):
    tops = []
    for i in range(count):
        m = jnp.max(s, axis=0, keepdims=True)
        tops.append(m)
        if i + 1 < count:
            s = jnp.where(s == m, NEG_INF, s)
    return tops


def _peer_route_kernel(q_ref, keys_ref, s2_ref, e2_ref, c2_ref, e1_ref, *, n_sub):
    nt = (((1,), (1,)), ((), ()))
    for sub in range(n_sub):
        tok = slice(sub * LANES, (sub + 1) * LANES) if n_sub > 1 else slice(None)
        q = q_ref[tok, :].astype(BF16)
        s1 = lax.dot_general(keys_ref[0].astype(BF16), q[:, :PEER_KEYS], nt, preferred_element_type=F32)
        s2 = lax.dot_general(keys_ref[1].astype(BF16), q[:, PEER_KEYS:], nt, preferred_element_type=F32)
        top1 = _take_top(s1, PEER_TOPK)
        top2 = _take_top(s2, PEER_TOPK)
        t2 = jnp.concatenate(top2, axis=0)
        cand = jnp.concatenate([t1 + t2 for t1 in top1], axis=0)
        best = _take_top(cand, PEER_TOPK + 1)
        z = jnp.ones_like(best[0])
        for m in best[1:PEER_TOPK]:
            z += jnp.exp(m - best[0])
        cut = 0.5 * (best[PEER_TOPK - 1] + best[PEER_TOPK])
        s2_ref[:, tok] = s2
        e2_ref[:, tok] = jnp.where(s2 >= top2[PEER_TOPK - 1], jnp.exp(s2 - top2[0]), 0.0)
        c2_ref[:, tok] = cut - s1
        e1_ref[:, tok] = jnp.where(s1 >= top1[PEER_TOPK - 1], jnp.exp(s1 - top1[0]) / z, 0.0)


def peer_route(q, sub_keys, tm):
    t = q.shape[0]
    n_sub = max(tm // LANES, 1)
    out = jax.ShapeDtypeStruct((PEER_HEADS, PEER_KEYS, t), F32)
    ospec = pl.BlockSpec((None, PEER_KEYS, tm), lambda i, h: (h, 0, i))
    return pl.pallas_call(
        functools.partial(_peer_route_kernel, n_sub=n_sub),
        grid=(t // tm, PEER_HEADS),
        in_specs=[pl.BlockSpec((tm, 2 * PEER_KEYS), lambda i, h: (i, h)),
                  pl.BlockSpec((None, 2, PEER_KEYS, PEER_KEYS), lambda i, h: (h, 0, 0, 0))],
        out_specs=[ospec, ospec, ospec, ospec],
        out_shape=[out, out, out, out],
        compiler_params=_params("parallel", "parallel"),
        name="peer_route",
    )(q, sub_keys)


def _peer_dense_kernel(x_ref, u_ref, vt_ref, s2_ref, e2_ref, c2_ref, e1_ref, o_ref,
                       acc_sc, z0_sc, z1_sc, g0_sc, g1_sc, *, te, tm):
    k = pl.program_id(1)
    a_per_tile = te // PEER_KEYS
    rows_per_chunk = 2 * SUBLANES
    lanes_per_chunk = min(tm, 2 * LANES)

    @pl.when(k == 0)
    def _():
        for ref in (acc_sc, z0_sc, z1_sc, g0_sc, g1_sc):
            ref[...] = jnp.zeros_like(ref)

    d = x_ref.shape[1]
    n_pieces = PEER_STAGE_PIECES
    tok_halves = 2 if tm % (4 * LANES) == 0 else 1
    k_pieces = n_pieces // tok_halves
    chunks = [(al, c) for al in range(a_per_tile) for c in range(PEER_KEYS // rows_per_chunk)]
    chunks_per_piece = len(chunks) // n_pieces
    nt = (((1,), (1,)), ((), ()))

    def stage(z_new, z_prev, g_new, g_prev):
        for piece in range(n_pieces):
            half, kq = divmod(piece, k_pieces)
            tok = slice(half * tm // tok_halves, (half + 1) * tm // tok_halves)
            ks = slice(kq * d // k_pieces, (kq + 1) * d // k_pieces)
            part = lax.dot_general(u_ref[:, ks], x_ref[tok, ks], nt, preferred_element_type=F32)
            if kq == 0:
                z_new[:, tok] = part
            else:
                z_new[:, tok] += part
            for al, c in chunks[piece * chunks_per_piece:(piece + 1) * chunks_per_piece]:
                a = jnp.clip((k - 1) * a_per_tile + al, 0, PEER_KEYS - 1)
                rows = slice(c * rows_per_chunk, (c + 1) * rows_per_chunk)
                zrows = slice(al * PEER_KEYS + c * rows_per_chunk, al * PEER_KEYS + (c + 1) * rows_per_chunk)
                for t0 in range(0, tm, lanes_per_chunk):
                    tok = slice(t0, t0 + lanes_per_chunk)
                    gate = jnp.zeros((rows_per_chunk, lanes_per_chunk), F32)
                    for h in range(PEER_HEADS):
                        keep = s2_ref[h, rows, tok] >= c2_ref[h, pl.ds(a, 1), tok]
                        gate += jnp.where(keep, e2_ref[h, rows, tok] * e1_ref[h, pl.ds(a, 1), tok], 0.0)
                    g_new[zrows, tok] = (gate * jax.nn.gelu(z_prev[zrows, tok])).astype(BF16)
            ch = slice(piece * d // n_pieces, (piece + 1) * d // n_pieces)
            acc_sc[ch, :] += jnp.dot(vt_ref[ch, :], g_prev[...], preferred_element_type=F32)

    @pl.when(k % 2 == 0)
    def _():
        stage(z0_sc, z1_sc, g1_sc, g0_sc)

    @pl.when(k % 2 == 1)
    def _():
        stage(z1_sc, z0_sc, g0_sc, g1_sc)

    @pl.when(k == pl.num_programs(1) - 1)
    def _():
        o_ref[...] = acc_sc[...].T


def peer_dense(x_bf, u_bf, vt_bf, routes, tm, te):
    t, d = x_bf.shape
    n_tiles = u_bf.shape[0] // te
    last = n_tiles - 1
    rspec = pl.BlockSpec((PEER_HEADS, PEER_KEYS, tm), lambda i, k: (0, 0, i))
    return pl.pallas_call(
        functools.partial(_peer_dense_kernel, te=te, tm=tm),
        grid=(t // tm, n_tiles + 2),
        in_specs=[pl.BlockSpec((tm, d), lambda i, k: (i, 0)),
                  pl.BlockSpec((te, d), lambda i, k: (jnp.minimum(k, last), 0)),
                  pl.BlockSpec((d, te), lambda i, k: (0, jnp.clip(k - 2, 0, last))),
                  rspec, rspec, rspec, rspec],
        out_specs=pl.BlockSpec((tm, d), lambda i, k: (i, 0)),
        out_shape=jax.ShapeDtypeStruct((t, d), F32),
        scratch_shapes=[pltpu.VMEM((d, tm), F32), pltpu.VMEM((te, tm), F32), pltpu.VMEM((te, tm), F32),
                        pltpu.VMEM((te, tm), BF16), pltpu.VMEM((te, tm), BF16)],
        compiler_params=_params("parallel", "arbitrary"),
        name="peer_dense",
    )(x_bf, u_bf, vt_bf, *routes)


def peer(x_bf, wq_bf, sub_keys, u_bf, vt_bf, tm, tm_route):
    q = matmul(x_bf, wq_bf, 0, wq_bf.shape[1], tm)
    routes = peer_route(q, sub_keys, tm_route)
    return peer_dense(x_bf, u_bf, vt_bf, routes, tm, PEER_EXPERT_TILE)


def _row_tile(t):
    tm = min(t, TOKEN_TILE)
    assert t % tm == 0
    return tm


def kernel(x_prompt, x_sample, cache_k, cache_v, cache_logf, page_table, state_s5_re, state_s5_im, w_in_even, s5_lambda_re, s5_lambda_im, s5_log_dt, s5_b_re, s5_b_im, s5_c_re, s5_c_im, s5_d, s5_w_glu, s5_b_glu, gm_ln_g, gm_ln_b, gm_w_s, gm_b_s, w_out_even, w_in_odd, b_f, w_out_odd, ln1_g, ln1_b, ln2_g, ln2_b, peer_w_q, peer_sub_keys, peer_u, peer_v):
    bsz, seq, d = x_prompt.shape
    dbs, dseq, _ = x_sample.shape
    n_odd, n_phys = cache_k.shape[0], cache_k.shape[1]
    tp, ts = bsz * seq, dbs * dseq
    xp = x_prompt.reshape(tp, d)
    xs = x_sample.reshape(ts, d)
    xp_bf, xs_bf = xp.astype(BF16), xs.astype(BF16)
    tm_p = _row_tile(tp)
    tm_s = _row_tile(ts)
    ck = cache_k.reshape(n_odd, n_phys, PAGE_SIZE * N_HEADS, HEAD_DIM)
    cv = cache_v.reshape(n_odd, n_phys, PAGE_SIZE * N_HEADS, HEAD_DIM)
    clf = cache_logf.transpose(0, 1, 3, 2)
    zero_state = jnp.zeros((bsz, S5_GROUPS, S5_STATE), F32)
    peer_rep = max(LANES // ts, 1)

    outs = {k: [] for k in ("kp", "vp", "lp", "ks", "vs", "ls", "s5p_re", "s5p_im", "s5s_re", "s5s_im", "gmv")}
    for li in range(DEPTH):
        j = li // 2
        g1, b1 = ln1_g[li].reshape(1, d), ln1_b[li].reshape(1, d)
        g2, b2 = ln2_g[li].reshape(1, d), ln2_b[li].reshape(1, d)
        if li % 2 == 0:
            tables = _s5_tables(s5_lambda_re[j], s5_lambda_im[j], s5_log_dt[j], s5_b_re[j], s5_b_im[j],
                                s5_c_re[j], s5_c_im[j])
            n_in = w_in_even.shape[2]
            w_in, w_glu, w_out = (w.astype(BF16) for w in (w_in_even[j], s5_w_glu[j], w_out_even[j]))

            def even(x, x_bf, nb, nl, h0_re, h0_im, tm):
                proj = matmul(x_bf, w_in, 0, n_in, tm).reshape(nb, nl, n_in)
                y, hl_re, hl_im = s5_scan(proj, h0_re, h0_im, tables, s5_d[j], min(nl, S5_CHUNK))
                a_out = s5_glu(y.reshape(nb * nl, S5_WIDTH), w_glu, s5_b_glu[j], tm)
                g_out, v_rows = chunk_gmlp(proj, gm_ln_g[j], gm_ln_b[j], gm_w_s[j], gm_b_s[j], min(nl, GM_CHUNK))
                x, x_bf = out_proj_dnln(a_out, g_out.reshape(nb * nl, GM_WIDTH), w_out, x, g1, b1, tm)
                return x, x_bf, hl_re, hl_im, v_rows

            xp, xp_bf, hp_re, hp_im, _ = even(xp, xp_bf, bsz, seq, zero_state, zero_state, tm_p)
            xs, xs_bf, hs_re, hs_im, v_rows = even(xs, xs_bf, dbs, dseq, state_s5_re[j].astype(F32),
                                                   state_s5_im[j].astype(F32), tm_s)
            outs["s5p_re"].append(hp_re)
            outs["s5p_im"].append(hp_im)
            outs["s5s_re"].append(hs_re)
            outs["s5s_im"].append(hs_im)
            outs["gmv"].append(v_rows)
        else:
            w_f = w_in_odd[j][:, 3 * d:]
            w_qkv, w_out = w_in_odd[j].astype(BF16), w_out_odd[j].astype(BF16)

            def project(x_bf, tm):
                q, k, v = (matmul(x_bf, w_qkv, part, d, tm) for part in range(3))
                return q, k, v, forget_log_gate(x_bf, w_f, b_f[j], tm)

            qp, kp, vp, lfp = project(xp_bf, tm_p)
            f_cum = jnp.cumsum(lfp.reshape(bsz, seq, N_HEADS), axis=1)
            ap = fox_prompt(qp.reshape(bsz, seq, d), kp.reshape(bsz, seq, d), vp.reshape(bsz, seq, d), f_cum,
                            min(seq, FOX_BLOCK))
            xp, xp_bf = out_proj_dnln(ap.reshape(tp, d), None, w_out, xp, g1, b1, tm_p)
            qs, kss, vss, lfs = project(xs_bf, tm_s)
            g_new = jnp.cumsum(lfs.reshape(dbs, dseq, N_HEADS), axis=1)
            a_s = fox_sample(qs.reshape(dbs, dseq, d), kss.reshape(dbs, dseq, d), vss.reshape(dbs, dseq, d),
                             g_new, ck, cv, clf, page_table, j)
            xs, xs_bf = out_proj_dnln(a_s.reshape(ts, d), None, w_out, xs, g1, b1, tm_s)
            outs["kp"].append(kp.reshape(bsz, seq, N_HEADS, HEAD_DIM))
            outs["vp"].append(vp.reshape(bsz, seq, N_HEADS, HEAD_DIM))
            outs["lp"].append(lfp.reshape(bsz, seq, N_HEADS))
            outs["ks"].append(kss.reshape(dbs, dseq, N_HEADS, HEAD_DIM))
            outs["vs"].append(vss.reshape(dbs, dseq, N_HEADS, HEAD_DIM))
            outs["ls"].append(lfs.reshape(dbs, dseq, N_HEADS))
        u_bf = peer_u[li].astype(BF16)
        vt_bf = peer_v[li].T.astype(BF16)
        wq_bf = peer_w_q[li].astype(BF16)
        mp = peer(xp_bf, wq_bf, peer_sub_keys[li], u_bf, vt_bf, tm_p, min(tm_p, PEER_ROUTE_TILE))
        xp, xp_bf = dn_layer_norm(xp, mp, g2, b2, tm_p)
        xs_rep = jnp.tile(xs_bf, (peer_rep, 1))
        t_rep = xs_rep.shape[0]
        ms = peer(xs_rep, wq_bf, peer_sub_keys[li], u_bf, vt_bf, t_rep, t_rep)[:ts]
        xs, xs_bf = dn_layer_norm(xs, ms, g2, b2, tm_s)

    stack = lambda name, dtype: jnp.stack(outs[name]).astype(dtype)
    return (xp.reshape(bsz, seq, d), xs.reshape(dbs, dseq, d),
            stack("kp", cache_k.dtype), stack("vp", cache_v.dtype), stack("lp", cache_logf.dtype),
            stack("ks", cache_k.dtype), stack("vs", cache_v.dtype), stack("ls", cache_logf.dtype),
            stack("s5p_re", state_s5_re.dtype), stack("s5p_im", state_s5_im.dtype),
            stack("s5s_re", state_s5_re.dtype), stack("s5s_im", state_s5_im.dtype),
            jnp.stack(outs["gmv"]))
```

```python
import functools
import math

import jax
import jax.numpy as jnp
from jax import lax
from jax.experimental import pallas as pl
from jax.experimental.pallas import tpu as pltpu

F32 = jnp.float32
BF16 = jnp.bfloat16
HIGHEST = lax.Precision.HIGHEST

D_MODEL = 2048
DEPTH = 4
PAGE_SIZE = 128
S5_WIDTH = 1024
S5_GROUP = 16
S5_GROUPS = 64
S5_STATE = 64
S5_GROUP_BLOCK = 8
S5_BLOCKS = S5_GROUPS // S5_GROUP_BLOCK
S5_BLOCK_STATE = S5_GROUP_BLOCK * S5_STATE
GM_WIDTH = 1024
GM_CHUNK = 128
GM_HEADS = 8
GM_HEAD_DIM = 128
N_HEADS = 16
HEAD_DIM = 128
ATTN_SCALE = HEAD_DIM ** -0.5
TOKEN_TILE = 512
S5_CHUNK = 512
PEER_EXPERT_TILE = 512
PEER_ROUTE_TILE = 512
FOX_BLOCK = 512
FOX_PAGES_PER_STEP = 4
PEER_HEADS = 8
PEER_KEYS = 128
PEER_TOPK = 16
DN_ALPHA = (2.0 * DEPTH) ** 0.25
LN_EPS = 1e-5
SUBLANES = 8
LANES = 128
VMEM_LIMIT = 56 * 1024 * 1024
NEG_INF = float("-inf")


def _params(*sem):
    return pltpu.CompilerParams(dimension_semantics=sem, vmem_limit_bytes=VMEM_LIMIT)


def _layer_norm(y, g, b):
    mu = jnp.mean(y, axis=-1, keepdims=True)
    yc = y - mu
    var = jnp.mean(yc * yc, axis=-1, keepdims=True)
    return yc * lax.rsqrt(var + LN_EPS) * g + b


def _mm_kernel(x_ref, w_ref, o_ref):
    o_ref[...] = jnp.dot(x_ref[...].astype(BF16), w_ref[...], preferred_element_type=F32)


def matmul(x, w_bf, col_block, n_out, tm):
    t, k = x.shape
    assert t % tm == 0
    return pl.pallas_call(
        _mm_kernel,
        grid=(t // tm,),
        in_specs=[pl.BlockSpec((tm, k), lambda i: (i, 0)),
                  pl.BlockSpec((k, n_out), lambda i: (0, col_block))],
        out_specs=pl.BlockSpec((tm, n_out), lambda i: (i, 0)),
        out_shape=jax.ShapeDtypeStruct((t, n_out), F32),
        compiler_params=_params("parallel"),
        name="matmul",
    )(x, w_bf)


def _store_norm(y, g_ref, beta_ref, o_ref, ob_ref):
    out = _layer_norm(y, g_ref[...], beta_ref[...])
    o_ref[...] = out
    ob_ref[...] = out.astype(BF16)


def _mm1_dnln_kernel(a_ref, w_ref, x_ref, g_ref, beta_ref, o_ref, ob_ref):
    m = jnp.dot(a_ref[...].astype(BF16), w_ref[...], preferred_element_type=F32)
    _store_norm(DN_ALPHA * x_ref[...] + m, g_ref, beta_ref, o_ref, ob_ref)


def _mm2_dnln_kernel(a_ref, b_ref, wa_ref, wb_ref, x_ref, g_ref, beta_ref, o_ref, ob_ref):
    m = jnp.dot(a_ref[...].astype(BF16), wa_ref[...], preferred_element_type=F32)
    m += jnp.dot(b_ref[...].astype(BF16), wb_ref[...], preferred_element_type=F32)
    _store_norm(DN_ALPHA * x_ref[...] + m, g_ref, beta_ref, o_ref, ob_ref)


def _norm_outputs(t, d, tm):
    row = lambda i: (i, 0)
    return ([pl.BlockSpec((tm, d), row), pl.BlockSpec((tm, d), row)],
            [jax.ShapeDtypeStruct((t, d), F32), jax.ShapeDtypeStruct((t, d), BF16)])


def out_proj_dnln(a, b, w_bf, x, g, beta, tm):
    t, ka = a.shape
    d = w_bf.shape[1]
    row = lambda i: (i, 0)
    fixed = lambda i: (0, 0)
    tail = [pl.BlockSpec((tm, d), row), pl.BlockSpec((1, d), fixed), pl.BlockSpec((1, d), fixed)]
    if b is None:
        kern = _mm1_dnln_kernel
        ins = [a, w_bf, x, g, beta]
        specs = [pl.BlockSpec((tm, ka), row), pl.BlockSpec((ka, d), fixed)] + tail
    else:
        assert b.shape[1] == ka
        kern = _mm2_dnln_kernel
        ins = [a, b, w_bf, w_bf, x, g, beta]
        specs = [pl.BlockSpec((tm, ka), row), pl.BlockSpec((tm, ka), row),
                 pl.BlockSpec((ka, d), fixed), pl.BlockSpec((ka, d), lambda i: (1, 0))] + tail
    out_specs, out_shape = _norm_outputs(t, d, tm)
    return pl.pallas_call(
        kern, grid=(t // tm,), in_specs=specs, out_specs=out_specs, out_shape=out_shape,
        compiler_params=_params("parallel"),
        name="out_proj_dnln",
    )(*ins)


def _dnln_kernel(x_ref, m_ref, g_ref, b_ref, o_ref, ob_ref):
    _store_norm(DN_ALPHA * x_ref[...] + m_ref[...], g_ref, b_ref, o_ref, ob_ref)


def dn_layer_norm(x, m, g, b, tm):
    t, d = x.shape
    row = lambda i: (i, 0)
    fixed = lambda i: (0, 0)
    out_specs, out_shape = _norm_outputs(t, d, tm)
    return pl.pallas_call(
        _dnln_kernel, grid=(t // tm,),
        in_specs=[pl.BlockSpec((tm, d), row), pl.BlockSpec((tm, d), row),
                  pl.BlockSpec((1, d), fixed), pl.BlockSpec((1, d), fixed)],
        out_specs=out_specs, out_shape=out_shape,
        compiler_params=_params("parallel"),
        name="dn_layer_norm",
    )(x, m, g, b)


def _split_bf16(x):
    hi = lax.bitcast_convert_type(lax.bitcast_convert_type(x, jnp.uint32) & jnp.uint32(0xFFFF0000), F32)
    return hi.astype(BF16), (x - hi).astype(BF16)


def _dot_split(a, b_ref):
    a_hi, a_lo = _split_bf16(a)
    dot = functools.partial(jnp.dot, preferred_element_type=F32)
    return dot(a_hi, b_ref[0]) + (dot(a_lo, b_ref[0]) + dot(a_hi, b_ref[1]))


def _s5_kernel(u_ref, bd_ref, cd_ref, tab_ref, d_ref, h0_ref, y_ref, hl_ref, hs_sc, carry_sc, *, lc):
    ns = S5_BLOCK_STATE

    @pl.when(pl.program_id(2) == 0)
    def _():
        carry_sc[...] = h0_ref[...]

    u = u_ref[...]
    hs_sc[...] = _dot_split(u, bd_ref)

    def row_group(r, carry):
        cr, ci = carry
        rows = pl.ds(pl.multiple_of(r * SUBLANES, SUBLANES), SUBLANES)
        xr = hs_sc[rows, :ns]
        xi = hs_sc[rows, ns:]
        for k in range(3):
            ar = tab_ref[2 * k]
            ai = tab_ref[2 * k + 1]
            sr = pltpu.roll(xr, 1 << k, 0)
            si = pltpu.roll(xi, 1 << k, 0)
            xr, xi = xr + (ar * sr - ai * si), xi + (ar * si + ai * sr)
        pr = tab_ref[6]
        pi = tab_ref[7]
        xr, xi = xr + (pr * cr - pi * ci), xi + (pr * ci + pi * cr)
        hs_sc[rows, :ns] = xr
        hs_sc[rows, ns:] = xi
        return xr[SUBLANES - 1:, :], xi[SUBLANES - 1:, :]

    cr, ci = lax.fori_loop(0, lc // SUBLANES, row_group, (carry_sc[:, :ns], carry_sc[:, ns:]))
    carry_sc[:, :ns] = cr
    carry_sc[:, ns:] = ci
    hl_ref[...] = carry_sc[...]
    y_ref[...] = _dot_split(hs_sc[...], cd_ref) + d_ref[...] * u


def _s5_tables(lam_re, lam_im, log_dt, b_re, b_im, c_re, c_im):
    dt = jnp.exp(log_dt.astype(F32))[:, None]
    mag = jnp.exp(lam_re * dt)
    lbr = mag * jnp.cos(lam_im * dt)
    lbi = mag * jnp.sin(lam_im * dt)
    den = lam_re * lam_re + lam_im * lam_im
    qr = ((lbr - 1.0) * lam_re + lbi * lam_im) / den
    qi = (lbi * lam_re - (lbr - 1.0) * lam_im) / den
    bbr = qr[..., None] * b_re - qi[..., None] * b_im
    bbi = qr[..., None] * b_im + qi[..., None] * b_re
    nb, gb = S5_BLOCKS, S5_GROUP_BLOCK
    eye = jnp.eye(gb, dtype=F32)

    def bdiag(m):
        m = m.reshape(nb, gb, S5_STATE, S5_GROUP)
        return jnp.einsum('ngpc,gh->ngchp', m, eye).reshape(nb, gb * S5_GROUP, gb * S5_STATE)

    def cdiag(m):
        m = m.reshape(nb, gb, S5_GROUP, S5_STATE)
        return jnp.einsum('ngcp,gh->ngphc', m, eye).reshape(nb, gb * S5_STATE, gb * S5_GROUP)

    bd = jnp.stack(_split_bf16(jnp.concatenate([bdiag(bbr), bdiag(bbi)], axis=-1)), axis=1)
    cd = jnp.stack(_split_bf16(jnp.concatenate([cdiag(c_re), cdiag(-c_im)], axis=1)), axis=1)

    def cmul(a, b):
        return a[0] * b[0] - a[1] * b[1], a[0] * b[1] + a[1] * b[0]

    powers = [(lbr, lbi)]
    for _ in range(SUBLANES - 1):
        powers.append(cmul(powers[-1], (lbr, lbi)))
    row = jnp.arange(SUBLANES)[:, None, None]
    tabs = []
    for k in range(3):
        sh = 1 << k
        for part in powers[sh - 1]:
            tabs.append(jnp.where(row >= sh, part[None], 0.0))
    tabs.append(jnp.stack([p[0] for p in powers]))
    tabs.append(jnp.stack([p[1] for p in powers]))
    tab = jnp.stack(tabs)
    tab = tab.reshape(8, SUBLANES, nb, S5_BLOCK_STATE).transpose(2, 0, 1, 3)
    return bd, cd, tab


def s5_scan(proj, h0_re, h0_im, tables, d_skip, lc):
    bd, cd, tab = tables
    bsz, seq = proj.shape[0], proj.shape[1]
    nb, ns = S5_BLOCKS, S5_BLOCK_STATE
    h0 = jnp.concatenate([h0_re.reshape(bsz, nb, 1, ns), h0_im.reshape(bsz, nb, 1, ns)], axis=-1)
    y, hl = pl.pallas_call(
        functools.partial(_s5_kernel, lc=lc),
        grid=(bsz, nb, seq // lc),
        in_specs=[pl.BlockSpec((None, lc, LANES), lambda b, g, l: (b, l, g)),
                  pl.BlockSpec((None, 2, LANES, 2 * ns), lambda b, g, l: (g, 0, 0, 0)),
                  pl.BlockSpec((None, 2, 2 * ns, LANES), lambda b, g, l: (g, 0, 0, 0)),
                  pl.BlockSpec((None, 8, SUBLANES, ns), lambda b, g, l: (g, 0, 0, 0)),
                  pl.BlockSpec((1, LANES), lambda b, g, l: (0, g)),
                  pl.BlockSpec((None, None, 1, 2 * ns), lambda b, g, l: (b, g, 0, 0))],
        out_specs=[pl.BlockSpec((None, lc, LANES), lambda b, g, l: (b, l, g)),
                   pl.BlockSpec((None, None, 1, 2 * ns), lambda b, g, l: (b, g, 0, 0))],
        out_shape=[jax.ShapeDtypeStruct((bsz, seq, S5_WIDTH), F32),
                   jax.ShapeDtypeStruct((bsz, nb, 1, 2 * ns), F32)],
        scratch_shapes=[pltpu.VMEM((lc, 2 * ns), F32), pltpu.VMEM((1, 2 * ns), F32)],
        compiler_params=_params("parallel", "parallel", "arbitrary"),
        name="s5_scan",
    )(proj, bd, cd, tab, d_skip.reshape(1, S5_WIDTH), h0)
    hl_re = hl[..., :ns].reshape(bsz, S5_GROUPS, S5_STATE)
    hl_im = hl[..., ns:].reshape(bsz, S5_GROUPS, S5_STATE)
    return y, hl_re, hl_im


def _glu_kernel(y_ref, w_ref, b_ref, o_ref):
    y = jax.nn.gelu(y_ref[...])
    z = jnp.dot(y.astype(BF16), w_ref[...], preferred_element_type=F32) + b_ref[...]
    o_ref[...] = y * jax.nn.sigmoid(z)


def s5_glu(y, w, b, tm):
    t, k = y.shape
    return pl.pallas_call(
        _glu_kernel, grid=(t // tm,),
        in_specs=[pl.BlockSpec((tm, k), lambda i: (i, 0)), pl.BlockSpec((k, k), lambda i: (0, 0)),
                  pl.BlockSpec((1, k), lambda i: (0, 0))],
        out_specs=pl.BlockSpec((tm, k), lambda i: (i, 0)),
        out_shape=jax.ShapeDtypeStruct((t, k), F32),
        compiler_params=_params("parallel"),
        name="s5_glu",
    )(y, w, b.reshape(1, k))


def _gmlp_kernel(zu_ref, zv_ref, g_ref, b_ref, w_ref, bs_ref, o_ref, v_ref, *, lc):
    u = jax.nn.gelu(zu_ref[...])
    v = _layer_norm(jax.nn.gelu(zv_ref[...]), g_ref[...], b_ref[...])
    v_ref[...] = v
    row = lax.broadcasted_iota(jnp.int32, (lc, lc), 0)
    col = lax.broadcasted_iota(jnp.int32, (lc, lc), 1)
    causal = row >= col
    vb = v.astype(BF16)
    bs = bs_ref[...]
    for h in range(GM_HEADS):
        cols = slice(h * GM_HEAD_DIM, (h + 1) * GM_HEAD_DIM)
        w = jnp.where(causal, w_ref[h], 0.0).astype(BF16)
        mixed = jnp.dot(w, vb[:, cols], preferred_element_type=F32) + bs[:, h:h + 1]
        o_ref[:, cols] = u[:, cols] * mixed


def chunk_gmlp(proj, ln_g, ln_b, w_s, b_s, lc):
    bsz, seq = proj.shape[0], proj.shape[1]
    w = w_s[:, :lc, :lc]
    bs = b_s[:, :lc].T
    fixed2 = lambda b, n: (0, 0)
    return pl.pallas_call(
        functools.partial(_gmlp_kernel, lc=lc),
        grid=(bsz, seq // lc),
        in_specs=[pl.BlockSpec((None, lc, GM_WIDTH), lambda b, n: (b, n, 1)),
                  pl.BlockSpec((None, lc, GM_WIDTH), lambda b, n: (b, n, 2)),
                  pl.BlockSpec((1, GM_WIDTH), fixed2), pl.BlockSpec((1, GM_WIDTH), fixed2),
                  pl.BlockSpec((GM_HEADS, lc, lc), lambda b, n: (0, 0, 0)),
                  pl.BlockSpec((lc, GM_HEADS), fixed2)],
        out_specs=[pl.BlockSpec((None, lc, GM_WIDTH), lambda b, n: (b, n, 0)),
                   pl.BlockSpec((None, lc, GM_WIDTH), lambda b, n: (b, n, 0))],
        out_shape=[jax.ShapeDtypeStruct((bsz, seq, GM_WIDTH), F32),
                   jax.ShapeDtypeStruct((bsz, seq, GM_WIDTH), F32)],
        compiler_params=_params("parallel", "parallel"),
        name="chunk_gmlp",
    )(proj, proj, ln_g.reshape(1, GM_WIDTH), ln_b.reshape(1, GM_WIDTH), w, bs)


def _logf_kernel(x_ref, w_ref, b_ref, o_ref):
    z = jnp.dot(x_ref[...].astype(BF16), w_ref[...].astype(BF16), preferred_element_type=F32)
    o_ref[...] = jax.nn.log_sigmoid(z + b_ref[...])


def forget_log_gate(x, w_f, b_f, tm):
    t, k = x.shape
    return pl.pallas_call(
        _logf_kernel, grid=(t // tm,),
        in_specs=[pl.BlockSpec((tm, k), lambda i: (i, 0)), pl.BlockSpec((k, N_HEADS), lambda i: (0, 0)),
                  pl.BlockSpec((1, N_HEADS), lambda i: (0, 0))],
        out_specs=pl.BlockSpec((tm, N_HEADS), lambda i: (i, 0)),
        out_shape=jax.ShapeDtypeStruct((t, N_HEADS), F32),
        compiler_params=_params("parallel"),
        name="forget_log_gate",
    )(x, w_f, b_f.reshape(1, N_HEADS))


def _fox_prompt_kernel(q_ref, k_ref, v_ref, fq_ref, fk_ref, o_ref, m_sc, l_sc, acc_sc, *, tq):
    h = pl.program_id(1)
    qi = pl.program_id(2)
    ki = pl.program_id(3)

    @pl.when(ki == 0)
    def _():
        m_sc[...] = jnp.full_like(m_sc, NEG_INF)
        l_sc[...] = jnp.zeros_like(l_sc)
        acc_sc[...] = jnp.zeros_like(acc_sc)

    def block(on_diagonal):
        q = (q_ref[...] * ATTN_SCALE).astype(BF16)
        s = lax.dot_general(q, k_ref[...].astype(BF16), (((1,), (1,)), ((), ())), preferred_element_type=F32)
        head = lax.broadcasted_iota(jnp.int32, (tq, N_HEADS), 1) == h
        fq = jnp.sum(jnp.where(head, fq_ref[...], 0.0), axis=-1, keepdims=True)
        s = s + fq - fk_ref[...]
        if on_diagonal:
            causal = (lax.broadcasted_iota(jnp.int32, (tq, tq), 0)
                      >= lax.broadcasted_iota(jnp.int32, (tq, tq), 1))
            s = jnp.where(causal, s, NEG_INF)
        m_old = m_sc[...]
        m_new = jnp.maximum(m_old, jnp.max(s, axis=-1, keepdims=True))
        alpha = jnp.exp(m_old - m_new)
        p = jnp.exp(s - m_new)
        l_sc[...] = alpha * l_sc[...] + jnp.sum(p, axis=-1, keepdims=True)
        acc_sc[...] = alpha * acc_sc[...] + jnp.dot(p.astype(BF16), v_ref[...].astype(BF16),
                                                    preferred_element_type=F32)
        m_sc[...] = m_new

    @pl.when(ki < qi)
    def _():
        block(False)

    @pl.when(ki == qi)
    def _():
        block(True)

    @pl.when(ki == pl.num_programs(3) - 1)
    def _():
        o_ref[...] = acc_sc[...] / l_sc[...]


def fox_prompt(q, k, v, f_cum, tq):
    bsz, seq = q.shape[0], q.shape[1]
    nq = seq // tq
    f_rows = f_cum.transpose(0, 2, 1).reshape(bsz, N_HEADS, 1, seq)
    kv_map = lambda b, h, i, j: (b, jnp.minimum(j, i), h)
    return pl.pallas_call(
        functools.partial(_fox_prompt_kernel, tq=tq),
        grid=(bsz, N_HEADS, nq, nq),
        in_specs=[pl.BlockSpec((None, tq, HEAD_DIM), lambda b, h, i, j: (b, i, h)),
                  pl.BlockSpec((None, tq, HEAD_DIM), kv_map),
                  pl.BlockSpec((None, tq, HEAD_DIM), kv_map),
                  pl.BlockSpec((None, tq, N_HEADS), lambda b, h, i, j: (b, i, 0)),
                  pl.BlockSpec((None, None, 1, tq), lambda b, h, i, j: (b, h, 0, jnp.minimum(j, i)))],
        out_specs=pl.BlockSpec((None, tq, HEAD_DIM), lambda b, h, i, j: (b, i, h)),
        out_shape=jax.ShapeDtypeStruct((bsz, seq, D_MODEL), F32),
        scratch_shapes=[pltpu.VMEM((tq, 1), F32), pltpu.VMEM((tq, 1), F32), pltpu.VMEM((tq, HEAD_DIM), F32)],
        compiler_params=_params("parallel", "parallel", "parallel", "arbitrary"),
        name="fox_prompt",
    )(q, k, v, f_cum, f_rows)


def _fox_sample_kernel(pt_ref, q_ref, kn_ref, vn_ref, gc_ref, gr_ref, *rest, n_new, pps):
    kp_refs, vp_refs, lp_refs = rest[:pps], rest[pps:2 * pps], rest[2 * pps:3 * pps]
    o_ref, m_sc, l_sc, acc_sc, carry_sc = rest[3 * pps:]
    step = pl.program_id(1)
    nt = (((1,), (1,)), ((), ()))
    head_rows = lambda h: slice(h * n_new, (h + 1) * n_new)
    head_cols = lambda h: slice(h * HEAD_DIM, (h + 1) * HEAD_DIM)
    q_heads = [(q_ref[:, head_cols(h)] * ATTN_SCALE).astype(BF16) for h in range(N_HEADS)]

    def online_update(s, values):
        m_old = m_sc[...]
        m_new = jnp.maximum(m_old, jnp.max(s, axis=-1, keepdims=True))
        alpha = jnp.exp(m_old - m_new)
        p = jnp.exp(s - m_new)
        l_sc[...] = alpha * l_sc[...] + jnp.sum(p, axis=-1, keepdims=True)
        pv = [jnp.dot(p[head_rows(h), :].astype(BF16), values(h), preferred_element_type=F32)
              for h in range(N_HEADS)]
        acc_sc[...] = alpha * acc_sc[...] + jnp.concatenate(pv, axis=0)
        m_sc[...] = m_new

    @pl.when(step == 0)
    def _():
        m_sc[...] = jnp.full_like(m_sc, NEG_INF)
        l_sc[...] = jnp.zeros_like(l_sc)
        acc_sc[...] = jnp.zeros_like(acc_sc)
        carry_sc[...] = jnp.zeros_like(carry_sc)
        s = jnp.concatenate([lax.dot_general(q_heads[h], kn_ref[:, head_cols(h)].astype(BF16), nt,
                                             preferred_element_type=F32) for h in range(N_HEADS)], axis=0)
        s = s + gc_ref[...] - gr_ref[...]
        rows = N_HEADS * n_new
        causal = (lax.broadcasted_iota(jnp.int32, (rows, n_new), 0) % n_new
                  >= lax.broadcasted_iota(jnp.int32, (rows, n_new), 1))
        online_update(jnp.where(causal, s, NEG_INF), lambda h: vn_ref[:, head_cols(h)].astype(BF16))

    @pl.when(step > 0)
    def _():
        later = (lax.broadcasted_iota(jnp.int32, (PAGE_SIZE, PAGE_SIZE), 0)
                 > lax.broadcasted_iota(jnp.int32, (PAGE_SIZE, PAGE_SIZE), 1)).astype(F32)
        carry = carry_sc[...]
        page_scores = []
        for pg in range(pps):
            lp = lp_refs[pg][...]
            suffix = jnp.dot(lp, later, precision=HIGHEST, preferred_element_type=F32) + carry
            carry = carry + jnp.sum(lp, axis=-1, keepdims=True)
            qk = [lax.dot_general(q_heads[h], kp_refs[pg][pl.ds(h, PAGE_SIZE, stride=N_HEADS), :].astype(BF16),
                                  nt, preferred_element_type=F32)
                  + jnp.broadcast_to(suffix[h:h + 1, :], (n_new, PAGE_SIZE)) for h in range(N_HEADS)]
            page_scores.append(jnp.concatenate(qk, axis=0))
        carry_sc[...] = carry
        s = jnp.concatenate(page_scores, axis=1) + gc_ref[...]

        def values(h):
            return jnp.concatenate([vp_refs[pg][pl.ds(h, PAGE_SIZE, stride=N_HEADS), :].astype(BF16)
                                    for pg in range(pps)], axis=0)

        online_update(s, values)

    @pl.when(step == pl.num_programs(1) - 1)
    def _():
        out = acc_sc[...] / l_sc[...]
        for h in range(N_HEADS):
            o_ref[:, head_cols(h)] = out[head_rows(h), :]


def fox_sample(q, k, v, g_new, cache_k, cache_v, cache_logf_t, page_table, layer):
    dbs, n_new = q.shape[0], q.shape[1]
    n_pages = page_table.shape[1]
    pps = math.gcd(n_pages, FOX_PAGES_PER_STEP)
    rows = N_HEADS * n_new
    g_t = g_new.transpose(0, 2, 1)
    g_col = g_t.reshape(dbs, rows, 1)
    g_row = jnp.repeat(g_t, n_new, axis=1)

    def page_map(pg):
        def index(b, s, pt):
            first = jnp.maximum(s - 1, 0) * pps
            return (layer, pt[b, n_pages - 1 - (first + pg)], 0, 0)
        return index

    seq_map = lambda b, s, pt: (b, 0, 0)
    page_specs = lambda shape: [pl.BlockSpec((None, None) + shape, page_map(pg)) for pg in range(pps)]
    grid_spec = pltpu.PrefetchScalarGridSpec(
        num_scalar_prefetch=1,
        grid=(dbs, n_pages // pps + 1),
        in_specs=([pl.BlockSpec((None, n_new, D_MODEL), seq_map)] * 3
                  + [pl.BlockSpec((None, rows, 1), seq_map), pl.BlockSpec((None, rows, n_new), seq_map)]
                  + page_specs((PAGE_SIZE * N_HEADS, HEAD_DIM)) * 2
                  + page_specs((N_HEADS, PAGE_SIZE))),
        out_specs=pl.BlockSpec((None, n_new, D_MODEL), seq_map),
        scratch_shapes=[pltpu.VMEM((rows, 1), F32), pltpu.VMEM((rows, 1), F32),
                        pltpu.VMEM((rows, HEAD_DIM), F32), pltpu.VMEM((N_HEADS, 1), F32)],
    )
    return pl.pallas_call(
        functools.partial(_fox_sample_kernel, n_new=n_new, pps=pps),
        grid_spec=grid_spec,
        out_shape=jax.ShapeDtypeStruct((dbs, n_new, D_MODEL), F32),
        compiler_params=_params("parallel", "arbitrary"),
        name="fox_sample",
    )(page_table, q, k, v, g_col, g_row, *([cache_k] * pps), *([cache_v] * pps), *([cache_logf_t] * pps))


def _take_top(s, count):
    tops = []
    for i in range(count):
        m = jnp.max(s, axis=0, keepdims=True)
        tops.append(m)
        if i + 1 < count:
            s = jnp.where(s == m, NEG_INF, s)
    return tops


def _candidate_sums(top1, top2):
    limit = PEER_TOPK + 1
    t2_head = jnp.concatenate(top2[:SUBLANES], axis=0)
    row = lax.broadcasted_iota(jnp.int32, t2_head.shape, 0)
    pieces = [top1[0] + jnp.concatenate(top2, axis=0)]
    for i in range(1, SUBLANES):
        pieces.append(jnp.where(row < limit // (i + 1), top1[i] + t2_head, NEG_INF))
    assert limit // (SUBLANES + 1) == 1
    pieces.append(jnp.concatenate(top1[SUBLANES:], axis=0) + top2[0])
    return jnp.concatenate(pieces, axis=0)


def _peer_route_kernel(q_ref, keys_ref, r2_ref, e2_ref, n2_ref, e1_ref, *, n_sub):
    nt = (((1,), (1,)), ((), ()))
    for sub in range(n_sub):
        tok = slice(sub * LANES, (sub + 1) * LANES) if n_sub > 1 else slice(None)
        q = q_ref[tok, :].astype(BF16)
        s1 = lax.dot_general(keys_ref[0].astype(BF16), q[:, :PEER_KEYS], nt, preferred_element_type=F32)
        s2 = lax.dot_general(keys_ref[1].astype(BF16), q[:, PEER_KEYS:], nt, preferred_element_type=F32)
        top1 = _take_top(s1, PEER_TOPK)
        top2 = _take_top(s2, PEER_TOPK)
        best = _take_top(_candidate_sums(top1, top2), PEER_TOPK + 1)
        z = jnp.ones_like(best[0])
        for m in best[1:PEER_TOPK]:
            z += jnp.exp(m - best[0])
        cut = 0.5 * (best[PEER_TOPK - 1] + best[PEER_TOPK])
        r2 = jnp.zeros_like(s2)
        n2 = jnp.zeros_like(s1)
        for t2 in top2:
            r2 += jnp.where(t2 > s2, 1.0, 0.0)
            n2 += jnp.where(s1 + t2 > cut, 1.0, 0.0)
        r2_ref[:, tok] = r2.astype(BF16)
        e2_ref[:, tok] = jnp.where(s2 >= top2[PEER_TOPK - 1], jnp.exp(s2 - top2[0]), 0.0).astype(BF16)
        n2_ref[:, tok] = n2
        e1_ref[:, tok] = jnp.where(s1 >= top1[PEER_TOPK - 1], jnp.exp(s1 - top1[0]) / z, 0.0)


def peer_route(q, sub_keys, tm):
    t = q.shape[0]
    n_sub = max(tm // LANES, 1)
    out = lambda dtype: jax.ShapeDtypeStruct((PEER_HEADS, PEER_KEYS, t), dtype)
    ospec = pl.BlockSpec((None, PEER_KEYS, tm), lambda i, h: (h, 0, i))
    return pl.pallas_call(
        functools.partial(_peer_route_kernel, n_sub=n_sub),
        grid=(t // tm, PEER_HEADS),
        in_specs=[pl.BlockSpec((tm, 2 * PEER_KEYS), lambda i, h: (i, h)),
                  pl.BlockSpec((None, 2, PEER_KEYS, PEER_KEYS), lambda i, h: (h, 0, 0, 0))],
        out_specs=[ospec, ospec, ospec, ospec],
        out_shape=[out(BF16), out(BF16), out(F32), out(F32)],
        compiler_params=_params("parallel", "parallel"),
        name="peer_route",
    )(q, sub_keys)


def _peer_dense_kernel(x_ref, u_ref, vt_ref, r2_ref, e2_ref, n2_ref, e1_ref, o_ref,
                       acc_sc, z0_sc, z1_sc, g0_sc, g1_sc, *, te, tm):
    k = pl.program_id(1)
    a_per_tile = te // PEER_KEYS
    rows_per_chunk = 2 * SUBLANES
    lanes_per_chunk = min(tm, 2 * LANES)

    @pl.when(k == 0)
    def _():
        for ref in (acc_sc, z0_sc, z1_sc, g0_sc, g1_sc):
            ref[...] = jnp.zeros_like(ref)

    n_chunks = PEER_KEYS // rows_per_chunk
    chunk_shape = (rows_per_chunk, lanes_per_chunk)

    def stage(z_new, z_prev, g_new, g_prev):
        z_new[...] = lax.dot_general(u_ref[...], x_ref[...], (((1,), (1,)), ((), ())),
                                     preferred_element_type=F32)
        for al in range(a_per_tile):
            for t0 in range(0, tm, lanes_per_chunk):
                tok = slice(t0, t0 + lanes_per_chunk)
                gates = [jnp.zeros(chunk_shape, BF16) for _ in range(n_chunks)]
                for h in range(PEER_HEADS):
                    n2 = jnp.broadcast_to(n2_ref[h, al:al + 1, tok], chunk_shape).astype(BF16)
                    e1 = jnp.broadcast_to(e1_ref[h, al:al + 1, tok], chunk_shape).astype(BF16)
                    for c in range(n_chunks):
                        rows = slice(c * rows_per_chunk, (c + 1) * rows_per_chunk)
                        keep = r2_ref[h, rows, tok] < n2
                        gates[c] += jnp.where(keep, e2_ref[h, rows, tok] * e1, jnp.zeros(chunk_shape, BF16))
                for c in range(n_chunks):
                    zrows = slice(al * PEER_KEYS + c * rows_per_chunk, al * PEER_KEYS + (c + 1) * rows_per_chunk)
                    g_new[zrows, tok] = (gates[c].astype(F32) * jax.nn.gelu(z_prev[zrows, tok])).astype(BF16)
        acc_sc[...] += jnp.dot(vt_ref[...], g_prev[...], preferred_element_type=F32)

    @pl.when(k % 2 == 0)
    def _():
        stage(z0_sc, z1_sc, g1_sc, g0_sc)

    @pl.when(k % 2 == 1)
    def _():
        stage(z1_sc, z0_sc, g0_sc, g1_sc)

    @pl.when(k == pl.num_programs(1) - 1)
    def _():
        o_ref[...] = acc_sc[...].T


def peer_dense(x_bf, u_bf, vt_bf, routes, tm, te):
    t, d = x_bf.shape
    n_tiles = u_bf.shape[0] // te
    last = n_tiles - 1
    r2, e2, n2, e1 = routes
    a_per_tile = te // PEER_KEYS
    n2, e1 = (r.reshape(PEER_HEADS, n_tiles, a_per_tile, t) for r in (n2, e1))
    bspec = pl.BlockSpec((PEER_HEADS, PEER_KEYS, tm), lambda i, k: (0, 0, i))
    aspec = pl.BlockSpec((PEER_HEADS, None, a_per_tile, tm), lambda i, k: (0, jnp.clip(k - 1, 0, last), 0, i))
    return pl.pallas_call(
        functools.partial(_peer_dense_kernel, te=te, tm=tm),
        grid=(t // tm, n_tiles + 2),
        in_specs=[pl.BlockSpec((tm, d), lambda i, k: (i, 0)),
                  pl.BlockSpec((te, d), lambda i, k: (jnp.minimum(k, last), 0)),
                  pl.BlockSpec((d, te), lambda i, k: (0, jnp.clip(k - 2, 0, last))),
                  bspec, bspec, aspec, aspec],
        out_specs=pl.BlockSpec((tm, d), lambda i, k: (i, 0)),
        out_shape=jax.ShapeDtypeStruct((t, d), F32),
        scratch_shapes=[pltpu.VMEM((d, tm), F32), pltpu.VMEM((te, tm), F32), pltpu.VMEM((te, tm), F32),
                        pltpu.VMEM((te, tm), BF16), pltpu.VMEM((te, tm), BF16)],
        compiler_params=_params("parallel", "arbitrary"),
        name="peer_dense",
    )(x_bf, u_bf, vt_bf, r2, e2, n2, e1)


def peer(x_bf, wq_bf, sub_keys, u_bf, vt_bf, tm, tm_route):
    q = matmul(x_bf, wq_bf, 0, wq_bf.shape[1], tm)
    routes = peer_route(q, sub_keys, tm_route)
    return peer_dense(x_bf, u_bf, vt_bf, routes, tm, PEER_EXPERT_TILE)


def _row_tile(t):
    tm = min(t, TOKEN_TILE)
    assert t % tm == 0
    return tm


def kernel(x_prompt, x_sample, cache_k, cache_v, cache_logf, page_table, state_s5_re, state_s5_im, w_in_even, s5_lambda_re, s5_lambda_im, s5_log_dt, s5_b_re, s5_b_im, s5_c_re, s5_c_im, s5_d, s5_w_glu, s5_b_glu, gm_ln_g, gm_ln_b, gm_w_s, gm_b_s, w_out_even, w_in_odd, b_f, w_out_odd, ln1_g, ln1_b, ln2_g, ln2_b, peer_w_q, peer_sub_keys, peer_u, peer_v):
    bsz, seq, d = x_prompt.shape
    dbs, dseq, _ = x_sample.shape
    n_odd, n_phys = cache_k.shape[0], cache_k.shape[1]
    tp, ts = bsz * seq, dbs * dseq
    xp = x_prompt.reshape(tp, d)
    xs = x_sample.reshape(ts, d)
    xp_bf, xs_bf = xp.astype(BF16), xs.astype(BF16)
    tm_p = _row_tile(tp)
    tm_s = _row_tile(ts)
    ck = cache_k.reshape(n_odd, n_phys, PAGE_SIZE * N_HEADS, HEAD_DIM)
    cv = cache_v.reshape(n_odd, n_phys, PAGE_SIZE * N_HEADS, HEAD_DIM)
    clf = cache_logf.transpose(0, 1, 3, 2)
    zero_state = jnp.zeros((bsz, S5_GROUPS, S5_STATE), F32)
    peer_rep = max(LANES // ts, 1)

    outs = {k: [] for k in ("kp", "vp", "lp", "ks", "vs", "ls", "s5p_re", "s5p_im", "s5s_re", "s5s_im", "gmv")}
    for li in range(DEPTH):
        j = li // 2
        g1, b1 = ln1_g[li].reshape(1, d), ln1_b[li].reshape(1, d)
        g2, b2 = ln2_g[li].reshape(1, d), ln2_b[li].reshape(1, d)
        if li % 2 == 0:
            tables = _s5_tables(s5_lambda_re[j], s5_lambda_im[j], s5_log_dt[j], s5_b_re[j], s5_b_im[j],
                                s5_c_re[j], s5_c_im[j])
            n_in = w_in_even.shape[2]
            w_in, w_glu, w_out = (w.astype(BF16) for w in (w_in_even[j], s5_w_glu[j], w_out_even[j]))

            def even(x, x_bf, nb, nl, h0_re, h0_im, tm):
                proj = matmul(x_bf, w_in, 0, n_in, tm).reshape(nb, nl, n_in)
                y, hl_re, hl_im = s5_scan(proj, h0_re, h0_im, tables, s5_d[j], min(nl, S5_CHUNK))
                a_out = s5_glu(y.reshape(nb * nl, S5_WIDTH), w_glu, s5_b_glu[j], tm)
                g_out, v_rows = chunk_gmlp(proj, gm_ln_g[j], gm_ln_b[j], gm_w_s[j], gm_b_s[j], min(nl, GM_CHUNK))
                x, x_bf = out_proj_dnln(a_out, g_out.reshape(nb * nl, GM_WIDTH), w_out, x, g1, b1, tm)
                return x, x_bf, hl_re, hl_im, v_rows

            xp, xp_bf, hp_re, hp_im, _ = even(xp, xp_bf, bsz, seq, zero_state, zero_state, tm_p)
            xs, xs_bf, hs_re, hs_im, v_rows = even(xs, xs_bf, dbs, dseq, state_s5_re[j].astype(F32),
                                                   state_s5_im[j].astype(F32), tm_s)
            outs["s5p_re"].append(hp_re)
            outs["s5p_im"].append(hp_im)
            outs["s5s_re"].append(hs_re)
            outs["s5s_im"].append(hs_im)
            outs["gmv"].append(v_rows)
        else:
            w_f = w_in_odd[j][:, 3 * d:]
            w_qkv, w_out = w_in_odd[j].astype(BF16), w_out_odd[j].astype(BF16)

            def project(x_bf, tm):
                q, k, v = (matmul(x_bf, w_qkv, part, d, tm) for part in range(3))
                return q, k, v, forget_log_gate(x_bf, w_f, b_f[j], tm)

            qp, kp, vp, lfp = project(xp_bf, tm_p)
            f_cum = jnp.cumsum(lfp.reshape(bsz, seq, N_HEADS), axis=1)
            ap = fox_prompt(qp.reshape(bsz, seq, d), kp.reshape(bsz, seq, d), vp.reshape(bsz, seq, d), f_cum,
                            min(seq, FOX_BLOCK))
            xp, xp_bf = out_proj_dnln(ap.reshape(tp, d), None, w_out, xp, g1, b1, tm_p)
            qs, kss, vss, lfs = project(xs_bf, tm_s)
            g_new = jnp.cumsum(lfs.reshape(dbs, dseq, N_HEADS), axis=1)
            a_s = fox_sample(qs.reshape(dbs, dseq, d), kss.reshape(dbs, dseq, d), vss.reshape(dbs, dseq, d),
                             g_new, ck, cv, clf, page_table, j)
            xs, xs_bf = out_proj_dnln(a_s.reshape(ts, d), None, w_out, xs, g1, b1, tm_s)
            outs["kp"].append(kp.reshape(bsz, seq, N_HEADS, HEAD_DIM))
            outs["vp"].append(vp.reshape(bsz, seq, N_HEADS, HEAD_DIM))
            outs["lp"].append(lfp.reshape(bsz, seq, N_HEADS))
            outs["ks"].append(kss.reshape(dbs, dseq, N_HEADS, HEAD_DIM))
            outs["vs"].append(vss.reshape(dbs, dseq, N_HEADS, HEAD_DIM))
            outs["ls"].append(lfs.reshape(dbs, dseq, N_HEADS))
        u_bf = peer_u[li].astype(BF16)
        vt_bf = peer_v[li].T.astype(BF16)
        wq_bf = peer_w_q[li].astype(BF16)
        mp = peer(xp_bf, wq_bf, peer_sub_keys[li], u_bf, vt_bf, tm_p, min(tm_p, PEER_ROUTE_TILE))
        xp, xp_bf = dn_layer_norm(xp, mp, g2, b2, tm_p)
        xs_rep = jnp.tile(xs_bf, (peer_rep, 1))
        t_rep = xs_rep.shape[0]
        ms = peer(xs_rep, wq_bf, peer_sub_keys[li], u_bf, vt_bf, t_rep, t_rep)[:ts]
        xs, xs_bf = dn_layer_norm(xs, ms, g2, b2, tm_s)

    stack = lambda name, dtype: jnp.stack(outs[name]).astype(dtype)
    return (xp.reshape(bsz, seq, d), xs.reshape(dbs, dseq, d),
            stack("kp", cache_k.dtype), stack("vp", cache_v.dtype), stack("lp", cache_logf.dtype),
            stack("ks", cache_k.dtype), stack("vs", cache_v.dtype), stack("ls", cache_logf.dtype),
            stack("s5p_re", state_s5_re.dtype), stack("s5p_im", state_s5_im.dtype),
            stack("s5s_re", state_s5_re.dtype), stack("s5s_im", state_s5_im.dtype),
            jnp.stack(outs["gmv"]))
```

```python
import functools
import math

import jax
import jax.numpy as jnp
from jax import lax
from jax.experimental import pallas as pl
from jax.experimental.pallas import tpu as pltpu

F32 = jnp.float32
BF16 = jnp.bfloat16
HIGHEST = lax.Precision.HIGHEST

D_MODEL = 2048
DEPTH = 4
PAGE_SIZE = 128
S5_WIDTH = 1024
S5_GROUP = 16
S5_GROUPS = 64
S5_STATE = 64
S5_GROUP_BLOCK = 8
S5_BLOCKS = S5_GROUPS // S5_GROUP_BLOCK
S5_BLOCK_STATE = S5_GROUP_BLOCK * S5_STATE
GM_WIDTH = 1024
GM_CHUNK = 128
GM_HEADS = 8
GM_HEAD_DIM = 128
N_HEADS = 16
HEAD_DIM = 128
ATTN_SCALE = HEAD_DIM ** -0.5
TOKEN_TILE = 512
S5_CHUNK = 512
PEER_EXPERT_TILE = 512
PEER_ROUTE_TILE = 512
FOX_BLOCK = 512
FOX_HEADS_PER_STEP = 4
FOX_PAGES_PER_STEP = 4
PEER_HEADS = 8
PEER_KEYS = 128
PEER_TOPK = 16
DN_ALPHA = (2.0 * DEPTH) ** 0.25
LN_EPS = 1e-5
SUBLANES = 8
LANES = 128
VMEM_LIMIT = 56 * 1024 * 1024
NEG_INF = float("-inf")


def _params(*sem):
    return pltpu.CompilerParams(dimension_semantics=sem, vmem_limit_bytes=VMEM_LIMIT)


def _layer_norm(y, g, b):
    mu = jnp.mean(y, axis=-1, keepdims=True)
    yc = y - mu
    var = jnp.mean(yc * yc, axis=-1, keepdims=True)
    return yc * lax.rsqrt(var + LN_EPS) * g + b


def _mm_kernel(x_ref, w_ref, o_ref):
    o_ref[...] = jnp.dot(x_ref[...].astype(BF16), w_ref[...], preferred_element_type=F32)


def matmul(x, w_bf, col_block, n_out, tm):
    t, k = x.shape
    assert t % tm == 0
    return pl.pallas_call(
        _mm_kernel,
        grid=(t // tm,),
        in_specs=[pl.BlockSpec((tm, k), lambda i: (i, 0)),
                  pl.BlockSpec((k, n_out), lambda i: (0, col_block))],
        out_specs=pl.BlockSpec((tm, n_out), lambda i: (i, 0)),
        out_shape=jax.ShapeDtypeStruct((t, n_out), F32),
        compiler_params=_params("parallel"),
        name="matmul",
    )(x, w_bf)


def _store_norm(y, g_ref, beta_ref, o_ref, ob_ref):
    out = _layer_norm(y, g_ref[...], beta_ref[...])
    o_ref[...] = out
    ob_ref[...] = out.astype(BF16)


def _mm1_dnln_kernel(a_ref, w_ref, x_ref, g_ref, beta_ref, o_ref, ob_ref):
    m = jnp.dot(a_ref[...].astype(BF16), w_ref[...], preferred_element_type=F32)
    _store_norm(DN_ALPHA * x_ref[...] + m, g_ref, beta_ref, o_ref, ob_ref)


def _mm2_dnln_kernel(a_ref, b_ref, wa_ref, wb_ref, x_ref, g_ref, beta_ref, o_ref, ob_ref):
    m = jnp.dot(a_ref[...].astype(BF16), wa_ref[...], preferred_element_type=F32)
    m += jnp.dot(b_ref[...].astype(BF16), wb_ref[...], preferred_element_type=F32)
    _store_norm(DN_ALPHA * x_ref[...] + m, g_ref, beta_ref, o_ref, ob_ref)


def _norm_outputs(t, d, tm):
    row = lambda i: (i, 0)
    return ([pl.BlockSpec((tm, d), row), pl.BlockSpec((tm, d), row)],
            [jax.ShapeDtypeStruct((t, d), F32), jax.ShapeDtypeStruct((t, d), BF16)])


def out_proj_dnln(a, b, w_bf, x, g, beta, tm):
    t, ka = a.shape
    d = w_bf.shape[1]
    row = lambda i: (i, 0)
    fixed = lambda i: (0, 0)
    tail = [pl.BlockSpec((tm, d), row), pl.BlockSpec((1, d), fixed), pl.BlockSpec((1, d), fixed)]
    if b is None:
        kern = _mm1_dnln_kernel
        ins = [a, w_bf, x, g, beta]
        specs = [pl.BlockSpec((tm, ka), row), pl.BlockSpec((ka, d), fixed)] + tail
    else:
        assert b.shape[1] == ka
        kern = _mm2_dnln_kernel
        ins = [a, b, w_bf, w_bf, x, g, beta]
        specs = [pl.BlockSpec((tm, ka), row), pl.BlockSpec((tm, ka), row),
                 pl.BlockSpec((ka, d), fixed), pl.BlockSpec((ka, d), lambda i: (1, 0))] + tail
    out_specs, out_shape = _norm_outputs(t, d, tm)
    return pl.pallas_call(
        kern, grid=(t // tm,), in_specs=specs, out_specs=out_specs, out_shape=out_shape,
        compiler_params=_params("parallel"),
        name="out_proj_dnln",
    )(*ins)


def _dnln_kernel(x_ref, m_ref, g_ref, b_ref, o_ref, ob_ref):
    _store_norm(DN_ALPHA * x_ref[...] + m_ref[...], g_ref, b_ref, o_ref, ob_ref)


def dn_layer_norm(x, m, g, b, tm):
    t, d = x.shape
    row = lambda i: (i, 0)
    fixed = lambda i: (0, 0)
    out_specs, out_shape = _norm_outputs(t, d, tm)
    return pl.pallas_call(
        _dnln_kernel, grid=(t // tm,),
        in_specs=[pl.BlockSpec((tm, d), row), pl.BlockSpec((tm, d), row),
                  pl.BlockSpec((1, d), fixed), pl.BlockSpec((1, d), fixed)],
        out_specs=out_specs, out_shape=out_shape,
        compiler_params=_params("parallel"),
        name="dn_layer_norm",
    )(x, m, g, b)


def _split_bf16(x):
    hi = lax.bitcast_convert_type(lax.bitcast_convert_type(x, jnp.uint32) & jnp.uint32(0xFFFF0000), F32)
    return hi.astype(BF16), (x - hi).astype(BF16)


def _dot_split(a, b_ref):
    a_hi, a_lo = _split_bf16(a)
    dot = functools.partial(jnp.dot, preferred_element_type=F32)
    return dot(a_hi, b_ref[0]) + (dot(a_lo, b_ref[0]) + dot(a_hi, b_ref[1]))


def _s5_kernel(u_ref, bd_ref, cd_ref, tab_ref, d_ref, h0_ref, y_ref, hl_ref, hs_sc, carry_sc, *, lc):
    ns = S5_BLOCK_STATE

    @pl.when(pl.program_id(2) == 0)
    def _():
        carry_sc[...] = h0_ref[...]

    u = u_ref[...]
    hs_sc[...] = _dot_split(u, bd_ref)

    def row_group(r, carry):
        cr, ci = carry
        rows = pl.ds(pl.multiple_of(r * SUBLANES, SUBLANES), SUBLANES)
        xr = hs_sc[rows, :ns]
        xi = hs_sc[rows, ns:]
        for k in range(3):
            ar = tab_ref[2 * k]
            ai = tab_ref[2 * k + 1]
            sr = pltpu.roll(xr, 1 << k, 0)
            si = pltpu.roll(xi, 1 << k, 0)
            xr, xi = xr + (ar * sr - ai * si), xi + (ar * si + ai * sr)
        pr = tab_ref[6]
        pi = tab_ref[7]
        xr, xi = xr + (pr * cr - pi * ci), xi + (pr * ci + pi * cr)
        hs_sc[rows, :ns] = xr
        hs_sc[rows, ns:] = xi
        return xr[SUBLANES - 1:, :], xi[SUBLANES - 1:, :]

    cr, ci = lax.fori_loop(0, lc // SUBLANES, row_group, (carry_sc[:, :ns], carry_sc[:, ns:]))
    carry_sc[:, :ns] = cr
    carry_sc[:, ns:] = ci
    hl_ref[...] = carry_sc[...]
    y_ref[...] = _dot_split(hs_sc[...], cd_ref) + d_ref[...] * u


def _s5_tables(lam_re, lam_im, log_dt, b_re, b_im, c_re, c_im):
    dt = jnp.exp(log_dt.astype(F32))[:, None]
    mag = jnp.exp(lam_re * dt)
    lbr = mag * jnp.cos(lam_im * dt)
    lbi = mag * jnp.sin(lam_im * dt)
    den = lam_re * lam_re + lam_im * lam_im
    qr = ((lbr - 1.0) * lam_re + lbi * lam_im) / den
    qi = (lbi * lam_re - (lbr - 1.0) * lam_im) / den
    bbr = qr[..., None] * b_re - qi[..., None] * b_im
    bbi = qr[..., None] * b_im + qi[..., None] * b_re
    nb, gb = S5_BLOCKS, S5_GROUP_BLOCK
    eye = jnp.eye(gb, dtype=F32)

    def bdiag(m):
        m = m.reshape(nb, gb, S5_STATE, S5_GROUP)
        return jnp.einsum('ngpc,gh->ngchp', m, eye).reshape(nb, gb * S5_GROUP, gb * S5_STATE)

    def cdiag(m):
        m = m.reshape(nb, gb, S5_GROUP, S5_STATE)
        return jnp.einsum('ngcp,gh->ngphc', m, eye).reshape(nb, gb * S5_STATE, gb * S5_GROUP)

    bd = jnp.stack(_split_bf16(jnp.concatenate([bdiag(bbr), bdiag(bbi)], axis=-1)), axis=1)
    cd = jnp.stack(_split_bf16(jnp.concatenate([cdiag(c_re), cdiag(-c_im)], axis=1)), axis=1)

    def cmul(a, b):
        return a[0] * b[0] - a[1] * b[1], a[0] * b[1] + a[1] * b[0]

    powers = [(lbr, lbi)]
    for _ in range(SUBLANES - 1):
        powers.append(cmul(powers[-1], (lbr, lbi)))
    row = jnp.arange(SUBLANES)[:, None, None]
    tabs = []
    for k in range(3):
        sh = 1 << k
        for part in powers[sh - 1]:
            tabs.append(jnp.where(row >= sh, part[None], 0.0))
    tabs.append(jnp.stack([p[0] for p in powers]))
    tabs.append(jnp.stack([p[1] for p in powers]))
    tab = jnp.stack(tabs)
    tab = tab.reshape(8, SUBLANES, nb, S5_BLOCK_STATE).transpose(2, 0, 1, 3)
    return bd, cd, tab


def s5_scan(proj, h0_re, h0_im, tables, d_skip, lc):
    bd, cd, tab = tables
    bsz, seq = proj.shape[0], proj.shape[1]
    nb, ns = S5_BLOCKS, S5_BLOCK_STATE
    h0 = jnp.concatenate([h0_re.reshape(bsz, nb, 1, ns), h0_im.reshape(bsz, nb, 1, ns)], axis=-1)
    y, hl = pl.pallas_call(
        functools.partial(_s5_kernel, lc=lc),
        grid=(bsz, nb, seq // lc),
        in_specs=[pl.BlockSpec((None, lc, LANES), lambda b, g, l: (b, l, g)),
                  pl.BlockSpec((None, 2, LANES, 2 * ns), lambda b, g, l: (g, 0, 0, 0)),
                  pl.BlockSpec((None, 2, 2 * ns, LANES), lambda b, g, l: (g, 0, 0, 0)),
                  pl.BlockSpec((None, 8, SUBLANES, ns), lambda b, g, l: (g, 0, 0, 0)),
                  pl.BlockSpec((1, LANES), lambda b, g, l: (0, g)),
                  pl.BlockSpec((None, None, 1, 2 * ns), lambda b, g, l: (b, g, 0, 0))],
        out_specs=[pl.BlockSpec((None, lc, LANES), lambda b, g, l: (b, l, g)),
                   pl.BlockSpec((None, None, 1, 2 * ns), lambda b, g, l: (b, g, 0, 0))],
        out_shape=[jax.ShapeDtypeStruct((bsz, seq, S5_WIDTH), F32),
                   jax.ShapeDtypeStruct((bsz, nb, 1, 2 * ns), F32)],
        scratch_shapes=[pltpu.VMEM((lc, 2 * ns), F32), pltpu.VMEM((1, 2 * ns), F32)],
        compiler_params=_params("parallel", "parallel", "arbitrary"),
        name="s5_scan",
    )(proj, bd, cd, tab, d_skip.reshape(1, S5_WIDTH), h0)
    hl_re = hl[..., :ns].reshape(bsz, S5_GROUPS, S5_STATE)
    hl_im = hl[..., ns:].reshape(bsz, S5_GROUPS, S5_STATE)
    return y, hl_re, hl_im


def _glu_kernel(y_ref, w_ref, b_ref, o_ref):
    y = jax.nn.gelu(y_ref[...])
    z = jnp.dot(y.astype(BF16), w_ref[...], preferred_element_type=F32) + b_ref[...]
    o_ref[...] = y * jax.nn.sigmoid(z)


def s5_glu(y, w, b, tm):
    t, k = y.shape
    return pl.pallas_call(
        _glu_kernel, grid=(t // tm,),
        in_specs=[pl.BlockSpec((tm, k), lambda i: (i, 0)), pl.BlockSpec((k, k), lambda i: (0, 0)),
                  pl.BlockSpec((1, k), lambda i: (0, 0))],
        out_specs=pl.BlockSpec((tm, k), lambda i: (i, 0)),
        out_shape=jax.ShapeDtypeStruct((t, k), F32),
        compiler_params=_params("parallel"),
        name="s5_glu",
    )(y, w, b.reshape(1, k))


def _gmlp_kernel(zu_ref, zv_ref, g_ref, b_ref, w_ref, bs_ref, o_ref, v_ref, *, lc):
    u = jax.nn.gelu(zu_ref[...])
    v = _layer_norm(jax.nn.gelu(zv_ref[...]), g_ref[...], b_ref[...])
    v_ref[...] = v
    row = lax.broadcasted_iota(jnp.int32, (lc, lc), 0)
    col = lax.broadcasted_iota(jnp.int32, (lc, lc), 1)
    causal = row >= col
    vb = v.astype(BF16)
    bs = bs_ref[...]
    for h in range(GM_HEADS):
        cols = slice(h * GM_HEAD_DIM, (h + 1) * GM_HEAD_DIM)
        w = jnp.where(causal, w_ref[h], 0.0).astype(BF16)
        mixed = jnp.dot(w, vb[:, cols], preferred_element_type=F32) + bs[:, h:h + 1]
        o_ref[:, cols] = u[:, cols] * mixed


def chunk_gmlp(proj, ln_g, ln_b, w_s, b_s, lc):
    bsz, seq = proj.shape[0], proj.shape[1]
    w = w_s[:, :lc, :lc]
    bs = b_s[:, :lc].T
    fixed2 = lambda b, n: (0, 0)
    return pl.pallas_call(
        functools.partial(_gmlp_kernel, lc=lc),
        grid=(bsz, seq // lc),
        in_specs=[pl.BlockSpec((None, lc, GM_WIDTH), lambda b, n: (b, n, 1)),
                  pl.BlockSpec((None, lc, GM_WIDTH), lambda b, n: (b, n, 2)),
                  pl.BlockSpec((1, GM_WIDTH), fixed2), pl.BlockSpec((1, GM_WIDTH), fixed2),
                  pl.BlockSpec((GM_HEADS, lc, lc), lambda b, n: (0, 0, 0)),
                  pl.BlockSpec((lc, GM_HEADS), fixed2)],
        out_specs=[pl.BlockSpec((None, lc, GM_WIDTH), lambda b, n: (b, n, 0)),
                   pl.BlockSpec((None, lc, GM_WIDTH), lambda b, n: (b, n, 0))],
        out_shape=[jax.ShapeDtypeStruct((bsz, seq, GM_WIDTH), F32),
                   jax.ShapeDtypeStruct((bsz, seq, GM_WIDTH), F32)],
        compiler_params=_params("parallel", "parallel"),
        name="chunk_gmlp",
    )(proj, proj, ln_g.reshape(1, GM_WIDTH), ln_b.reshape(1, GM_WIDTH), w, bs)


def _logf_kernel(x_ref, w_ref, b_ref, o_ref):
    z = jnp.dot(x_ref[...].astype(BF16), w_ref[...].astype(BF16), preferred_element_type=F32)
    o_ref[...] = jax.nn.log_sigmoid(z + b_ref[...])


def forget_log_gate(x, w_f, b_f, tm):
    t, k = x.shape
    return pl.pallas_call(
        _logf_kernel, grid=(t // tm,),
        in_specs=[pl.BlockSpec((tm, k), lambda i: (i, 0)), pl.BlockSpec((k, N_HEADS), lambda i: (0, 0)),
                  pl.BlockSpec((1, N_HEADS), lambda i: (0, 0))],
        out_specs=pl.BlockSpec((tm, N_HEADS), lambda i: (i, 0)),
        out_shape=jax.ShapeDtypeStruct((t, N_HEADS), F32),
        compiler_params=_params("parallel"),
        name="forget_log_gate",
    )(x, w_f, b_f.reshape(1, N_HEADS))


def _fox_prompt_kernel(q_ref, k_ref, v_ref, fq_ref, fk_ref, o_ref, m_sc, l_sc, acc_sc, *, tq):
    hg = pl.program_id(1)
    qi = pl.program_id(2)
    ki = pl.program_id(3)

    @pl.when(ki == 0)
    def _():
        m_sc[...] = jnp.full_like(m_sc, NEG_INF)
        l_sc[...] = jnp.zeros_like(l_sc)
        acc_sc[...] = jnp.zeros_like(acc_sc)

    def block(on_diagonal):
        for hl in range(FOX_HEADS_PER_STEP):
            cols = slice(hl * HEAD_DIM, (hl + 1) * HEAD_DIM)
            q = (q_ref[:, cols] * ATTN_SCALE).astype(BF16)
            s = lax.dot_general(q, k_ref[:, cols].astype(BF16), (((1,), (1,)), ((), ())),
                                preferred_element_type=F32)
            head = lax.broadcasted_iota(jnp.int32, (tq, N_HEADS), 1) == hg * FOX_HEADS_PER_STEP + hl
            fq = jnp.sum(jnp.where(head, fq_ref[...], 0.0), axis=-1, keepdims=True)
            s = s + fq - fk_ref[hl]
            if on_diagonal:
                causal = (lax.broadcasted_iota(jnp.int32, (tq, tq), 0)
                          >= lax.broadcasted_iota(jnp.int32, (tq, tq), 1))
                s = jnp.where(causal, s, NEG_INF)
            m_old = m_sc[hl]
            m_new = jnp.maximum(m_old, jnp.max(s, axis=-1, keepdims=True))
            alpha = jnp.exp(m_old - m_new)
            p = jnp.exp(s - m_new)
            l_sc[hl] = alpha * l_sc[hl] + jnp.sum(p, axis=-1, keepdims=True)
            acc_sc[:, cols] = alpha * acc_sc[:, cols] + jnp.dot(p.astype(BF16), v_ref[:, cols].astype(BF16),
                                                                 preferred_element_type=F32)
            m_sc[hl] = m_new

    @pl.when(ki < qi)
    def _():
        block(False)

    @pl.when(ki == qi)
    def _():
        block(True)

    @pl.when(ki == pl.num_programs(3) - 1)
    def _():
        for hl in range(FOX_HEADS_PER_STEP):
            cols = slice(hl * HEAD_DIM, (hl + 1) * HEAD_DIM)
            o_ref[:, cols] = acc_sc[:, cols] / l_sc[hl]


def fox_prompt(q, k, v, f_cum, tq):
    bsz, seq = q.shape[0], q.shape[1]
    nq = seq // tq
    hps = FOX_HEADS_PER_STEP
    width = hps * HEAD_DIM
    f_rows = f_cum.transpose(0, 2, 1).reshape(bsz, N_HEADS, 1, seq)
    kv_map = lambda b, h, i, j: (b, jnp.minimum(j, i), h)
    return pl.pallas_call(
        functools.partial(_fox_prompt_kernel, tq=tq),
        grid=(bsz, N_HEADS // hps, nq, nq),
        in_specs=[pl.BlockSpec((None, tq, width), lambda b, h, i, j: (b, i, h)),
                  pl.BlockSpec((None, tq, width), kv_map),
                  pl.BlockSpec((None, tq, width), kv_map),
                  pl.BlockSpec((None, tq, N_HEADS), lambda b, h, i, j: (b, i, 0)),
                  pl.BlockSpec((None, hps, 1, tq), lambda b, h, i, j: (b, h, 0, jnp.minimum(j, i)))],
        out_specs=pl.BlockSpec((None, tq, width), lambda b, h, i, j: (b, i, h)),
        out_shape=jax.ShapeDtypeStruct((bsz, seq, D_MODEL), F32),
        scratch_shapes=[pltpu.VMEM((hps, tq, 1), F32), pltpu.VMEM((hps, tq, 1), F32),
                        pltpu.VMEM((tq, width), F32)],
        compiler_params=_params("parallel", "parallel", "parallel", "arbitrary"),
        name="fox_prompt",
    )(q, k, v, f_cum, f_rows)


def _fox_sample_kernel(pt_ref, q_ref, kn_ref, vn_ref, gc_ref, gr_ref, *rest, n_new, pps):
    kp_refs, vp_refs, lp_refs = rest[:pps], rest[pps:2 * pps], rest[2 * pps:3 * pps]
    o_ref, m_sc, l_sc, acc_sc, carry_sc = rest[3 * pps:]
    step = pl.program_id(1)
    nt = (((1,), (1,)), ((), ()))
    head_rows = lambda h: slice(h * n_new, (h + 1) * n_new)
    head_cols = lambda h: slice(h * HEAD_DIM, (h + 1) * HEAD_DIM)
    q_heads = [(q_ref[:, head_cols(h)] * ATTN_SCALE).astype(BF16) for h in range(N_HEADS)]

    def online_update(s, values):
        m_old = m_sc[...]
        m_new = jnp.maximum(m_old, jnp.max(s, axis=-1, keepdims=True))
        alpha = jnp.exp(m_old - m_new)
        p = jnp.exp(s - m_new)
        l_sc[...] = alpha * l_sc[...] + jnp.sum(p, axis=-1, keepdims=True)
        pv = [jnp.dot(p[head_rows(h), :].astype(BF16), values(h), preferred_element_type=F32)
              for h in range(N_HEADS)]
        acc_sc[...] = alpha * acc_sc[...] + jnp.concatenate(pv, axis=0)
        m_sc[...] = m_new

    @pl.when(step == 0)
    def _():
        m_sc[...] = jnp.full_like(m_sc, NEG_INF)
        l_sc[...] = jnp.zeros_like(l_sc)
        acc_sc[...] = jnp.zeros_like(acc_sc)
        carry_sc[...] = jnp.zeros_like(carry_sc)
        s = jnp.concatenate([lax.dot_general(q_heads[h], kn_ref[:, head_cols(h)].astype(BF16), nt,
                                             preferred_element_type=F32) for h in range(N_HEADS)], axis=0)
        s = s + gc_ref[...] - gr_ref[...]
        rows = N_HEADS * n_new
        causal = (lax.broadcasted_iota(jnp.int32, (rows, n_new), 0) % n_new
                  >= lax.broadcasted_iota(jnp.int32, (rows, n_new), 1))
        online_update(jnp.where(causal, s, NEG_INF), lambda h: vn_ref[:, head_cols(h)].astype(BF16))

    @pl.when(step > 0)
    def _():
        later = (lax.broadcasted_iota(jnp.int32, (PAGE_SIZE, PAGE_SIZE), 0)
                 > lax.broadcasted_iota(jnp.int32, (PAGE_SIZE, PAGE_SIZE), 1)).astype(F32)
        carry = carry_sc[...]
        page_scores = []
        for pg in range(pps):
            lp = lp_refs[pg][...]
            suffix = jnp.dot(lp, later, precision=HIGHEST, preferred_element_type=F32) + carry
            carry = carry + jnp.sum(lp, axis=-1, keepdims=True)
            qk = [lax.dot_general(q_heads[h], kp_refs[pg][pl.ds(h, PAGE_SIZE, stride=N_HEADS), :].astype(BF16),
                                  nt, preferred_element_type=F32)
                  + jnp.broadcast_to(suffix[h:h + 1, :], (n_new, PAGE_SIZE)) for h in range(N_HEADS)]
            page_scores.append(jnp.concatenate(qk, axis=0))
        carry_sc[...] = carry
        s = jnp.concatenate(page_scores, axis=1) + gc_ref[...]

        def values(h):
            return jnp.concatenate([vp_refs[pg][pl.ds(h, PAGE_SIZE, stride=N_HEADS), :].astype(BF16)
                                    for pg in range(pps)], axis=0)

        online_update(s, values)

    @pl.when(step == pl.num_programs(1) - 1)
    def _():
        out = acc_sc[...] / l_sc[...]
        for h in range(N_HEADS):
            o_ref[:, head_cols(h)] = out[head_rows(h), :]


def fox_sample(q, k, v, g_new, cache_k, cache_v, cache_logf_t, page_table, layer):
    dbs, n_new = q.shape[0], q.shape[1]
    n_pages = page_table.shape[1]
    pps = math.gcd(n_pages, FOX_PAGES_PER_STEP)
    rows = N_HEADS * n_new
    g_t = g_new.transpose(0, 2, 1)
    g_col = g_t.reshape(dbs, rows, 1)
    g_row = jnp.repeat(g_t, n_new, axis=1)

    def page_map(pg):
        def index(b, s, pt):
            first = jnp.maximum(s - 1, 0) * pps
            return (layer, pt[b, n_pages - 1 - (first + pg)], 0, 0)
        return index

    seq_map = lambda b, s, pt: (b, 0, 0)
    page_specs = lambda shape: [pl.BlockSpec((None, None) + shape, page_map(pg)) for pg in range(pps)]
    grid_spec = pltpu.PrefetchScalarGridSpec(
        num_scalar_prefetch=1,
        grid=(dbs, n_pages // pps + 1),
        in_specs=([pl.BlockSpec((None, n_new, D_MODEL), seq_map)] * 3
                  + [pl.BlockSpec((None, rows, 1), seq_map), pl.BlockSpec((None, rows, n_new), seq_map)]
                  + page_specs((PAGE_SIZE * N_HEADS, HEAD_DIM)) * 2
                  + page_specs((N_HEADS, PAGE_SIZE))),
        out_specs=pl.BlockSpec((None, n_new, D_MODEL), seq_map),
        scratch_shapes=[pltpu.VMEM((rows, 1), F32), pltpu.VMEM((rows, 1), F32),
                        pltpu.VMEM((rows, HEAD_DIM), F32), pltpu.VMEM((N_HEADS, 1), F32)],
    )
    return pl.pallas_call(
        functools.partial(_fox_sample_kernel, n_new=n_new, pps=pps),
        grid_spec=grid_spec,
        out_shape=jax.ShapeDtypeStruct((dbs, n_new, D_MODEL), F32),
        compiler_params=_params("parallel", "arbitrary"),
        name="fox_sample",
    )(page_table, q, k, v, g_col, g_row, *([cache_k] * pps), *([cache_v] * pps), *([cache_logf_t] * pps))


def _take_top(s, count):
    tops = []
    for i in range(count):
        m = jnp.max(s, axis=0, keepdims=True)
        tops.append(m)
        if i + 1 < count:
            s = jnp.where(s == m, NEG_INF, s)
    return tops


def _candidate_sums(top1, top2):
    limit = PEER_TOPK + 1
    t2_head = jnp.concatenate(top2[:SUBLANES], axis=0)
    row = lax.broadcasted_iota(jnp.int32, t2_head.shape, 0)
    pieces = [top1[0] + jnp.concatenate(top2, axis=0)]
    for i in range(1, SUBLANES):
        pieces.append(jnp.where(row < limit // (i + 1), top1[i] + t2_head, NEG_INF))
    assert limit // (SUBLANES + 1) == 1
    pieces.append(jnp.concatenate(top1[SUBLANES:], axis=0) + top2[0])
    return jnp.concatenate(pieces, axis=0)


def _peer_route_kernel(q_ref, keys_ref, r2_ref, e2_ref, n2_ref, e1_ref, *, n_sub):
    nt = (((1,), (1,)), ((), ()))
    for sub in range(n_sub):
        tok = slice(sub * LANES, (sub + 1) * LANES) if n_sub > 1 else slice(None)
        q = q_ref[tok, :].astype(BF16)
        s1 = lax.dot_general(keys_ref[0].astype(BF16), q[:, :PEER_KEYS], nt, preferred_element_type=F32)
        s2 = lax.dot_general(keys_ref[1].astype(BF16), q[:, PEER_KEYS:], nt, preferred_element_type=F32)
        top1 = _take_top(s1, PEER_TOPK)
        top2 = _take_top(s2, PEER_TOPK)
        best = _take_top(_candidate_sums(top1, top2), PEER_TOPK + 1)
        z = jnp.ones_like(best[0])
        for m in best[1:PEER_TOPK]:
            z += jnp.exp(m - best[0])
        cut = 0.5 * (best[PEER_TOPK - 1] + best[PEER_TOPK])
        r2 = jnp.zeros_like(s2)
        n2 = jnp.zeros_like(s1)
        for t2 in top2:
            r2 += jnp.where(t2 > s2, 1.0, 0.0)
            n2 += jnp.where(s1 + t2 > cut, 1.0, 0.0)
        r2_ref[:, tok] = r2.astype(BF16)
        e2_ref[:, tok] = jnp.where(s2 >= top2[PEER_TOPK - 1], jnp.exp(s2 - top2[0]), 0.0).astype(BF16)
        n2_ref[:, tok] = n2
        e1_ref[:, tok] = jnp.where(s1 >= top1[PEER_TOPK - 1], jnp.exp(s1 - top1[0]) / z, 0.0)


def peer_route(q, sub_keys, tm):
    t = q.shape[0]
    n_sub = max(tm // LANES, 1)
    out = lambda dtype: jax.ShapeDtypeStruct((PEER_HEADS, PEER_KEYS, t), dtype)
    ospec = pl.BlockSpec((None, PEER_KEYS, tm), lambda i, h: (h, 0, i))
    return pl.pallas_call(
        functools.partial(_peer_route_kernel, n_sub=n_sub),
        grid=(t // tm, PEER_HEADS),
        in_specs=[pl.BlockSpec((tm, 2 * PEER_KEYS), lambda i, h: (i, h)),
                  pl.BlockSpec((None, 2, PEER_KEYS, PEER_KEYS), lambda i, h: (h, 0, 0, 0))],
        out_specs=[ospec, ospec, ospec, ospec],
        out_shape=[out(BF16), out(BF16), out(F32), out(F32)],
        compiler_params=_params("parallel", "parallel"),
        name="peer_route",
    )(q, sub_keys)


def _peer_dense_kernel(x_ref, u_ref, vt_ref, r2_ref, e2_ref, n2_ref, e1_ref, o_ref, acc_sc, z_sc, g_sc, *, te, tm):
    k = pl.program_id(1)
    a_per_tile = te // PEER_KEYS
    rows_per_chunk = 2 * SUBLANES
    lanes_per_chunk = min(tm, 2 * LANES)
    n_chunks = PEER_KEYS // rows_per_chunk
    chunk_shape = (rows_per_chunk, lanes_per_chunk)

    @pl.when(k == 0)
    def _():
        acc_sc[...] = jnp.zeros_like(acc_sc)

    z_sc[...] = lax.dot_general(u_ref[...], x_ref[...], (((1,), (1,)), ((), ())), preferred_element_type=F32)
    for al in range(a_per_tile):
        for t0 in range(0, tm, lanes_per_chunk):
            tok = slice(t0, t0 + lanes_per_chunk)
            gates = [jnp.zeros(chunk_shape, BF16) for _ in range(n_chunks)]
            for h in range(PEER_HEADS):
                n2 = jnp.broadcast_to(n2_ref[h, al:al + 1, tok], chunk_shape).astype(BF16)
                e1 = jnp.broadcast_to(e1_ref[h, al:al + 1, tok], chunk_shape).astype(BF16)
                for c in range(n_chunks):
                    rows = slice(c * rows_per_chunk, (c + 1) * rows_per_chunk)
                    keep = r2_ref[h, rows, tok] < n2
                    gates[c] += jnp.where(keep, e2_ref[h, rows, tok] * e1, jnp.zeros(chunk_shape, BF16))
            for c in range(n_chunks):
                zrows = slice(al * PEER_KEYS + c * rows_per_chunk, al * PEER_KEYS + (c + 1) * rows_per_chunk)
                g_sc[zrows, tok] = (gates[c].astype(F32) * jax.nn.gelu(z_sc[zrows, tok])).astype(BF16)
    acc_sc[...] += jnp.dot(vt_ref[...], g_sc[...], preferred_element_type=F32)

    @pl.when(k == pl.num_programs(1) - 1)
    def _():
        o_ref[...] = acc_sc[...].T


def peer_dense(x_bf, u_bf, vt_bf, routes, tm, te):
    t, d = x_bf.shape
    n_tiles = vt_bf.shape[0]
    assert vt_bf.shape == (n_tiles, d, te)
    r2, e2, n2, e1 = routes
    a_per_tile = te // PEER_KEYS
    n2, e1 = (r.reshape(PEER_HEADS, n_tiles, a_per_tile, t) for r in (n2, e1))
    bspec = pl.BlockSpec((PEER_HEADS, PEER_KEYS, tm), lambda i, k: (0, 0, i))
    aspec = pl.BlockSpec((PEER_HEADS, None, a_per_tile, tm), lambda i, k: (0, k, 0, i))
    return pl.pallas_call(
        functools.partial(_peer_dense_kernel, te=te, tm=tm),
        grid=(t // tm, n_tiles),
        in_specs=[pl.BlockSpec((tm, d), lambda i, k: (i, 0)),
                  pl.BlockSpec((te, d), lambda i, k: (k, 0)),
                  pl.BlockSpec((None, d, te), lambda i, k: (k, 0, 0)),
                  bspec, bspec, aspec, aspec],
        out_specs=pl.BlockSpec((tm, d), lambda i, k: (i, 0)),
        out_shape=jax.ShapeDtypeStruct((t, d), F32),
        scratch_shapes=[pltpu.VMEM((d, tm), F32), pltpu.VMEM((te, tm), F32), pltpu.VMEM((te, tm), BF16)],
        compiler_params=_params("parallel", "arbitrary"),
        name="peer_dense",
    )(x_bf, u_bf, vt_bf, r2, e2, n2, e1)


def peer(x_bf, wq_bf, sub_keys, u_bf, vt_bf, tm, tm_route):
    q = matmul(x_bf, wq_bf, 0, wq_bf.shape[1], tm)
    routes = peer_route(q, sub_keys, tm_route)
    return peer_dense(x_bf, u_bf, vt_bf, routes, tm, PEER_EXPERT_TILE)


def _row_tile(t):
    tm = min(t, TOKEN_TILE)
    assert t % tm == 0
    return tm


def kernel(x_prompt, x_sample, cache_k, cache_v, cache_logf, page_table, state_s5_re, state_s5_im, w_in_even, s5_lambda_re, s5_lambda_im, s5_log_dt, s5_b_re, s5_b_im, s5_c_re, s5_c_im, s5_d, s5_w_glu, s5_b_glu, gm_ln_g, gm_ln_b, gm_w_s, gm_b_s, w_out_even, w_in_odd, b_f, w_out_odd, ln1_g, ln1_b, ln2_g, ln2_b, peer_w_q, peer_sub_keys, peer_u, peer_v):
    bsz, seq, d = x_prompt.shape
    dbs, dseq, _ = x_sample.shape
    n_odd, n_phys = cache_k.shape[0], cache_k.shape[1]
    tp, ts = bsz * seq, dbs * dseq
    xp = x_prompt.reshape(tp, d)
    xs = x_sample.reshape(ts, d)
    xp_bf, xs_bf = xp.astype(BF16), xs.astype(BF16)
    tm_p = _row_tile(tp)
    tm_s = _row_tile(ts)
    ck = cache_k.reshape(n_odd, n_phys, PAGE_SIZE * N_HEADS, HEAD_DIM)
    cv = cache_v.reshape(n_odd, n_phys, PAGE_SIZE * N_HEADS, HEAD_DIM)
    clf = cache_logf.transpose(0, 1, 3, 2)
    zero_state = jnp.zeros((bsz, S5_GROUPS, S5_STATE), F32)
    peer_rep = max(LANES // ts, 1)

    outs = {k: [] for k in ("kp", "vp", "lp", "ks", "vs", "ls", "s5p_re", "s5p_im", "s5s_re", "s5s_im", "gmv")}
    for li in range(DEPTH):
        j = li // 2
        g1, b1 = ln1_g[li].reshape(1, d), ln1_b[li].reshape(1, d)
        g2, b2 = ln2_g[li].reshape(1, d), ln2_b[li].reshape(1, d)
        if li % 2 == 0:
            tables = _s5_tables(s5_lambda_re[j], s5_lambda_im[j], s5_log_dt[j], s5_b_re[j], s5_b_im[j],
                                s5_c_re[j], s5_c_im[j])
            n_in = w_in_even.shape[2]
            w_in, w_glu, w_out = (w.astype(BF16) for w in (w_in_even[j], s5_w_glu[j], w_out_even[j]))

            def even(x, x_bf, nb, nl, h0_re, h0_im, tm):
                proj = matmul(x_bf, w_in, 0, n_in, tm).reshape(nb, nl, n_in)
                y, hl_re, hl_im = s5_scan(proj, h0_re, h0_im, tables, s5_d[j], min(nl, S5_CHUNK))
                a_out = s5_glu(y.reshape(nb * nl, S5_WIDTH), w_glu, s5_b_glu[j], tm)
                g_out, v_rows = chunk_gmlp(proj, gm_ln_g[j], gm_ln_b[j], gm_w_s[j], gm_b_s[j], min(nl, GM_CHUNK))
                x, x_bf = out_proj_dnln(a_out, g_out.reshape(nb * nl, GM_WIDTH), w_out, x, g1, b1, tm)
                return x, x_bf, hl_re, hl_im, v_rows

            xp, xp_bf, hp_re, hp_im, _ = even(xp, xp_bf, bsz, seq, zero_state, zero_state, tm_p)
            xs, xs_bf, hs_re, hs_im, v_rows = even(xs, xs_bf, dbs, dseq, state_s5_re[j].astype(F32),
                                                   state_s5_im[j].astype(F32), tm_s)
            outs["s5p_re"].append(hp_re)
            outs["s5p_im"].append(hp_im)
            outs["s5s_re"].append(hs_re)
            outs["s5s_im"].append(hs_im)
            outs["gmv"].append(v_rows)
        else:
            w_f = w_in_odd[j][:, 3 * d:]
            w_qkv, w_out = w_in_odd[j].astype(BF16), w_out_odd[j].astype(BF16)

            def project(x_bf, tm):
                q, k, v = (matmul(x_bf, w_qkv, part, d, tm) for part in range(3))
                return q, k, v, forget_log_gate(x_bf, w_f, b_f[j], tm)

            qp, kp, vp, lfp = project(xp_bf, tm_p)
            f_cum = jnp.cumsum(lfp.reshape(bsz, seq, N_HEADS), axis=1)
            ap = fox_prompt(qp.reshape(bsz, seq, d), kp.reshape(bsz, seq, d), vp.reshape(bsz, seq, d), f_cum,
                            min(seq, FOX_BLOCK))
            xp, xp_bf = out_proj_dnln(ap.reshape(tp, d), None, w_out, xp, g1, b1, tm_p)
            qs, kss, vss, lfs = project(xs_bf, tm_s)
            g_new = jnp.cumsum(lfs.reshape(dbs, dseq, N_HEADS), axis=1)
            a_s = fox_sample(qs.reshape(dbs, dseq, d), kss.reshape(dbs, dseq, d), vss.reshape(dbs, dseq, d),
                             g_new, ck, cv, clf, page_table, j)
            xs, xs_bf = out_proj_dnln(a_s.reshape(ts, d), None, w_out, xs, g1, b1, tm_s)
            outs["kp"].append(kp.reshape(bsz, seq, N_HEADS, HEAD_DIM))
            outs["vp"].append(vp.reshape(bsz, seq, N_HEADS, HEAD_DIM))
            outs["lp"].append(lfp.reshape(bsz, seq, N_HEADS))
            outs["ks"].append(kss.reshape(dbs, dseq, N_HEADS, HEAD_DIM))
            outs["vs"].append(vss.reshape(dbs, dseq, N_HEADS, HEAD_DIM))
            outs["ls"].append(lfs.reshape(dbs, dseq, N_HEADS))
        u_bf = peer_u[li].astype(BF16)
        vt_bf = peer_v[li].reshape(-1, PEER_EXPERT_TILE, d).transpose(0, 2, 1).astype(BF16)
        wq_bf = peer_w_q[li].astype(BF16)
        mp = peer(xp_bf, wq_bf, peer_sub_keys[li], u_bf, vt_bf, tm_p, min(tm_p, PEER_ROUTE_TILE))
        xp, xp_bf = dn_layer_norm(xp, mp, g2, b2, tm_p)
        xs_rep = jnp.tile(xs_bf, (peer_rep, 1))
        t_rep = xs_rep.shape[0]
        ms = peer(xs_rep, wq_bf, peer_sub_keys[li], u_bf, vt_bf, t_rep, t_rep)[:ts]
        xs, xs_bf = dn_layer_norm(xs, ms, g2, b2, tm_s)

    stack = lambda name, dtype: jnp.stack(outs[name]).astype(dtype)
    return (xp.reshape(bsz, seq, d), xs.reshape(dbs, dseq, d),
            stack("kp", cache_k.dtype), stack("vp", cache_v.dtype), stack("lp", cache_logf.dtype),
            stack("ks", cache_k.dtype), stack("vs", cache_v.dtype), stack("ls", cache_logf.dtype),
            stack("s5p_re", state_s5_re.dtype), stack("s5p_im", state_s5_im.dtype),
            stack("s5s_re", state_s5_re.dtype), stack("s5s_im", state_s5_im.dtype),
            jnp.stack(outs["gmv"]))
```

```python
import functools
import math

import jax
import jax.numpy as jnp
from jax import lax
from jax.experimental import pallas as pl
from jax.experimental.pallas import tpu as pltpu

F32 = jnp.float32
BF16 = jnp.bfloat16
HIGHEST = lax.Precision.HIGHEST

D_MODEL = 2048
DEPTH = 4
PAGE_SIZE = 128
S5_WIDTH = 1024
S5_GROUP = 16
S5_GROUPS = 64
S5_STATE = 64
S5_GROUP_BLOCK = 8
S5_BLOCKS = S5_GROUPS // S5_GROUP_BLOCK
S5_BLOCK_STATE = S5_GROUP_BLOCK * S5_STATE
GM_WIDTH = 1024
GM_CHUNK = 128
GM_HEADS = 8
GM_HEAD_DIM = 128
N_HEADS = 16
HEAD_DIM = 128
ATTN_SCALE = HEAD_DIM ** -0.5
TOKEN_TILE = 512
S5_CHUNK = 512
PEER_EXPERT_TILE = 512
PEER_ROUTE_TILE = 512
FOX_BLOCK = 512
FOX_HEADS_PER_STEP = 4
FOX_PAGES_PER_STEP = 4
PEER_HEADS = 8
PEER_KEYS = 128
PEER_TOPK = 16
DN_ALPHA = (2.0 * DEPTH) ** 0.25
LN_EPS = 1e-5
SUBLANES = 8
LANES = 128
VMEM_LIMIT = 56 * 1024 * 1024
NEG_INF = float("-inf")


def _params(*sem):
    return pltpu.CompilerParams(dimension_semantics=sem, vmem_limit_bytes=VMEM_LIMIT)


def _layer_norm(y, g, b):
    mu = jnp.mean(y, axis=-1, keepdims=True)
    yc = y - mu
    var = jnp.mean(yc * yc, axis=-1, keepdims=True)
    return yc * lax.rsqrt(var + LN_EPS) * g + b


def _mm_kernel(x_ref, w_ref, o_ref):
    o_ref[...] = jnp.dot(x_ref[...].astype(BF16), w_ref[...], preferred_element_type=F32)


def matmul(x, w_bf, col_block, n_out, tm):
    t, k = x.shape
    assert t % tm == 0
    return pl.pallas_call(
        _mm_kernel,
        grid=(t // tm,),
        in_specs=[pl.BlockSpec((tm, k), lambda i: (i, 0)),
                  pl.BlockSpec((k, n_out), lambda i: (0, col_block))],
        out_specs=pl.BlockSpec((tm, n_out), lambda i: (i, 0)),
        out_shape=jax.ShapeDtypeStruct((t, n_out), F32),
        compiler_params=_params("parallel"),
        name="matmul",
    )(x, w_bf)


def _store_norm(y, g_ref, beta_ref, o_ref, ob_ref):
    out = _layer_norm(y, g_ref[...], beta_ref[...])
    o_ref[...] = out
    ob_ref[...] = out.astype(BF16)


def _mm1_dnln_kernel(a_ref, w_ref, x_ref, g_ref, beta_ref, o_ref, ob_ref):
    m = jnp.dot(a_ref[...].astype(BF16), w_ref[...], preferred_element_type=F32)
    _store_norm(DN_ALPHA * x_ref[...] + m, g_ref, beta_ref, o_ref, ob_ref)


def _mm2_dnln_kernel(a_ref, b_ref, wa_ref, wb_ref, x_ref, g_ref, beta_ref, o_ref, ob_ref):
    m = jnp.dot(a_ref[...].astype(BF16), wa_ref[...], preferred_element_type=F32)
    m += jnp.dot(b_ref[...].astype(BF16), wb_ref[...], preferred_element_type=F32)
    _store_norm(DN_ALPHA * x_ref[...] + m, g_ref, beta_ref, o_ref, ob_ref)


def _norm_outputs(t, d, tm):
    row = lambda i: (i, 0)
    return ([pl.BlockSpec((tm, d), row), pl.BlockSpec((tm, d), row)],
            [jax.ShapeDtypeStruct((t, d), F32), jax.ShapeDtypeStruct((t, d), BF16)])


def out_proj_dnln(a, b, w_bf, x, g, beta, tm):
    t, ka = a.shape
    d = w_bf.shape[1]
    row = lambda i: (i, 0)
    fixed = lambda i: (0, 0)
    tail = [pl.BlockSpec((tm, d), row), pl.BlockSpec((1, d), fixed), pl.BlockSpec((1, d), fixed)]
    if b is None:
        kern = _mm1_dnln_kernel
        ins = [a, w_bf, x, g, beta]
        specs = [pl.BlockSpec((tm, ka), row), pl.BlockSpec((ka, d), fixed)] + tail
    else:
        assert b.shape[1] == ka
        kern = _mm2_dnln_kernel
        ins = [a, b, w_bf, w_bf, x, g, beta]
        specs = [pl.BlockSpec((tm, ka), row), pl.BlockSpec((tm, ka), row),
                 pl.BlockSpec((ka, d), fixed), pl.BlockSpec((ka, d), lambda i: (1, 0))] + tail
    out_specs, out_shape = _norm_outputs(t, d, tm)
    return pl.pallas_call(
        kern, grid=(t // tm,), in_specs=specs, out_specs=out_specs, out_shape=out_shape,
        compiler_params=_params("parallel"),
        name="out_proj_dnln",
    )(*ins)


def _split_bf16(x):
    hi = lax.bitcast_convert_type(lax.bitcast_convert_type(x, jnp.uint32) & jnp.uint32(0xFFFF0000), F32)
    return hi.astype(BF16), (x - hi).astype(BF16)


def _dot_split(a, b_ref):
    a_hi, a_lo = _split_bf16(a)
    dot = functools.partial(jnp.dot, preferred_element_type=F32)
    return dot(a_hi, b_ref[0]) + (dot(a_lo, b_ref[0]) + dot(a_hi, b_ref[1]))


def _s5_kernel(u_ref, bd_ref, cd_ref, tab_ref, d_ref, h0_ref, y_ref, hl_ref, hs_sc, carry_sc, *, lc):
    ns = S5_BLOCK_STATE

    @pl.when(pl.program_id(2) == 0)
    def _():
        carry_sc[...] = h0_ref[...]

    u = u_ref[...]
    hs_sc[...] = _dot_split(u, bd_ref)

    def row_group(r, carry):
        cr, ci = carry
        rows = pl.ds(pl.multiple_of(r * SUBLANES, SUBLANES), SUBLANES)
        xr = hs_sc[rows, :ns]
        xi = hs_sc[rows, ns:]
        for k in range(3):
            ar = tab_ref[2 * k]
            ai = tab_ref[2 * k + 1]
            sr = pltpu.roll(xr, 1 << k, 0)
            si = pltpu.roll(xi, 1 << k, 0)
            xr, xi = xr + (ar * sr - ai * si), xi + (ar * si + ai * sr)
        pr = tab_ref[6]
        pi = tab_ref[7]
        xr, xi = xr + (pr * cr - pi * ci), xi + (pr * ci + pi * cr)
        hs_sc[rows, :ns] = xr
        hs_sc[rows, ns:] = xi
        return xr[SUBLANES - 1:, :], xi[SUBLANES - 1:, :]

    cr, ci = lax.fori_loop(0, lc // SUBLANES, row_group, (carry_sc[:, :ns], carry_sc[:, ns:]))
    carry_sc[:, :ns] = cr
    carry_sc[:, ns:] = ci
    hl_ref[...] = carry_sc[...]
    y_ref[...] = _dot_split(hs_sc[...], cd_ref) + d_ref[...] * u


def _s5_tables(lam_re, lam_im, log_dt, b_re, b_im, c_re, c_im):
    dt = jnp.exp(log_dt.astype(F32))[:, None]
    mag = jnp.exp(lam_re * dt)
    lbr = mag * jnp.cos(lam_im * dt)
    lbi = mag * jnp.sin(lam_im * dt)
    den = lam_re * lam_re + lam_im * lam_im
    qr = ((lbr - 1.0) * lam_re + lbi * lam_im) / den
    qi = (lbi * lam_re - (lbr - 1.0) * lam_im) / den
    bbr = qr[..., None] * b_re - qi[..., None] * b_im
    bbi = qr[..., None] * b_im + qi[..., None] * b_re
    nb, gb = S5_BLOCKS, S5_GROUP_BLOCK
    eye = jnp.eye(gb, dtype=F32)

    def bdiag(m):
        m = m.reshape(nb, gb, S5_STATE, S5_GROUP)
        return jnp.einsum('ngpc,gh->ngchp', m, eye).reshape(nb, gb * S5_GROUP, gb * S5_STATE)

    def cdiag(m):
        m = m.reshape(nb, gb, S5_GROUP, S5_STATE)
        return jnp.einsum('ngcp,gh->ngphc', m, eye).reshape(nb, gb * S5_STATE, gb * S5_GROUP)

    bd = jnp.stack(_split_bf16(jnp.concatenate([bdiag(bbr), bdiag(bbi)], axis=-1)), axis=1)
    cd = jnp.stack(_split_bf16(jnp.concatenate([cdiag(c_re), cdiag(-c_im)], axis=1)), axis=1)

    def cmul(a, b):
        return a[0] * b[0] - a[1] * b[1], a[0] * b[1] + a[1] * b[0]

    powers = [(lbr, lbi)]
    for _ in range(SUBLANES - 1):
        powers.append(cmul(powers[-1], (lbr, lbi)))
    row = jnp.arange(SUBLANES)[:, None, None]
    tabs = []
    for k in range(3):
        sh = 1 << k
        for part in powers[sh - 1]:
            tabs.append(jnp.where(row >= sh, part[None], 0.0))
    tabs.append(jnp.stack([p[0] for p in powers]))
    tabs.append(jnp.stack([p[1] for p in powers]))
    tab = jnp.stack(tabs)
    tab = tab.reshape(8, SUBLANES, nb, S5_BLOCK_STATE).transpose(2, 0, 1, 3)
    return bd, cd, tab


def s5_scan(proj, h0_re, h0_im, tables, d_skip, lc):
    bd, cd, tab = tables
    bsz, seq = proj.shape[0], proj.shape[1]
    nb, ns = S5_BLOCKS, S5_BLOCK_STATE
    h0 = jnp.concatenate([h0_re.reshape(bsz, nb, 1, ns), h0_im.reshape(bsz, nb, 1, ns)], axis=-1)
    y, hl = pl.pallas_call(
        functools.partial(_s5_kernel, lc=lc),
        grid=(nb, bsz, seq // lc),
        in_specs=[pl.BlockSpec((None, lc, LANES), lambda g, b, l: (b, l, g)),
                  pl.BlockSpec((None, 2, LANES, 2 * ns), lambda g, b, l: (g, 0, 0, 0)),
                  pl.BlockSpec((None, 2, 2 * ns, LANES), lambda g, b, l: (g, 0, 0, 0)),
                  pl.BlockSpec((None, 8, SUBLANES, ns), lambda g, b, l: (g, 0, 0, 0)),
                  pl.BlockSpec((1, LANES), lambda g, b, l: (0, g)),
                  pl.BlockSpec((None, None, 1, 2 * ns), lambda g, b, l: (b, g, 0, 0))],
        out_specs=[pl.BlockSpec((None, lc, LANES), lambda g, b, l: (b, l, g)),
                   pl.BlockSpec((None, None, 1, 2 * ns), lambda g, b, l: (b, g, 0, 0))],
        out_shape=[jax.ShapeDtypeStruct((bsz, seq, S5_WIDTH), F32),
                   jax.ShapeDtypeStruct((bsz, nb, 1, 2 * ns), F32)],
        scratch_shapes=[pltpu.VMEM((lc, 2 * ns), F32), pltpu.VMEM((1, 2 * ns), F32)],
        compiler_params=_params("parallel", "parallel", "arbitrary"),
        name="s5_scan",
    )(proj, bd, cd, tab, d_skip.reshape(1, S5_WIDTH), h0)
    hl_re = hl[..., :ns].reshape(bsz, S5_GROUPS, S5_STATE)
    hl_im = hl[..., ns:].reshape(bsz, S5_GROUPS, S5_STATE)
    return y, hl_re, hl_im


def _glu_kernel(y_ref, w_ref, b_ref, o_ref):
    y = jax.nn.gelu(y_ref[...])
    z = jnp.dot(y.astype(BF16), w_ref[...], preferred_element_type=F32) + b_ref[...]
    o_ref[...] = y * jax.nn.sigmoid(z)


def s5_glu(y, w, b, tm):
    t, k = y.shape
    return pl.pallas_call(
        _glu_kernel, grid=(t // tm,),
        in_specs=[pl.BlockSpec((tm, k), lambda i: (i, 0)), pl.BlockSpec((k, k), lambda i: (0, 0)),
                  pl.BlockSpec((1, k), lambda i: (0, 0))],
        out_specs=pl.BlockSpec((tm, k), lambda i: (i, 0)),
        out_shape=jax.ShapeDtypeStruct((t, k), F32),
        compiler_params=_params("parallel"),
        name="s5_glu",
    )(y, w, b.reshape(1, k))


def _gmlp_kernel(zu_ref, zv_ref, g_ref, b_ref, w_ref, bs_ref, o_ref, v_ref, *, lc):
    u = jax.nn.gelu(zu_ref[...])
    v = _layer_norm(jax.nn.gelu(zv_ref[...]), g_ref[...], b_ref[...])
    v_ref[...] = v
    row = lax.broadcasted_iota(jnp.int32, (lc, lc), 0)
    col = lax.broadcasted_iota(jnp.int32, (lc, lc), 1)
    causal = row >= col
    vb = v.astype(BF16)
    bs = bs_ref[...]
    for h in range(GM_HEADS):
        cols = slice(h * GM_HEAD_DIM, (h + 1) * GM_HEAD_DIM)
        w = jnp.where(causal, w_ref[h], 0.0).astype(BF16)
        mixed = jnp.dot(w, vb[:, cols], preferred_element_type=F32) + bs[:, h:h + 1]
        o_ref[:, cols] = u[:, cols] * mixed


def chunk_gmlp(proj, ln_g, ln_b, w_s, b_s, lc):
    bsz, seq = proj.shape[0], proj.shape[1]
    w = w_s[:, :lc, :lc]
    bs = b_s[:, :lc].T
    fixed2 = lambda b, n: (0, 0)
    return pl.pallas_call(
        functools.partial(_gmlp_kernel, lc=lc),
        grid=(bsz, seq // lc),
        in_specs=[pl.BlockSpec((None, lc, GM_WIDTH), lambda b, n: (b, n, 1)),
                  pl.BlockSpec((None, lc, GM_WIDTH), lambda b, n: (b, n, 2)),
                  pl.BlockSpec((1, GM_WIDTH), fixed2), pl.BlockSpec((1, GM_WIDTH), fixed2),
                  pl.BlockSpec((GM_HEADS, lc, lc), lambda b, n: (0, 0, 0)),
                  pl.BlockSpec((lc, GM_HEADS), fixed2)],
        out_specs=[pl.BlockSpec((None, lc, GM_WIDTH), lambda b, n: (b, n, 0)),
                   pl.BlockSpec((None, lc, GM_WIDTH), lambda b, n: (b, n, 0))],
        out_shape=[jax.ShapeDtypeStruct((bsz, seq, GM_WIDTH), F32),
                   jax.ShapeDtypeStruct((bsz, seq, GM_WIDTH), F32)],
        compiler_params=_params("parallel", "parallel"),
        name="chunk_gmlp",
    )(proj, proj, ln_g.reshape(1, GM_WIDTH), ln_b.reshape(1, GM_WIDTH), w, bs)


def _logf_kernel(x_ref, w_ref, b_ref, o_ref):
    z = jnp.dot(x_ref[...].astype(BF16), w_ref[...].astype(BF16), preferred_element_type=F32)
    o_ref[...] = jax.nn.log_sigmoid(z + b_ref[...])


def forget_log_gate(x, w_f, b_f, tm):
    t, k = x.shape
    return pl.pallas_call(
        _logf_kernel, grid=(t // tm,),
        in_specs=[pl.BlockSpec((tm, k), lambda i: (i, 0)), pl.BlockSpec((k, N_HEADS), lambda i: (0, 0)),
                  pl.BlockSpec((1, N_HEADS), lambda i: (0, 0))],
        out_specs=pl.BlockSpec((tm, N_HEADS), lambda i: (i, 0)),
        out_shape=jax.ShapeDtypeStruct((t, N_HEADS), F32),
        compiler_params=_params("parallel"),
        name="forget_log_gate",
    )(x, w_f, b_f.reshape(1, N_HEADS))


def _fox_prompt_kernel(q_ref, k_ref, v_ref, fq_ref, fk_ref, o_ref, m_sc, l_sc, acc_sc, *, tq):
    hg = pl.program_id(1)
    qi = pl.program_id(2)
    ki = pl.program_id(3)

    @pl.when(ki == 0)
    def _():
        m_sc[...] = jnp.full_like(m_sc, NEG_INF)
        l_sc[...] = jnp.zeros_like(l_sc)
        acc_sc[...] = jnp.zeros_like(acc_sc)

    def block(on_diagonal):
        for hl in range(FOX_HEADS_PER_STEP):
            cols = slice(hl * HEAD_DIM, (hl + 1) * HEAD_DIM)
            q = (q_ref[:, cols] * ATTN_SCALE).astype(BF16)
            s = lax.dot_general(q, k_ref[:, cols].astype(BF16), (((1,), (1,)), ((), ())),
                                preferred_element_type=F32)
            head = lax.broadcasted_iota(jnp.int32, (tq, N_HEADS), 1) == hg * FOX_HEADS_PER_STEP + hl
            fq = jnp.sum(jnp.where(head, fq_ref[...], 0.0), axis=-1, keepdims=True)
            s = s + fq - fk_ref[hl]
            if on_diagonal:
                causal = (lax.broadcasted_iota(jnp.int32, (tq, tq), 0)
                          >= lax.broadcasted_iota(jnp.int32, (tq, tq), 1))
                s = jnp.where(causal, s, NEG_INF)
            m_old = m_sc[hl]
            m_new = jnp.maximum(m_old, jnp.max(s, axis=-1, keepdims=True))
            alpha = jnp.exp(m_old - m_new)
            p = jnp.exp(s - m_new)
            l_sc[hl] = alpha * l_sc[hl] + jnp.sum(p, axis=-1, keepdims=True)
            acc_sc[:, cols] = alpha * acc_sc[:, cols] + jnp.dot(p.astype(BF16), v_ref[:, cols].astype(BF16),
                                                                 preferred_element_type=F32)
            m_sc[hl] = m_new

    @pl.when(ki < qi)
    def _():
        block(False)

    @pl.when(ki == qi)
    def _():
        block(True)

    @pl.when(ki == pl.num_programs(3) - 1)
    def _():
        for hl in range(FOX_HEADS_PER_STEP):
            cols = slice(hl * HEAD_DIM, (hl + 1) * HEAD_DIM)
            o_ref[:, cols] = acc_sc[:, cols] / l_sc[hl]


def fox_prompt(q, k, v, f_cum, tq):
    bsz, seq = q.shape[0], q.shape[1]
    nq = seq // tq
    hps = FOX_HEADS_PER_STEP
    width = hps * HEAD_DIM
    f_rows = f_cum.transpose(0, 2, 1).reshape(bsz, N_HEADS, 1, seq)
    kv_map = lambda b, h, i, j: (b, jnp.minimum(j, i), h)
    return pl.pallas_call(
        functools.partial(_fox_prompt_kernel, tq=tq),
        grid=(bsz, N_HEADS // hps, nq, nq),
        in_specs=[pl.BlockSpec((None, tq, width), lambda b, h, i, j: (b, i, h)),
                  pl.BlockSpec((None, tq, width), kv_map),
                  pl.BlockSpec((None, tq, width), kv_map),
                  pl.BlockSpec((None, tq, N_HEADS), lambda b, h, i, j: (b, i, 0)),
                  pl.BlockSpec((None, hps, 1, tq), lambda b, h, i, j: (b, h, 0, jnp.minimum(j, i)))],
        out_specs=pl.BlockSpec((None, tq, width), lambda b, h, i, j: (b, i, h)),
        out_shape=jax.ShapeDtypeStruct((bsz, seq, D_MODEL), F32),
        scratch_shapes=[pltpu.VMEM((hps, tq, 1), F32), pltpu.VMEM((hps, tq, 1), F32),
                        pltpu.VMEM((tq, width), F32)],
        compiler_params=_params("parallel", "parallel", "parallel", "arbitrary"),
        name="fox_prompt",
    )(q, k, v, f_cum, f_rows)


def _fox_sample_kernel(pt_ref, q_ref, kn_ref, vn_ref, gc_ref, gr_ref, *rest, n_new, pps):
    kp_refs, vp_refs, lp_refs = rest[:pps], rest[pps:2 * pps], rest[2 * pps:3 * pps]
    o_ref, m_sc, l_sc, acc_sc, carry_sc = rest[3 * pps:]
    step = pl.program_id(1)
    nt = (((1,), (1,)), ((), ()))
    head_rows = lambda h: slice(h * n_new, (h + 1) * n_new)
    head_cols = lambda h: slice(h * HEAD_DIM, (h + 1) * HEAD_DIM)
    q_heads = [(q_ref[:, head_cols(h)] * ATTN_SCALE).astype(BF16) for h in range(N_HEADS)]

    def online_update(s, values):
        m_old = m_sc[...]
        m_new = jnp.maximum(m_old, jnp.max(s, axis=-1, keepdims=True))
        alpha = jnp.exp(m_old - m_new)
        p = jnp.exp(s - m_new)
        l_sc[...] = alpha * l_sc[...] + jnp.sum(p, axis=-1, keepdims=True)
        pv = [jnp.dot(p[head_rows(h), :].astype(BF16), values(h), preferred_element_type=F32)
              for h in range(N_HEADS)]
        acc_sc[...] = alpha * acc_sc[...] + jnp.concatenate(pv, axis=0)
        m_sc[...] = m_new

    @pl.when(step == 0)
    def _():
        m_sc[...] = jnp.full_like(m_sc, NEG_INF)
        l_sc[...] = jnp.zeros_like(l_sc)
        acc_sc[...] = jnp.zeros_like(acc_sc)
        carry_sc[...] = jnp.zeros_like(carry_sc)
        s = jnp.concatenate([lax.dot_general(q_heads[h], kn_ref[:, head_cols(h)].astype(BF16), nt,
                                             preferred_element_type=F32) for h in range(N_HEADS)], axis=0)
        s = s + gc_ref[...] - gr_ref[...]
        rows = N_HEADS * n_new
        causal = (lax.broadcasted_iota(jnp.int32, (rows, n_new), 0) % n_new
                  >= lax.broadcasted_iota(jnp.int32, (rows, n_new), 1))
        online_update(jnp.where(causal, s, NEG_INF), lambda h: vn_ref[:, head_cols(h)].astype(BF16))

    @pl.when(step > 0)
    def _():
        later = (lax.broadcasted_iota(jnp.int32, (PAGE_SIZE, PAGE_SIZE), 0)
                 > lax.broadcasted_iota(jnp.int32, (PAGE_SIZE, PAGE_SIZE), 1)).astype(F32)
        carry = carry_sc[...]
        page_scores = []
        for pg in range(pps):
            lp = lp_refs[pg][...]
            suffix = jnp.dot(lp, later, precision=HIGHEST, preferred_element_type=F32) + carry
            carry = carry + jnp.sum(lp, axis=-1, keepdims=True)
            qk = [lax.dot_general(q_heads[h], kp_refs[pg][pl.ds(h, PAGE_SIZE, stride=N_HEADS), :].astype(BF16),
                                  nt, preferred_element_type=F32)
                  + jnp.broadcast_to(suffix[h:h + 1, :], (n_new, PAGE_SIZE)) for h in range(N_HEADS)]
            page_scores.append(jnp.concatenate(qk, axis=0))
        carry_sc[...] = carry
        s = jnp.concatenate(page_scores, axis=1) + gc_ref[...]

        def values(h):
            return jnp.concatenate([vp_refs[pg][pl.ds(h, PAGE_SIZE, stride=N_HEADS), :].astype(BF16)
                                    for pg in range(pps)], axis=0)

        online_update(s, values)

    @pl.when(step == pl.num_programs(1) - 1)
    def _():
        out = acc_sc[...] / l_sc[...]
        for h in range(N_HEADS):
            o_ref[:, head_cols(h)] = out[head_rows(h), :]


def fox_sample(q, k, v, g_new, cache_k, cache_v, cache_logf_t, page_table, layer):
    dbs, n_new = q.shape[0], q.shape[1]
    n_pages = page_table.shape[1]
    pps = math.gcd(n_pages, FOX_PAGES_PER_STEP)
    rows = N_HEADS * n_new
    g_t = g_new.transpose(0, 2, 1)
    g_col = g_t.reshape(dbs, rows, 1)
    g_row = jnp.repeat(g_t, n_new, axis=1)

    def page_map(pg):
        def index(b, s, pt):
            first = jnp.maximum(s - 1, 0) * pps
            return (layer, pt[b, n_pages - 1 - (first + pg)], 0, 0)
        return index

    seq_map = lambda b, s, pt: (b, 0, 0)
    page_specs = lambda shape: [pl.BlockSpec((None, None) + shape, page_map(pg)) for pg in range(pps)]
    grid_spec = pltpu.PrefetchScalarGridSpec(
        num_scalar_prefetch=1,
        grid=(dbs, n_pages // pps + 1),
        in_specs=([pl.BlockSpec((None, n_new, D_MODEL), seq_map)] * 3
                  + [pl.BlockSpec((None, rows, 1), seq_map), pl.BlockSpec((None, rows, n_new), seq_map)]
                  + page_specs((PAGE_SIZE * N_HEADS, HEAD_DIM)) * 2
                  + page_specs((N_HEADS, PAGE_SIZE))),
        out_specs=pl.BlockSpec((None, n_new, D_MODEL), seq_map),
        scratch_shapes=[pltpu.VMEM((rows, 1), F32), pltpu.VMEM((rows, 1), F32),
                        pltpu.VMEM((rows, HEAD_DIM), F32), pltpu.VMEM((N_HEADS, 1), F32)],
    )
    return pl.pallas_call(
        functools.partial(_fox_sample_kernel, n_new=n_new, pps=pps),
        grid_spec=grid_spec,
        out_shape=jax.ShapeDtypeStruct((dbs, n_new, D_MODEL), F32),
        compiler_params=_params("parallel", "arbitrary"),
        name="fox_sample",
    )(page_table, q, k, v, g_col, g_row, *([cache_k] * pps), *([cache_v] * pps), *([cache_logf_t] * pps))


def _take_top(s, count):
    return _take_top_ranked(s, count, with_rank=False)[0]


def _take_top_ranked(s, count, with_rank=True):
    tops = []
    rank = jnp.full_like(s, float(count)) if with_rank else None
    for i in range(count):
        m = jnp.max(s, axis=0, keepdims=True)
        tops.append(m)
        if with_rank or i + 1 < count:
            hit = s == m
            if with_rank:
                rank = jnp.where(hit, float(i), rank)
            if i + 1 < count:
                s = jnp.where(hit, NEG_INF, s)
    return tops, rank


def _candidate_sums(top1, top2):
    limit = PEER_TOPK + 1
    t2_head = jnp.concatenate(top2[:SUBLANES], axis=0)
    row = lax.broadcasted_iota(jnp.int32, t2_head.shape, 0)
    pieces = [top1[0] + jnp.concatenate(top2, axis=0)]
    for i in range(1, SUBLANES):
        pieces.append(jnp.where(row < limit // (i + 1), top1[i] + t2_head, NEG_INF))
    assert limit // (SUBLANES + 1) == 1
    pieces.append(jnp.concatenate(top1[SUBLANES:], axis=0) + top2[0])
    return jnp.concatenate(pieces, axis=0)


def _peer_route_kernel(q_ref, keys_ref, r2_ref, e2_ref, n2_ref, e1_ref, *, n_sub):
    nt = (((1,), (1,)), ((), ()))
    for sub in range(n_sub):
        tok = slice(sub * LANES, (sub + 1) * LANES) if n_sub > 1 else slice(None)
        q = q_ref[tok, :].astype(BF16)
        s1 = lax.dot_general(keys_ref[0].astype(BF16), q[:, :PEER_KEYS], nt, preferred_element_type=F32)
        s2 = lax.dot_general(keys_ref[1].astype(BF16), q[:, PEER_KEYS:], nt, preferred_element_type=F32)
        top1 = _take_top(s1, PEER_TOPK)
        top2, r2 = _take_top_ranked(s2, PEER_TOPK)
        best = _take_top(_candidate_sums(top1, top2), PEER_TOPK + 1)
        z = jnp.ones_like(best[0])
        for m in best[1:PEER_TOPK]:
            z += jnp.exp(m - best[0])
        cut = 0.5 * (best[PEER_TOPK - 1] + best[PEER_TOPK])
        n2 = jnp.zeros_like(s1)
        for t2 in top2:
            n2 += jnp.where(s1 + t2 > cut, 1.0, 0.0)
        r2_ref[:, tok] = r2.astype(BF16)
        e2_ref[:, tok] = jnp.where(s2 >= top2[PEER_TOPK - 1], jnp.exp(s2 - top2[0]), 0.0).astype(BF16)
        n2_ref[:, tok] = n2
        e1_ref[:, tok] = jnp.where(s1 >= top1[PEER_TOPK - 1], jnp.exp(s1 - top1[0]) / z, 0.0)


def peer_route(q, sub_keys, tm):
    t = q.shape[0]
    n_sub = max(tm // LANES, 1)
    out = lambda dtype: jax.ShapeDtypeStruct((PEER_HEADS, PEER_KEYS, t), dtype)
    ospec = pl.BlockSpec((None, PEER_KEYS, tm), lambda i, h: (h, 0, i))
    return pl.pallas_call(
        functools.partial(_peer_route_kernel, n_sub=n_sub),
        grid=(t // tm, PEER_HEADS),
        in_specs=[pl.BlockSpec((tm, 2 * PEER_KEYS), lambda i, h: (i, h)),
                  pl.BlockSpec((None, 2, PEER_KEYS, PEER_KEYS), lambda i, h: (h, 0, 0, 0))],
        out_specs=[ospec, ospec, ospec, ospec],
        out_shape=[out(BF16), out(BF16), out(F32), out(F32)],
        compiler_params=_params("parallel", "parallel"),
        name="peer_route",
    )(q, sub_keys)


def _peer_dense_kernel(x_ref, u_ref, vt_ref, r2_ref, e2_ref, n2_ref, e1_ref, res_ref, ln_g_ref, ln_b_ref,
                       o_ref, ob_ref, acc_sc, z_sc, g_sc, *, te, tm):
    k = pl.program_id(1)
    a_per_tile = te // PEER_KEYS
    rows_per_chunk = 2 * SUBLANES
    lanes_per_chunk = min(tm, 2 * LANES)
    n_chunks = PEER_KEYS // rows_per_chunk
    chunk_shape = (rows_per_chunk, lanes_per_chunk)

    @pl.when(k == 0)
    def _():
        acc_sc[...] = jnp.zeros_like(acc_sc)

    z_sc[...] = lax.dot_general(u_ref[...], x_ref[...], (((1,), (1,)), ((), ())), preferred_element_type=F32)
    for al in range(a_per_tile):
        for t0 in range(0, tm, lanes_per_chunk):
            tok = slice(t0, t0 + lanes_per_chunk)
            gates = [jnp.zeros(chunk_shape, BF16) for _ in range(n_chunks)]
            for h in range(PEER_HEADS):
                n2 = jnp.broadcast_to(n2_ref[h, al:al + 1, tok], chunk_shape).astype(BF16)
                e1 = jnp.broadcast_to(e1_ref[h, al:al + 1, tok], chunk_shape).astype(BF16)
                for c in range(n_chunks):
                    rows = slice(c * rows_per_chunk, (c + 1) * rows_per_chunk)
                    keep = r2_ref[h, rows, tok] < n2
                    gates[c] += jnp.where(keep, e2_ref[h, rows, tok] * e1, jnp.zeros(chunk_shape, BF16))
            for c in range(n_chunks):
                zrows = slice(al * PEER_KEYS + c * rows_per_chunk, al * PEER_KEYS + (c + 1) * rows_per_chunk)
                g_sc[zrows, tok] = (gates[c].astype(F32) * jax.nn.gelu(z_sc[zrows, tok])).astype(BF16)
    acc_sc[...] += jnp.dot(vt_ref[...], g_sc[...], preferred_element_type=F32)

    @pl.when(k == pl.num_programs(1) - 1)
    def _():
        _store_norm(DN_ALPHA * res_ref[...] + acc_sc[...].T, ln_g_ref, ln_b_ref, o_ref, ob_ref)


def peer_dense(x, x_bf, u_bf, vt_bf, routes, ln_g, ln_b, tm, te):
    t, d = x_bf.shape
    n_tiles = vt_bf.shape[0]
    assert vt_bf.shape == (n_tiles, d, te)
    r2, e2, n2, e1 = routes
    a_per_tile = te // PEER_KEYS
    n2, e1 = (r.reshape(PEER_HEADS, n_tiles, a_per_tile, t) for r in (n2, e1))
    bspec = pl.BlockSpec((PEER_HEADS, PEER_KEYS, tm), lambda i, k: (0, 0, i))
    aspec = pl.BlockSpec((PEER_HEADS, None, a_per_tile, tm), lambda i, k: (0, k, 0, i))
    return pl.pallas_call(
        functools.partial(_peer_dense_kernel, te=te, tm=tm),
        grid=(t // tm, n_tiles),
        in_specs=[pl.BlockSpec((tm, d), lambda i, k: (i, 0)),
                  pl.BlockSpec((te, d), lambda i, k: (k, 0)),
                  pl.BlockSpec((None, d, te), lambda i, k: (k, 0, 0)),
                  bspec, bspec, aspec, aspec,
                  pl.BlockSpec((tm, d), lambda i, k: (i, 0)),
                  pl.BlockSpec((1, d), lambda i, k: (0, 0)), pl.BlockSpec((1, d), lambda i, k: (0, 0))],
        out_specs=[pl.BlockSpec((tm, d), lambda i, k: (i, 0)), pl.BlockSpec((tm, d), lambda i, k: (i, 0))],
        out_shape=[jax.ShapeDtypeStruct((t, d), F32), jax.ShapeDtypeStruct((t, d), BF16)],
        scratch_shapes=[pltpu.VMEM((d, tm), F32), pltpu.VMEM((te, tm), F32), pltpu.VMEM((te, tm), BF16)],
        compiler_params=_params("parallel", "arbitrary"),
        name="peer_dense",
    )(x_bf, u_bf, vt_bf, r2, e2, n2, e1, x, ln_g, ln_b)


def peer_block(x, x_bf, wq_bf, sub_keys, u_bf, vt_bf, ln_g, ln_b, tm, tm_route):
    q = matmul(x_bf, wq_bf, 0, wq_bf.shape[1], tm)
    routes = peer_route(q, sub_keys, tm_route)
    return peer_dense(x, x_bf, u_bf, vt_bf, routes, ln_g, ln_b, tm, PEER_EXPERT_TILE)


def _row_tile(t):
    tm = min(t, TOKEN_TILE)
    assert t % tm == 0
    return tm


def kernel(x_prompt, x_sample, cache_k, cache_v, cache_logf, page_table, state_s5_re, state_s5_im, w_in_even, s5_lambda_re, s5_lambda_im, s5_log_dt, s5_b_re, s5_b_im, s5_c_re, s5_c_im, s5_d, s5_w_glu, s5_b_glu, gm_ln_g, gm_ln_b, gm_w_s, gm_b_s, w_out_even, w_in_odd, b_f, w_out_odd, ln1_g, ln1_b, ln2_g, ln2_b, peer_w_q, peer_sub_keys, peer_u, peer_v):
    bsz, seq, d = x_prompt.shape
    dbs, dseq, _ = x_sample.shape
    n_odd, n_phys = cache_k.shape[0], cache_k.shape[1]
    tp, ts = bsz * seq, dbs * dseq
    xp = x_prompt.reshape(tp, d)
    xs = x_sample.reshape(ts, d)
    xp_bf, xs_bf = xp.astype(BF16), xs.astype(BF16)
    tm_p = _row_tile(tp)
    tm_s = _row_tile(ts)
    ck = cache_k.reshape(n_odd, n_phys, PAGE_SIZE * N_HEADS, HEAD_DIM)
    cv = cache_v.reshape(n_odd, n_phys, PAGE_SIZE * N_HEADS, HEAD_DIM)
    clf = cache_logf.transpose(0, 1, 3, 2)
    zero_state = jnp.zeros((bsz, S5_GROUPS, S5_STATE), F32)
    peer_rep = max(LANES // ts, 1)

    outs = {k: [] for k in ("kp", "vp", "lp", "ks", "vs", "ls", "s5p_re", "s5p_im", "s5s_re", "s5s_im", "gmv")}
    for li in range(DEPTH):
        j = li // 2
        g1, b1 = ln1_g[li].reshape(1, d), ln1_b[li].reshape(1, d)
        g2, b2 = ln2_g[li].reshape(1, d), ln2_b[li].reshape(1, d)
        if li % 2 == 0:
            tables = _s5_tables(s5_lambda_re[j], s5_lambda_im[j], s5_log_dt[j], s5_b_re[j], s5_b_im[j],
                                s5_c_re[j], s5_c_im[j])
            n_in = w_in_even.shape[2]
            w_in, w_glu, w_out = (w.astype(BF16) for w in (w_in_even[j], s5_w_glu[j], w_out_even[j]))

            def even(x, x_bf, nb, nl, h0_re, h0_im, tm):
                proj = matmul(x_bf, w_in, 0, n_in, tm).reshape(nb, nl, n_in)
                y, hl_re, hl_im = s5_scan(proj, h0_re, h0_im, tables, s5_d[j], min(nl, S5_CHUNK))
                a_out = s5_glu(y.reshape(nb * nl, S5_WIDTH), w_glu, s5_b_glu[j], tm)
                g_out, v_rows = chunk_gmlp(proj, gm_ln_g[j], gm_ln_b[j], gm_w_s[j], gm_b_s[j], min(nl, GM_CHUNK))
                x, x_bf = out_proj_dnln(a_out, g_out.reshape(nb * nl, GM_WIDTH), w_out, x, g1, b1, tm)
                return x, x_bf, hl_re, hl_im, v_rows

            xp, xp_bf, hp_re, hp_im, _ = even(xp, xp_bf, bsz, seq, zero_state, zero_state, tm_p)
            xs, xs_bf, hs_re, hs_im, v_rows = even(xs, xs_bf, dbs, dseq, state_s5_re[j].astype(F32),
                                                   state_s5_im[j].astype(F32), tm_s)
            outs["s5p_re"].append(hp_re)
            outs["s5p_im"].append(hp_im)
            outs["s5s_re"].append(hs_re)
            outs["s5s_im"].append(hs_im)
            outs["gmv"].append(v_rows)
        else:
            w_f = w_in_odd[j][:, 3 * d:]
            w_qkv, w_out = w_in_odd[j].astype(BF16), w_out_odd[j].astype(BF16)

            def project(x_bf, tm):
                q, k, v = (matmul(x_bf, w_qkv, part, d, tm) for part in range(3))
                return q, k, v, forget_log_gate(x_bf, w_f, b_f[j], tm)

            qp, kp, vp, lfp = project(xp_bf, tm_p)
            f_cum = jnp.cumsum(lfp.reshape(bsz, seq, N_HEADS), axis=1)
            ap = fox_prompt(qp.reshape(bsz, seq, d), kp.reshape(bsz, seq, d), vp.reshape(bsz, seq, d), f_cum,
                            min(seq, FOX_BLOCK))
            xp, xp_bf = out_proj_dnln(ap.reshape(tp, d), None, w_out, xp, g1, b1, tm_p)
            qs, kss, vss, lfs = project(xs_bf, tm_s)
            g_new = jnp.cumsum(lfs.reshape(dbs, dseq, N_HEADS), axis=1)
            a_s = fox_sample(qs.reshape(dbs, dseq, d), kss.reshape(dbs, dseq, d), vss.reshape(dbs, dseq, d),
                             g_new, ck, cv, clf, page_table, j)
            xs, xs_bf = out_proj_dnln(a_s.reshape(ts, d), None, w_out, xs, g1, b1, tm_s)
            outs["kp"].append(kp.reshape(bsz, seq, N_HEADS, HEAD_DIM))
            outs["vp"].append(vp.reshape(bsz, seq, N_HEADS, HEAD_DIM))
            outs["lp"].append(lfp.reshape(bsz, seq, N_HEADS))
            outs["ks"].append(kss.reshape(dbs, dseq, N_HEADS, HEAD_DIM))
            outs["vs"].append(vss.reshape(dbs, dseq, N_HEADS, HEAD_DIM))
            outs["ls"].append(lfs.reshape(dbs, dseq, N_HEADS))
        u_bf = peer_u[li].astype(BF16)
        vt_bf = peer_v[li].reshape(-1, PEER_EXPERT_TILE, d).transpose(0, 2, 1).astype(BF16)
        wq_bf = peer_w_q[li].astype(BF16)
        xp, xp_bf = peer_block(xp, xp_bf, wq_bf, peer_sub_keys[li], u_bf, vt_bf, g2, b2, tm_p,
                               min(tm_p, PEER_ROUTE_TILE))
        t_rep = peer_rep * ts
        xs, xs_bf = (r[:ts] for r in peer_block(jnp.tile(xs, (peer_rep, 1)), jnp.tile(xs_bf, (peer_rep, 1)), wq_bf,
                                                peer_sub_keys[li], u_bf, vt_bf, g2, b2, t_rep, t_rep))

    stack = lambda name, dtype: jnp.stack(outs[name]).astype(dtype)
    return (xp.reshape(bsz, seq, d), xs.reshape(dbs, dseq, d),
            stack("kp", cache_k.dtype), stack("vp", cache_v.dtype), stack("lp", cache_logf.dtype),
            stack("ks", cache_k.dtype), stack("vs", cache_v.dtype), stack("ls", cache_logf.dtype),
            stack("s5p_re", state_s5_re.dtype), stack("s5p_im", state_s5_im.dtype),
            stack("s5s_re", state_s5_re.dtype), stack("s5s_im", state_s5_im.dtype),
            jnp.stack(outs["gmv"]))
```

```python
import functools
import math

import jax
import jax.numpy as jnp
from jax import lax
from jax.experimental import pallas as pl
from jax.experimental.pallas import tpu as pltpu

F32 = jnp.float32
BF16 = jnp.bfloat16
HIGHEST = lax.Precision.HIGHEST

D_MODEL = 2048
DEPTH = 4
PAGE_SIZE = 128
S5_WIDTH = 1024
S5_GROUP = 16
S5_GROUPS = 64
S5_STATE = 64
S5_GROUP_BLOCK = 8
S5_BLOCKS = S5_GROUPS // S5_GROUP_BLOCK
S5_BLOCK_STATE = S5_GROUP_BLOCK * S5_STATE
GM_WIDTH = 1024
GM_CHUNK = 128
GM_HEADS = 8
GM_HEAD_DIM = 128
N_HEADS = 16
HEAD_DIM = 128
ATTN_SCALE = HEAD_DIM ** -0.5
TOKEN_TILE = 512
S5_CHUNK = 512
PEER_EXPERT_TILE = 512
PEER_ROUTE_TILE = 512
FOX_BLOCK = 512
FOX_HEADS_PER_STEP = 4
FOX_PAGES_PER_STEP = 4
PEER_HEADS = 8
PEER_KEYS = 128
PEER_TOPK = 16
DN_ALPHA = (2.0 * DEPTH) ** 0.25
LN_EPS = 1e-5
SUBLANES = 8
LANES = 128
VMEM_LIMIT = 56 * 1024 * 1024
NEG_INF = float("-inf")


def _params(*sem):
    return pltpu.CompilerParams(dimension_semantics=sem, vmem_limit_bytes=VMEM_LIMIT)


def _gelu_tanh(x):
    c0 = math.sqrt(2.0 / math.pi)
    return (0.5 * x) * (1.0 + jnp.tanh(x * (c0 + (c0 * 0.044715) * (x * x))))


def _layer_norm(y, g, b):
    mu = jnp.mean(y, axis=-1, keepdims=True)
    yc = y - mu
    var = jnp.mean(yc * yc, axis=-1, keepdims=True)
    return yc * lax.rsqrt(var + LN_EPS) * g + b


def _mm_kernel(x_ref, w_ref, o_ref):
    o_ref[...] = jnp.dot(x_ref[...].astype(BF16), w_ref[...], preferred_element_type=F32)


def matmul(x, w_bf, col_block, n_out, tm):
    t, k = x.shape
    assert t % tm == 0
    return pl.pallas_call(
        _mm_kernel,
        grid=(t // tm,),
        in_specs=[pl.BlockSpec((tm, k), lambda i: (i, 0)),
                  pl.BlockSpec((k, n_out), lambda i: (0, col_block))],
        out_specs=pl.BlockSpec((tm, n_out), lambda i: (i, 0)),
        out_shape=jax.ShapeDtypeStruct((t, n_out), F32),
        compiler_params=_params("parallel"),
        name="matmul",
    )(x, w_bf)


def _store_norm(y, g_ref, beta_ref, o_ref, ob_ref):
    out = _layer_norm(y, g_ref[...], beta_ref[...])
    o_ref[...] = out
    ob_ref[...] = out.astype(BF16)


def _mm1_dnln_kernel(a_ref, w_ref, x_ref, g_ref, beta_ref, o_ref, ob_ref):
    m = jnp.dot(a_ref[...].astype(BF16), w_ref[...], preferred_element_type=F32)
    _store_norm(DN_ALPHA * x_ref[...] + m, g_ref, beta_ref, o_ref, ob_ref)


def _mm2_dnln_kernel(a_ref, b_ref, wa_ref, wb_ref, x_ref, g_ref, beta_ref, o_ref, ob_ref):
    m = jnp.dot(a_ref[...].astype(BF16), wa_ref[...], preferred_element_type=F32)
    m += jnp.dot(b_ref[...].astype(BF16), wb_ref[...], preferred_element_type=F32)
    _store_norm(DN_ALPHA * x_ref[...] + m, g_ref, beta_ref, o_ref, ob_ref)


def _norm_outputs(t, d, tm):
    row = lambda i: (i, 0)
    return ([pl.BlockSpec((tm, d), row), pl.BlockSpec((tm, d), row)],
            [jax.ShapeDtypeStruct((t, d), F32), jax.ShapeDtypeStruct((t, d), BF16)])


def out_proj_dnln(a, b, w_bf, x, g, beta, tm):
    t, ka = a.shape
    d = w_bf.shape[1]
    row = lambda i: (i, 0)
    fixed = lambda i: (0, 0)
    tail = [pl.BlockSpec((tm, d), row), pl.BlockSpec((1, d), fixed), pl.BlockSpec((1, d), fixed)]
    if b is None:
        kern = _mm1_dnln_kernel
        ins = [a, w_bf, x, g, beta]
        specs = [pl.BlockSpec((tm, ka), row), pl.BlockSpec((ka, d), fixed)] + tail
    else:
        assert b.shape[1] == ka
        kern = _mm2_dnln_kernel
        ins = [a, b, w_bf, w_bf, x, g, beta]
        specs = [pl.BlockSpec((tm, ka), row), pl.BlockSpec((tm, ka), row),
                 pl.BlockSpec((ka, d), fixed), pl.BlockSpec((ka, d), lambda i: (1, 0))] + tail
    out_specs, out_shape = _norm_outputs(t, d, tm)
    return pl.pallas_call(
        kern, grid=(t // tm,), in_specs=specs, out_specs=out_specs, out_shape=out_shape,
        compiler_params=_params("parallel"),
        name="out_proj_dnln",
    )(*ins)


def _split_bf16(x):
    hi = lax.bitcast_convert_type(lax.bitcast_convert_type(x, jnp.uint32) & jnp.uint32(0xFFFF0000), F32)
    return hi.astype(BF16), (x - hi).astype(BF16)


def _dot_split(a, b_ref):
    a_hi, a_lo = _split_bf16(a)
    dot = functools.partial(jnp.dot, preferred_element_type=F32)
    return dot(a_hi, b_ref[0]) + (dot(a_lo, b_ref[0]) + dot(a_hi, b_ref[1]))


def _s5_kernel(u_ref, bd_ref, cd_ref, tab_ref, d_ref, h0_ref, y_ref, hl_ref, hs_sc, carry_sc, *, lc):
    ns = S5_BLOCK_STATE

    @pl.when(pl.program_id(2) == 0)
    def _():
        carry_sc[...] = h0_ref[...]

    u = u_ref[...]
    hs_sc[...] = _dot_split(u, bd_ref)

    def row_group(r, carry):
        cr, ci = carry
        rows = pl.ds(pl.multiple_of(r * SUBLANES, SUBLANES), SUBLANES)
        xr = hs_sc[rows, :ns]
        xi = hs_sc[rows, ns:]
        for k in range(3):
            ar = tab_ref[2 * k]
            ai = tab_ref[2 * k + 1]
            sr = pltpu.roll(xr, 1 << k, 0)
            si = pltpu.roll(xi, 1 << k, 0)
            xr, xi = xr + (ar * sr - ai * si), xi + (ar * si + ai * sr)
        pr = tab_ref[6]
        pi = tab_ref[7]
        xr, xi = xr + (pr * cr - pi * ci), xi + (pr * ci + pi * cr)
        hs_sc[rows, :ns] = xr
        hs_sc[rows, ns:] = xi
        return xr[SUBLANES - 1:, :], xi[SUBLANES - 1:, :]

    cr, ci = lax.fori_loop(0, lc // SUBLANES, row_group, (carry_sc[:, :ns], carry_sc[:, ns:]))
    carry_sc[:, :ns] = cr
    carry_sc[:, ns:] = ci
    hl_ref[...] = carry_sc[...]
    y_ref[...] = _dot_split(hs_sc[...], cd_ref) + d_ref[...] * u


def _s5_tables(lam_re, lam_im, log_dt, b_re, b_im, c_re, c_im):
    dt = jnp.exp(log_dt.astype(F32))[:, None]
    mag = jnp.exp(lam_re * dt)
    lbr = mag * jnp.cos(lam_im * dt)
    lbi = mag * jnp.sin(lam_im * dt)
    den = lam_re * lam_re + lam_im * lam_im
    qr = ((lbr - 1.0) * lam_re + lbi * lam_im) / den
    qi = (lbi * lam_re - (lbr - 1.0) * lam_im) / den
    bbr = qr[..., None] * b_re - qi[..., None] * b_im
    bbi = qr[..., None] * b_im + qi[..., None] * b_re
    nb, gb = S5_BLOCKS, S5_GROUP_BLOCK
    eye = jnp.eye(gb, dtype=F32)

    def bdiag(m):
        m = m.reshape(nb, gb, S5_STATE, S5_GROUP)
        return jnp.einsum('ngpc,gh->ngchp', m, eye).reshape(nb, gb * S5_GROUP, gb * S5_STATE)

    def cdiag(m):
        m = m.reshape(nb, gb, S5_GROUP, S5_STATE)
        return jnp.einsum('ngcp,gh->ngphc', m, eye).reshape(nb, gb * S5_STATE, gb * S5_GROUP)

    bd = jnp.stack(_split_bf16(jnp.concatenate([bdiag(bbr), bdiag(bbi)], axis=-1)), axis=1)
    cd = jnp.stack(_split_bf16(jnp.concatenate([cdiag(c_re), cdiag(-c_im)], axis=1)), axis=1)

    def cmul(a, b):
        return a[0] * b[0] - a[1] * b[1], a[0] * b[1] + a[1] * b[0]

    powers = [(lbr, lbi)]
    for _ in range(SUBLANES - 1):
        powers.append(cmul(powers[-1], (lbr, lbi)))
    row = jnp.arange(SUBLANES)[:, None, None]
    tabs = []
    for k in range(3):
        sh = 1 << k
        for part in powers[sh - 1]:
            tabs.append(jnp.where(row >= sh, part[None], 0.0))
    tabs.append(jnp.stack([p[0] for p in powers]))
    tabs.append(jnp.stack([p[1] for p in powers]))
    tab = jnp.stack(tabs)
    tab = tab.reshape(8, SUBLANES, nb, S5_BLOCK_STATE).transpose(2, 0, 1, 3)
    return bd, cd, tab


def s5_scan(proj, h0_re, h0_im, tables, d_skip, lc):
    bd, cd, tab = tables
    bsz, seq = proj.shape[0], proj.shape[1]
    nb, ns = S5_BLOCKS, S5_BLOCK_STATE
    h0 = jnp.concatenate([h0_re.reshape(bsz, nb, 1, ns), h0_im.reshape(bsz, nb, 1, ns)], axis=-1)
    y, hl = pl.pallas_call(
        functools.partial(_s5_kernel, lc=lc),
        grid=(nb, bsz, seq // lc),
        in_specs=[pl.BlockSpec((None, lc, LANES), lambda g, b, l: (b, l, g)),
                  pl.BlockSpec((None, 2, LANES, 2 * ns), lambda g, b, l: (g, 0, 0, 0)),
                  pl.BlockSpec((None, 2, 2 * ns, LANES), lambda g, b, l: (g, 0, 0, 0)),
                  pl.BlockSpec((None, 8, SUBLANES, ns), lambda g, b, l: (g, 0, 0, 0)),
                  pl.BlockSpec((1, LANES), lambda g, b, l: (0, g)),
                  pl.BlockSpec((None, None, 1, 2 * ns), lambda g, b, l: (b, g, 0, 0))],
        out_specs=[pl.BlockSpec((None, lc, LANES), lambda g, b, l: (b, l, g)),
                   pl.BlockSpec((None, None, 1, 2 * ns), lambda g, b, l: (b, g, 0, 0))],
        out_shape=[jax.ShapeDtypeStruct((bsz, seq, S5_WIDTH), F32),
                   jax.ShapeDtypeStruct((bsz, nb, 1, 2 * ns), F32)],
        scratch_shapes=[pltpu.VMEM((lc, 2 * ns), F32), pltpu.VMEM((1, 2 * ns), F32)],
        compiler_params=_params("parallel", "parallel", "arbitrary"),
        name="s5_scan",
    )(proj, bd, cd, tab, d_skip.reshape(1, S5_WIDTH), h0)
    hl_re = hl[..., :ns].reshape(bsz, S5_GROUPS, S5_STATE)
    hl_im = hl[..., ns:].reshape(bsz, S5_GROUPS, S5_STATE)
    return y, hl_re, hl_im


def _glu_kernel(y_ref, w_ref, b_ref, o_ref):
    y = jax.nn.gelu(y_ref[...])
    z = jnp.dot(y.astype(BF16), w_ref[...], preferred_element_type=F32) + b_ref[...]
    o_ref[...] = y * jax.nn.sigmoid(z)


def s5_glu(y, w, b, tm):
    t, k = y.shape
    return pl.pallas_call(
        _glu_kernel, grid=(t // tm,),
        in_specs=[pl.BlockSpec((tm, k), lambda i: (i, 0)), pl.BlockSpec((k, k), lambda i: (0, 0)),
                  pl.BlockSpec((1, k), lambda i: (0, 0))],
        out_specs=pl.BlockSpec((tm, k), lambda i: (i, 0)),
        out_shape=jax.ShapeDtypeStruct((t, k), F32),
        compiler_params=_params("parallel"),
        name="s5_glu",
    )(y, w, b.reshape(1, k))


def _gmlp_kernel(zu_ref, zv_ref, g_ref, b_ref, w_ref, bs_ref, o_ref, v_ref, *, lc):
    u = jax.nn.gelu(zu_ref[...])
    v = _layer_norm(jax.nn.gelu(zv_ref[...]), g_ref[...], b_ref[...])
    v_ref[...] = v
    row = lax.broadcasted_iota(jnp.int32, (lc, lc), 0)
    col = lax.broadcasted_iota(jnp.int32, (lc, lc), 1)
    causal = row >= col
    vb = v.astype(BF16)
    bs = bs_ref[...]
    for h in range(GM_HEADS):
        cols = slice(h * GM_HEAD_DIM, (h + 1) * GM_HEAD_DIM)
        w = jnp.where(causal, w_ref[h], 0.0).astype(BF16)
        mixed = jnp.dot(w, vb[:, cols], preferred_element_type=F32) + bs[:, h:h + 1]
        o_ref[:, cols] = u[:, cols] * mixed


def chunk_gmlp(proj, ln_g, ln_b, w_s, b_s, lc):
    bsz, seq = proj.shape[0], proj.shape[1]
    w = w_s[:, :lc, :lc]
    bs = b_s[:, :lc].T
    fixed2 = lambda b, n: (0, 0)
    return pl.pallas_call(
        functools.partial(_gmlp_kernel, lc=lc),
        grid=(bsz, seq // lc),
        in_specs=[pl.BlockSpec((None, lc, GM_WIDTH), lambda b, n: (b, n, 1)),
                  pl.BlockSpec((None, lc, GM_WIDTH), lambda b, n: (b, n, 2)),
                  pl.BlockSpec((1, GM_WIDTH), fixed2), pl.BlockSpec((1, GM_WIDTH), fixed2),
                  pl.BlockSpec((GM_HEADS, lc, lc), lambda b, n: (0, 0, 0)),
                  pl.BlockSpec((lc, GM_HEADS), fixed2)],
        out_specs=[pl.BlockSpec((None, lc, GM_WIDTH), lambda b, n: (b, n, 0)),
                   pl.BlockSpec((None, lc, GM_WIDTH), lambda b, n: (b, n, 0))],
        out_shape=[jax.ShapeDtypeStruct((bsz, seq, GM_WIDTH), F32),
                   jax.ShapeDtypeStruct((bsz, seq, GM_WIDTH), F32)],
        compiler_params=_params("parallel", "parallel"),
        name="chunk_gmlp",
    )(proj, proj, ln_g.reshape(1, GM_WIDTH), ln_b.reshape(1, GM_WIDTH), w, bs)


def _logf_kernel(x_ref, w_ref, b_ref, o_ref):
    z = jnp.dot(x_ref[...].astype(BF16), w_ref[...].astype(BF16), preferred_element_type=F32)
    o_ref[...] = jax.nn.log_sigmoid(z + b_ref[...])


def forget_log_gate(x, w_f, b_f, tm):
    t, k = x.shape
    return pl.pallas_call(
        _logf_kernel, grid=(t // tm,),
        in_specs=[pl.BlockSpec((tm, k), lambda i: (i, 0)), pl.BlockSpec((k, N_HEADS), lambda i: (0, 0)),
                  pl.BlockSpec((1, N_HEADS), lambda i: (0, 0))],
        out_specs=pl.BlockSpec((tm, N_HEADS), lambda i: (i, 0)),
        out_shape=jax.ShapeDtypeStruct((t, N_HEADS), F32),
        compiler_params=_params("parallel"),
        name="forget_log_gate",
    )(x, w_f, b_f.reshape(1, N_HEADS))


def _fox_prompt_kernel(q_ref, k_ref, v_ref, fq_ref, fk_ref, o_ref, m_sc, l_sc, acc_sc, *, tq):
    hg = pl.program_id(1)
    qi = pl.program_id(2)
    ki = pl.program_id(3)

    @pl.when(ki == 0)
    def _():
        m_sc[...] = jnp.full_like(m_sc, NEG_INF)
        l_sc[...] = jnp.zeros_like(l_sc)
        acc_sc[...] = jnp.zeros_like(acc_sc)

    def block(on_diagonal):
        for hl in range(FOX_HEADS_PER_STEP):
            cols = slice(hl * HEAD_DIM, (hl + 1) * HEAD_DIM)
            q = (q_ref[:, cols] * ATTN_SCALE).astype(BF16)
            s = lax.dot_general(q, k_ref[:, cols].astype(BF16), (((1,), (1,)), ((), ())),
                                preferred_element_type=F32)
            head = lax.broadcasted_iota(jnp.int32, (tq, N_HEADS), 1) == hg * FOX_HEADS_PER_STEP + hl
            fq = jnp.sum(jnp.where(head, fq_ref[...], 0.0), axis=-1, keepdims=True)
            s = s + fq - fk_ref[hl]
            if on_diagonal:
                causal = (lax.broadcasted_iota(jnp.int32, (tq, tq), 0)
                          >= lax.broadcasted_iota(jnp.int32, (tq, tq), 1))
                s = jnp.where(causal, s, NEG_INF)
            m_old = m_sc[hl]
            m_new = jnp.maximum(m_old, jnp.max(s, axis=-1, keepdims=True))
            alpha = jnp.exp(m_old - m_new)
            p = jnp.exp(s - m_new)
            l_sc[hl] = alpha * l_sc[hl] + jnp.sum(p, axis=-1, keepdims=True)
            acc_sc[:, cols] = alpha * acc_sc[:, cols] + jnp.dot(p.astype(BF16), v_ref[:, cols].astype(BF16),
                                                                 preferred_element_type=F32)
            m_sc[hl] = m_new

    @pl.when(ki < qi)
    def _():
        block(False)

    @pl.when(ki == qi)
    def _():
        block(True)

    @pl.when(ki == pl.num_programs(3) - 1)
    def _():
        for hl in range(FOX_HEADS_PER_STEP):
            cols = slice(hl * HEAD_DIM, (hl + 1) * HEAD_DIM)
            o_ref[:, cols] = acc_sc[:, cols] / l_sc[hl]


def fox_prompt(q, k, v, f_cum, tq):
    bsz, seq = q.shape[0], q.shape[1]
    nq = seq // tq
    hps = FOX_HEADS_PER_STEP
    width = hps * HEAD_DIM
    f_rows = f_cum.transpose(0, 2, 1).reshape(bsz, N_HEADS, 1, seq)
    kv_map = lambda b, h, i, j: (b, jnp.minimum(j, i), h)
    return pl.pallas_call(
        functools.partial(_fox_prompt_kernel, tq=tq),
        grid=(bsz, N_HEADS // hps, nq, nq),
        in_specs=[pl.BlockSpec((None, tq, width), lambda b, h, i, j: (b, i, h)),
                  pl.BlockSpec((None, tq, width), kv_map),
                  pl.BlockSpec((None, tq, width), kv_map),
                  pl.BlockSpec((None, tq, N_HEADS), lambda b, h, i, j: (b, i, 0)),
                  pl.BlockSpec((None, hps, 1, tq), lambda b, h, i, j: (b, h, 0, jnp.minimum(j, i)))],
        out_specs=pl.BlockSpec((None, tq, width), lambda b, h, i, j: (b, i, h)),
        out_shape=jax.ShapeDtypeStruct((bsz, seq, D_MODEL), F32),
        scratch_shapes=[pltpu.VMEM((hps, tq, 1), F32), pltpu.VMEM((hps, tq, 1), F32),
                        pltpu.VMEM((tq, width), F32)],
        compiler_params=_params("parallel", "parallel", "parallel", "arbitrary"),
        name="fox_prompt",
    )(q, k, v, f_cum, f_rows)


def _fox_sample_kernel(pt_ref, q_ref, kn_ref, vn_ref, gc_ref, gr_ref, *rest, n_new, pps):
    kp_refs, vp_refs, lp_refs = rest[:pps], rest[pps:2 * pps], rest[2 * pps:3 * pps]
    o_ref, m_sc, l_sc, acc_sc, carry_sc = rest[3 * pps:]
    step = pl.program_id(1)
    nt = (((1,), (1,)), ((), ()))
    head_rows = lambda h: slice(h * n_new, (h + 1) * n_new)
    head_cols = lambda h: slice(h * HEAD_DIM, (h + 1) * HEAD_DIM)
    q_heads = [(q_ref[:, head_cols(h)] * ATTN_SCALE).astype(BF16) for h in range(N_HEADS)]

    def online_update(s, values):
        m_old = m_sc[...]
        m_new = jnp.maximum(m_old, jnp.max(s, axis=-1, keepdims=True))
        alpha = jnp.exp(m_old - m_new)
        p = jnp.exp(s - m_new)
        l_sc[...] = alpha * l_sc[...] + jnp.sum(p, axis=-1, keepdims=True)
        pv = [jnp.dot(p[head_rows(h), :].astype(BF16), values(h), preferred_element_type=F32)
              for h in range(N_HEADS)]
        acc_sc[...] = alpha * acc_sc[...] + jnp.concatenate(pv, axis=0)
        m_sc[...] = m_new

    @pl.when(step == 0)
    def _():
        m_sc[...] = jnp.full_like(m_sc, NEG_INF)
        l_sc[...] = jnp.zeros_like(l_sc)
        acc_sc[...] = jnp.zeros_like(acc_sc)
        carry_sc[...] = jnp.zeros_like(carry_sc)
        s = jnp.concatenate([lax.dot_general(q_heads[h], kn_ref[:, head_cols(h)].astype(BF16), nt,
                                             preferred_element_type=F32) for h in range(N_HEADS)], axis=0)
        s = s + gc_ref[...] - gr_ref[...]
        rows = N_HEADS * n_new
        causal = (lax.broadcasted_iota(jnp.int32, (rows, n_new), 0) % n_new
                  >= lax.broadcasted_iota(jnp.int32, (rows, n_new), 1))
        online_update(jnp.where(causal, s, NEG_INF), lambda h: vn_ref[:, head_cols(h)].astype(BF16))

    @pl.when(step > 0)
    def _():
        later = (lax.broadcasted_iota(jnp.int32, (PAGE_SIZE, PAGE_SIZE), 0)
                 > lax.broadcasted_iota(jnp.int32, (PAGE_SIZE, PAGE_SIZE), 1)).astype(F32)
        carry = carry_sc[...]
        page_scores = []
        for pg in range(pps):
            lp = lp_refs[pg][...]
            suffix = jnp.dot(lp, later, precision=HIGHEST, preferred_element_type=F32) + carry
            carry = carry + jnp.sum(lp, axis=-1, keepdims=True)
            qk = [lax.dot_general(q_heads[h], kp_refs[pg][pl.ds(h, PAGE_SIZE, stride=N_HEADS), :].astype(BF16),
                                  nt, preferred_element_type=F32)
                  + jnp.broadcast_to(suffix[h:h + 1, :], (n_new, PAGE_SIZE)) for h in range(N_HEADS)]
            page_scores.append(jnp.concatenate(qk, axis=0))
        carry_sc[...] = carry
        s = jnp.concatenate(page_scores, axis=1) + gc_ref[...]

        def values(h):
            return jnp.concatenate([vp_refs[pg][pl.ds(h, PAGE_SIZE, stride=N_HEADS), :].astype(BF16)
                                    for pg in range(pps)], axis=0)

        online_update(s, values)

    @pl.when(step == pl.num_programs(1) - 1)
    def _():
        out = acc_sc[...] / l_sc[...]
        for h in range(N_HEADS):
            o_ref[:, head_cols(h)] = out[head_rows(h), :]


def fox_sample(q, k, v, g_new, cache_k, cache_v, cache_logf_t, page_table, layer):
    dbs, n_new = q.shape[0], q.shape[1]
    n_pages = page_table.shape[1]
    pps = math.gcd(n_pages, FOX_PAGES_PER_STEP)
    rows = N_HEADS * n_new
    g_t = g_new.transpose(0, 2, 1)
    g_col = g_t.reshape(dbs, rows, 1)
    g_row = jnp.repeat(g_t, n_new, axis=1)

    def page_map(pg):
        def index(b, s, pt):
            first = jnp.maximum(s - 1, 0) * pps
            return (layer, pt[b, n_pages - 1 - (first + pg)], 0, 0)
        return index

    seq_map = lambda b, s, pt: (b, 0, 0)
    page_specs = lambda shape: [pl.BlockSpec((None, None) + shape, page_map(pg)) for pg in range(pps)]
    grid_spec = pltpu.PrefetchScalarGridSpec(
        num_scalar_prefetch=1,
        grid=(dbs, n_pages // pps + 1),
        in_specs=([pl.BlockSpec((None, n_new, D_MODEL), seq_map)] * 3
                  + [pl.BlockSpec((None, rows, 1), seq_map), pl.BlockSpec((None, rows, n_new), seq_map)]
                  + page_specs((PAGE_SIZE * N_HEADS, HEAD_DIM)) * 2
                  + page_specs((N_HEADS, PAGE_SIZE))),
        out_specs=pl.BlockSpec((None, n_new, D_MODEL), seq_map),
        scratch_shapes=[pltpu.VMEM((rows, 1), F32), pltpu.VMEM((rows, 1), F32),
                        pltpu.VMEM((rows, HEAD_DIM), F32), pltpu.VMEM((N_HEADS, 1), F32)],
    )
    return pl.pallas_call(
        functools.partial(_fox_sample_kernel, n_new=n_new, pps=pps),
        grid_spec=grid_spec,
        out_shape=jax.ShapeDtypeStruct((dbs, n_new, D_MODEL), F32),
        compiler_params=_params("parallel", "arbitrary"),
        name="fox_sample",
    )(page_table, q, k, v, g_col, g_row, *([cache_k] * pps), *([cache_v] * pps), *([cache_logf_t] * pps))


def _take_top(s, count):
    return _take_top_ranked(s, count, with_rank=False)[0]


def _take_top_ranked(s, count, with_rank=True):
    tops = []
    rank = jnp.full_like(s, float(count)) if with_rank else None
    for i in range(count):
        m = jnp.max(s, axis=0, keepdims=True)
        tops.append(m)
        if with_rank or i + 1 < count:
            hit = s == m
            if with_rank:
                rank = jnp.where(hit, float(i), rank)
            if i + 1 < count:
                s = jnp.where(hit, NEG_INF, s)
    return tops, rank


def _candidate_sums(top1, top2):
    limit = PEER_TOPK + 1
    t2_head = jnp.concatenate(top2[:SUBLANES], axis=0)
    row = lax.broadcasted_iota(jnp.int32, t2_head.shape, 0)
    pieces = [top1[0] + jnp.concatenate(top2, axis=0)]
    for i in range(1, SUBLANES):
        pieces.append(jnp.where(row < limit // (i + 1), top1[i] + t2_head, NEG_INF))
    assert limit // (SUBLANES + 1) == 1
    pieces.append(jnp.concatenate(top1[SUBLANES:], axis=0) + top2[0])
    return jnp.concatenate(pieces, axis=0)


def _peer_route_kernel(q_ref, keys_ref, r2_ref, e2_ref, n2_ref, e1_ref, *, n_sub):
    nt = (((1,), (1,)), ((), ()))
    for sub in range(n_sub):
        tok = slice(sub * LANES, (sub + 1) * LANES) if n_sub > 1 else slice(None)
        q = q_ref[tok, :].astype(BF16)
        s1 = lax.dot_general(keys_ref[0].astype(BF16), q[:, :PEER_KEYS], nt, preferred_element_type=F32)
        s2 = lax.dot_general(keys_ref[1].astype(BF16), q[:, PEER_KEYS:], nt, preferred_element_type=F32)
        top1 = _take_top(s1, PEER_TOPK)
        top2, r2 = _take_top_ranked(s2, PEER_TOPK)
        best = _take_top(_candidate_sums(top1, top2), PEER_TOPK + 1)
        z = jnp.ones_like(best[0])
        for m in best[1:PEER_TOPK]:
            z += jnp.exp(m - best[0])
        cut = 0.5 * (best[PEER_TOPK - 1] + best[PEER_TOPK])
        n2 = jnp.zeros_like(s1)
        for t2 in top2:
            n2 += jnp.where(s1 + t2 > cut, 1.0, 0.0)
        r2_ref[:, tok] = r2.astype(BF16)
        e2_ref[:, tok] = jnp.where(s2 >= top2[PEER_TOPK - 1], jnp.exp(s2 - top2[0]), 0.0).astype(BF16)
        n2_ref[:, tok] = n2
        e1_ref[:, tok] = jnp.where(s1 >= top1[PEER_TOPK - 1], jnp.exp(s1 - top1[0]) / z, 0.0)


def peer_route(q, sub_keys, tm):
    t = q.shape[0]
    n_sub = max(tm // LANES, 1)
    out = lambda dtype: jax.ShapeDtypeStruct((PEER_HEADS, PEER_KEYS, t), dtype)
    ospec = pl.BlockSpec((None, PEER_KEYS, tm), lambda i, h: (h, 0, i))
    return pl.pallas_call(
        functools.partial(_peer_route_kernel, n_sub=n_sub),
        grid=(t // tm, PEER_HEADS),
        in_specs=[pl.BlockSpec((tm, 2 * PEER_KEYS), lambda i, h: (i, h)),
                  pl.BlockSpec((None, 2, PEER_KEYS, PEER_KEYS), lambda i, h: (h, 0, 0, 0))],
        out_specs=[ospec, ospec, ospec, ospec],
        out_shape=[out(BF16), out(BF16), out(F32), out(F32)],
        compiler_params=_params("parallel", "parallel"),
        name="peer_route",
    )(q, sub_keys)


def _peer_dense_kernel(x_ref, u_ref, vt_ref, r2_ref, e2_ref, n2_ref, e1_ref, res_ref, ln_g_ref, ln_b_ref,
                       o_ref, ob_ref, acc_sc, z_sc, g_sc, *, te, tm):
    k = pl.program_id(1)
    a_per_tile = te // PEER_KEYS
    rows_per_chunk = 2 * SUBLANES
    lanes_per_chunk = min(tm, 2 * LANES)
    n_chunks = PEER_KEYS // rows_per_chunk
    chunk_shape = (rows_per_chunk, lanes_per_chunk)

    @pl.when(k == 0)
    def _():
        acc_sc[...] = jnp.zeros_like(acc_sc)

    z_sc[...] = lax.dot_general(u_ref[...], x_ref[...], (((1,), (1,)), ((), ())), preferred_element_type=F32)
    for al in range(a_per_tile):
        for t0 in range(0, tm, lanes_per_chunk):
            tok = slice(t0, t0 + lanes_per_chunk)
            gates = [jnp.zeros(chunk_shape, BF16) for _ in range(n_chunks)]
            for h in range(PEER_HEADS):
                n2 = jnp.broadcast_to(n2_ref[h, al:al + 1, tok], chunk_shape).astype(BF16)
                e1 = jnp.broadcast_to(e1_ref[h, al:al + 1, tok], chunk_shape).astype(BF16)
                for c in range(n_chunks):
                    rows = slice(c * rows_per_chunk, (c + 1) * rows_per_chunk)
                    keep = r2_ref[h, rows, tok] < n2
                    gates[c] += jnp.where(keep, e2_ref[h, rows, tok] * e1, jnp.zeros(chunk_shape, BF16))
            for c in range(n_chunks):
                zrows = slice(al * PEER_KEYS + c * rows_per_chunk, al * PEER_KEYS + (c + 1) * rows_per_chunk)
                g_sc[zrows, tok] = gates[c] * _gelu_tanh(z_sc[zrows, tok].astype(BF16))
    acc_sc[...] += jnp.dot(vt_ref[...], g_sc[...], preferred_element_type=F32)

    @pl.when(k == pl.num_programs(1) - 1)
    def _():
        _store_norm(DN_ALPHA * res_ref[...] + acc_sc[...].T, ln_g_ref, ln_b_ref, o_ref, ob_ref)


def peer_dense(x, x_bf, u_bf, vt_bf, routes, ln_g, ln_b, tm, te):
    t, d = x_bf.shape
    n_tiles = vt_bf.shape[0]
    assert vt_bf.shape == (n_tiles, d, te)
    r2, e2, n2, e1 = routes
    a_per_tile = te // PEER_KEYS
    n2, e1 = (r.reshape(PEER_HEADS, n_tiles, a_per_tile, t) for r in (n2, e1))
    bspec = pl.BlockSpec((PEER_HEADS, PEER_KEYS, tm), lambda i, k: (0, 0, i))
    aspec = pl.BlockSpec((PEER_HEADS, None, a_per_tile, tm), lambda i, k: (0, k, 0, i))
    return pl.pallas_call(
        functools.partial(_peer_dense_kernel, te=te, tm=tm),
        grid=(t // tm, n_tiles),
        in_specs=[pl.BlockSpec((tm, d), lambda i, k: (i, 0)),
                  pl.BlockSpec((te, d), lambda i, k: (k, 0)),
                  pl.BlockSpec((None, d, te), lambda i, k: (k, 0, 0)),
                  bspec, bspec, aspec, aspec,
                  pl.BlockSpec((tm, d), lambda i, k: (i, 0)),
                  pl.BlockSpec((1, d), lambda i, k: (0, 0)), pl.BlockSpec((1, d), lambda i, k: (0, 0))],
        out_specs=[pl.BlockSpec((tm, d), lambda i, k: (i, 0)), pl.BlockSpec((tm, d), lambda i, k: (i, 0))],
        out_shape=[jax.ShapeDtypeStruct((t, d), F32), jax.ShapeDtypeStruct((t, d), BF16)],
        scratch_shapes=[pltpu.VMEM((d, tm), F32), pltpu.VMEM((te, tm), F32), pltpu.VMEM((te, tm), BF16)],
        compiler_params=_params("parallel", "arbitrary"),
        name="peer_dense",
    )(x_bf, u_bf, vt_bf, r2, e2, n2, e1, x, ln_g, ln_b)


def peer_block(x, x_bf, wq_bf, sub_keys, u_bf, vt_bf, ln_g, ln_b, tm, tm_route):
    q = matmul(x_bf, wq_bf, 0, wq_bf.shape[1], tm)
    routes = peer_route(q, sub_keys, tm_route)
    return peer_dense(x, x_bf, u_bf, vt_bf, routes, ln_g, ln_b, tm, PEER_EXPERT_TILE)


def _row_tile(t):
    tm = min(t, TOKEN_TILE)
    assert t % tm == 0
    return tm


def kernel(x_prompt, x_sample, cache_k, cache_v, cache_logf, page_table, state_s5_re, state_s5_im, w_in_even, s5_lambda_re, s5_lambda_im, s5_log_dt, s5_b_re, s5_b_im, s5_c_re, s5_c_im, s5_d, s5_w_glu, s5_b_glu, gm_ln_g, gm_ln_b, gm_w_s, gm_b_s, w_out_even, w_in_odd, b_f, w_out_odd, ln1_g, ln1_b, ln2_g, ln2_b, peer_w_q, peer_sub_keys, peer_u, peer_v):
    bsz, seq, d = x_prompt.shape
    dbs, dseq, _ = x_sample.shape
    n_odd, n_phys = cache_k.shape[0], cache_k.shape[1]
    tp, ts = bsz * seq, dbs * dseq
    xp = x_prompt.reshape(tp, d)
    xs = x_sample.reshape(ts, d)
    xp_bf, xs_bf = xp.astype(BF16), xs.astype(BF16)
    tm_p = _row_tile(tp)
    tm_s = _row_tile(ts)
    ck = cache_k.reshape(n_odd, n_phys, PAGE_SIZE * N_HEADS, HEAD_DIM)
    cv = cache_v.reshape(n_odd, n_phys, PAGE_SIZE * N_HEADS, HEAD_DIM)
    clf = cache_logf.transpose(0, 1, 3, 2)
    zero_state = jnp.zeros((bsz, S5_GROUPS, S5_STATE), F32)
    peer_rep = max(LANES // ts, 1)

    outs = {k: [] for k in ("kp", "vp", "lp", "ks", "vs", "ls", "s5p_re", "s5p_im", "s5s_re", "s5s_im", "gmv")}
    for li in range(DEPTH):
        j = li // 2
        g1, b1 = ln1_g[li].reshape(1, d), ln1_b[li].reshape(1, d)
        g2, b2 = ln2_g[li].reshape(1, d), ln2_b[li].reshape(1, d)
        if li % 2 == 0:
            tables = _s5_tables(s5_lambda_re[j], s5_lambda_im[j], s5_log_dt[j], s5_b_re[j], s5_b_im[j],
                                s5_c_re[j], s5_c_im[j])
            n_in = w_in_even.shape[2]
            w_in, w_glu, w_out = (w.astype(BF16) for w in (w_in_even[j], s5_w_glu[j], w_out_even[j]))

            def even(x, x_bf, nb, nl, h0_re, h0_im, tm):
                proj = matmul(x_bf, w_in, 0, n_in, tm).reshape(nb, nl, n_in)
                y, hl_re, hl_im = s5_scan(proj, h0_re, h0_im, tables, s5_d[j], min(nl, S5_CHUNK))
                a_out = s5_glu(y.reshape(nb * nl, S5_WIDTH), w_glu, s5_b_glu[j], tm)
                g_out, v_rows = chunk_gmlp(proj, gm_ln_g[j], gm_ln_b[j], gm_w_s[j], gm_b_s[j], min(nl, GM_CHUNK))
                x, x_bf = out_proj_dnln(a_out, g_out.reshape(nb * nl, GM_WIDTH), w_out, x, g1, b1, tm)
                return x, x_bf, hl_re, hl_im, v_rows

            xp, xp_bf, hp_re, hp_im, _ = even(xp, xp_bf, bsz, seq, zero_state, zero_state, tm_p)
            xs, xs_bf, hs_re, hs_im, v_rows = even(xs, xs_bf, dbs, dseq, state_s5_re[j].astype(F32),
                                                   state_s5_im[j].astype(F32), tm_s)
            outs["s5p_re"].append(hp_re)
            outs["s5p_im"].append(hp_im)
            outs["s5s_re"].append(hs_re)
            outs["s5s_im"].append(hs_im)
            outs["gmv"].append(v_rows)
        else:
            w_f = w_in_odd[j][:, 3 * d:]
            w_qkv, w_out = w_in_odd[j].astype(BF16), w_out_odd[j].astype(BF16)

            def project(x_bf, tm):
                q, k, v = (matmul(x_bf, w_qkv, part, d, tm) for part in range(3))
                return q, k, v, forget_log_gate(x_bf, w_f, b_f[j], tm)

            qp, kp, vp, lfp = project(xp_bf, tm_p)
            f_cum = jnp.cumsum(lfp.reshape(bsz, seq, N_HEADS), axis=1)
            ap = fox_prompt(qp.reshape(bsz, seq, d), kp.reshape(bsz, seq, d), vp.reshape(bsz, seq, d), f_cum,
                            min(seq, FOX_BLOCK))
            xp, xp_bf = out_proj_dnln(ap.reshape(tp, d), None, w_out, xp, g1, b1, tm_p)
            qs, kss, vss, lfs = project(xs_bf, tm_s)
            g_new = jnp.cumsum(lfs.reshape(dbs, dseq, N_HEADS), axis=1)
            a_s = fox_sample(qs.reshape(dbs, dseq, d), kss.reshape(dbs, dseq, d), vss.reshape(dbs, dseq, d),
                             g_new, ck, cv, clf, page_table, j)
            xs, xs_bf = out_proj_dnln(a_s.reshape(ts, d), None, w_out, xs, g1, b1, tm_s)
            outs["kp"].append(kp.reshape(bsz, seq, N_HEADS, HEAD_DIM))
            outs["vp"].append(vp.reshape(bsz, seq, N_HEADS, HEAD_DIM))
            outs["lp"].append(lfp.reshape(bsz, seq, N_HEADS))
            outs["ks"].append(kss.reshape(dbs, dseq, N_HEADS, HEAD_DIM))
            outs["vs"].append(vss.reshape(dbs, dseq, N_HEADS, HEAD_DIM))
            outs["ls"].append(lfs.reshape(dbs, dseq, N_HEADS))
        u_bf = peer_u[li].astype(BF16)
        vt_bf = peer_v[li].reshape(-1, PEER_EXPERT_TILE, d).transpose(0, 2, 1).astype(BF16)
        wq_bf = peer_w_q[li].astype(BF16)
        xp, xp_bf = peer_block(xp, xp_bf, wq_bf, peer_sub_keys[li], u_bf, vt_bf, g2, b2, tm_p,
                               min(tm_p, PEER_ROUTE_TILE))
        t_rep = peer_rep * ts
        xs, xs_bf = (r[:ts] for r in peer_block(jnp.tile(xs, (peer_rep, 1)), jnp.tile(xs_bf, (peer_rep, 1)), wq_bf,
                                                peer_sub_keys[li], u_bf, vt_bf, g2, b2, t_rep, t_rep))

    stack = lambda name, dtype: jnp.stack(outs[name]).astype(dtype)
    return (xp.reshape(bsz, seq, d), xs.reshape(dbs, dseq, d),
            stack("kp", cache_k.dtype), stack("vp", cache_v.dtype), stack("lp", cache_logf.dtype),
            stack("ks", cache_k.dtype), stack("vs", cache_v.dtype), stack("ls", cache_logf.dtype),
            stack("s5p_re", state_s5_re.dtype), stack("s5p_im", state_s5_im.dtype),
            stack("s5s_re", state_s5_re.dtype), stack("s5s_im", state_s5_im.dtype),
            jnp.stack(outs["gmv"]))
```

```python
import functools
import math

import jax
import jax.numpy as jnp
from jax import lax
from jax.experimental import pallas as pl
from jax.experimental.pallas import tpu as pltpu

F32 = jnp.float32
BF16 = jnp.bfloat16
HIGHEST = lax.Precision.HIGHEST

D_MODEL = 2048
DEPTH = 4
PAGE_SIZE = 128
S5_WIDTH = 1024
S5_GROUP = 16
S5_GROUPS = 64
S5_STATE = 64
S5_GROUP_BLOCK = 8
S5_BLOCKS = S5_GROUPS // S5_GROUP_BLOCK
S5_BLOCK_STATE = S5_GROUP_BLOCK * S5_STATE
GM_WIDTH = 1024
GM_CHUNK = 128
GM_HEADS = 8
GM_HEAD_DIM = 128
N_HEADS = 16
HEAD_DIM = 128
ATTN_SCALE = HEAD_DIM ** -0.5
TOKEN_TILE = 512
S5_CHUNK = 512
PEER_EXPERT_TILE = 1024
PEER_ROUTE_TILE = 512
FOX_BLOCK = 512
FOX_HEADS_PER_STEP = 4
FOX_PAGES_PER_STEP = 4
PEER_HEADS = 8
PEER_KEYS = 128
PEER_TOPK = 16
DN_ALPHA = (2.0 * DEPTH) ** 0.25
LN_EPS = 1e-5
SUBLANES = 8
LANES = 128
VMEM_LIMIT = 56 * 1024 * 1024
NEG_INF = float("-inf")


def _params(*sem):
    return pltpu.CompilerParams(dimension_semantics=sem, vmem_limit_bytes=VMEM_LIMIT)


def _gelu_tanh(x):
    c0 = math.sqrt(2.0 / math.pi)
    return (0.5 * x) * (1.0 + jnp.tanh(x * (c0 + (c0 * 0.044715) * (x * x))))


def _layer_norm(y, g, b):
    mu = jnp.mean(y, axis=-1, keepdims=True)
    yc = y - mu
    var = jnp.mean(yc * yc, axis=-1, keepdims=True)
    return yc * lax.rsqrt(var + LN_EPS) * g + b


def _mm_kernel(x_ref, w_ref, o_ref):
    o_ref[...] = jnp.dot(x_ref[...].astype(BF16), w_ref[...], preferred_element_type=F32)


def matmul(x, w_bf, col_block, n_out, tm):
    t, k = x.shape
    assert t % tm == 0
    return pl.pallas_call(
        _mm_kernel,
        grid=(t // tm,),
        in_specs=[pl.BlockSpec((tm, k), lambda i: (i, 0)),
                  pl.BlockSpec((k, n_out), lambda i: (0, col_block))],
        out_specs=pl.BlockSpec((tm, n_out), lambda i: (i, 0)),
        out_shape=jax.ShapeDtypeStruct((t, n_out), F32),
        compiler_params=_params("parallel"),
        name="matmul",
    )(x, w_bf)


def _store_norm(y, g_ref, beta_ref, o_ref, ob_ref):
    out = _layer_norm(y, g_ref[...], beta_ref[...])
    o_ref[...] = out
    ob_ref[...] = out.astype(BF16)


def _mm1_dnln_kernel(a_ref, w_ref, x_ref, g_ref, beta_ref, o_ref, ob_ref):
    m = jnp.dot(a_ref[...].astype(BF16), w_ref[...], preferred_element_type=F32)
    _store_norm(DN_ALPHA * x_ref[...] + m, g_ref, beta_ref, o_ref, ob_ref)


def _mm2_dnln_kernel(a_ref, b_ref, wa_ref, wb_ref, x_ref, g_ref, beta_ref, o_ref, ob_ref):
    m = jnp.dot(a_ref[...].astype(BF16), wa_ref[...], preferred_element_type=F32)
    m += jnp.dot(b_ref[...].astype(BF16), wb_ref[...], preferred_element_type=F32)
    _store_norm(DN_ALPHA * x_ref[...] + m, g_ref, beta_ref, o_ref, ob_ref)


def _norm_outputs(t, d, tm):
    row = lambda i: (i, 0)
    return ([pl.BlockSpec((tm, d), row), pl.BlockSpec((tm, d), row)],
            [jax.ShapeDtypeStruct((t, d), F32), jax.ShapeDtypeStruct((t, d), BF16)])


def out_proj_dnln(a, b, w_bf, x, g, beta, tm):
    t, ka = a.shape
    d = w_bf.shape[1]
    row = lambda i: (i, 0)
    fixed = lambda i: (0, 0)
    tail = [pl.BlockSpec((tm, d), row), pl.BlockSpec((1, d), fixed), pl.BlockSpec((1, d), fixed)]
    if b is None:
        kern = _mm1_dnln_kernel
        ins = [a, w_bf, x, g, beta]
        specs = [pl.BlockSpec((tm, ka), row), pl.BlockSpec((ka, d), fixed)] + tail
    else:
        assert b.shape[1] == ka
        kern = _mm2_dnln_kernel
        ins = [a, b, w_bf, w_bf, x, g, beta]
        specs = [pl.BlockSpec((tm, ka), row), pl.BlockSpec((tm, ka), row),
                 pl.BlockSpec((ka, d), fixed), pl.BlockSpec((ka, d), lambda i: (1, 0))] + tail
    out_specs, out_shape = _norm_outputs(t, d, tm)
    return pl.pallas_call(
        kern, grid=(t // tm,), in_specs=specs, out_specs=out_specs, out_shape=out_shape,
        compiler_params=_params("parallel"),
        name="out_proj_dnln",
    )(*ins)


def _split_bf16(x):
    hi = lax.bitcast_convert_type(lax.bitcast_convert_type(x, jnp.uint32) & jnp.uint32(0xFFFF0000), F32)
    return hi.astype(BF16), (x - hi).astype(BF16)


def _dot_split(a, b_ref):
    a_hi, a_lo = _split_bf16(a)
    dot = functools.partial(jnp.dot, preferred_element_type=F32)
    return dot(a_hi, b_ref[0]) + (dot(a_lo, b_ref[0]) + dot(a_hi, b_ref[1]))


def _s5_kernel(u_ref, bd_ref, cd_ref, tab_ref, d_ref, h0_ref, y_ref, hl_ref, hs_sc, carry_sc, *, lc):
    ns = S5_BLOCK_STATE

    @pl.when(pl.program_id(2) == 0)
    def _():
        carry_sc[...] = h0_ref[...]

    u = u_ref[...]
    hs_sc[...] = _dot_split(u, bd_ref)

    def row_group(r, carry):
        cr, ci = carry
        rows = pl.ds(pl.multiple_of(r * SUBLANES, SUBLANES), SUBLANES)
        xr = hs_sc[rows, :ns]
        xi = hs_sc[rows, ns:]
        for k in range(3):
            ar = tab_ref[2 * k]
            ai = tab_ref[2 * k + 1]
            sr = pltpu.roll(xr, 1 << k, 0)
            si = pltpu.roll(xi, 1 << k, 0)
            xr, xi = xr + (ar * sr - ai * si), xi + (ar * si + ai * sr)
        pr = tab_ref[6]
        pi = tab_ref[7]
        xr, xi = xr + (pr * cr - pi * ci), xi + (pr * ci + pi * cr)
        hs_sc[rows, :ns] = xr
        hs_sc[rows, ns:] = xi
        return xr[SUBLANES - 1:, :], xi[SUBLANES - 1:, :]

    cr, ci = lax.fori_loop(0, lc // SUBLANES, row_group, (carry_sc[:, :ns], carry_sc[:, ns:]))
    carry_sc[:, :ns] = cr
    carry_sc[:, ns:] = ci
    hl_ref[...] = carry_sc[...]
    y_ref[...] = _dot_split(hs_sc[...], cd_ref) + d_ref[...] * u


def _s5_tables(lam_re, lam_im, log_dt, b_re, b_im, c_re, c_im):
    dt = jnp.exp(log_dt.astype(F32))[:, None]
    mag = jnp.exp(lam_re * dt)
    lbr = mag * jnp.cos(lam_im * dt)
    lbi = mag * jnp.sin(lam_im * dt)
    den = lam_re * lam_re + lam_im * lam_im
    qr = ((lbr - 1.0) * lam_re + lbi * lam_im) / den
    qi = (lbi * lam_re - (lbr - 1.0) * lam_im) / den
    bbr = qr[..., None] * b_re - qi[..., None] * b_im
    bbi = qr[..., None] * b_im + qi[..., None] * b_re
    nb, gb = S5_BLOCKS, S5_GROUP_BLOCK
    eye = jnp.eye(gb, dtype=F32)

    def bdiag(m):
        m = m.reshape(nb, gb, S5_STATE, S5_GROUP)
        return jnp.einsum('ngpc,gh->ngchp', m, eye).reshape(nb, gb * S5_GROUP, gb * S5_STATE)

    def cdiag(m):
        m = m.reshape(nb, gb, S5_GROUP, S5_STATE)
        return jnp.einsum('ngcp,gh->ngphc', m, eye).reshape(nb, gb * S5_STATE, gb * S5_GROUP)

    bd = jnp.stack(_split_bf16(jnp.concatenate([bdiag(bbr), bdiag(bbi)], axis=-1)), axis=1)
    cd = jnp.stack(_split_bf16(jnp.concatenate([cdiag(c_re), cdiag(-c_im)], axis=1)), axis=1)

    def cmul(a, b):
        return a[0] * b[0] - a[1] * b[1], a[0] * b[1] + a[1] * b[0]

    powers = [(lbr, lbi)]
    for _ in range(SUBLANES - 1):
        powers.append(cmul(powers[-1], (lbr, lbi)))
    row = jnp.arange(SUBLANES)[:, None, None]
    tabs = []
    for k in range(3):
        sh = 1 << k
        for part in powers[sh - 1]:
            tabs.append(jnp.where(row >= sh, part[None], 0.0))
    tabs.append(jnp.stack([p[0] for p in powers]))
    tabs.append(jnp.stack([p[1] for p in powers]))
    tab = jnp.stack(tabs)
    tab = tab.reshape(8, SUBLANES, nb, S5_BLOCK_STATE).transpose(2, 0, 1, 3)
    return bd, cd, tab


def s5_scan(proj, h0_re, h0_im, tables, d_skip, lc):
    bd, cd, tab = tables
    bsz, seq = proj.shape[0], proj.shape[1]
    nb, ns = S5_BLOCKS, S5_BLOCK_STATE
    h0 = jnp.concatenate([h0_re.reshape(bsz, nb, 1, ns), h0_im.reshape(bsz, nb, 1, ns)], axis=-1)
    y, hl = pl.pallas_call(
        functools.partial(_s5_kernel, lc=lc),
        grid=(nb, bsz, seq // lc),
        in_specs=[pl.BlockSpec((None, lc, LANES), lambda g, b, l: (b, l, g)),
                  pl.BlockSpec((None, 2, LANES, 2 * ns), lambda g, b, l: (g, 0, 0, 0)),
                  pl.BlockSpec((None, 2, 2 * ns, LANES), lambda g, b, l: (g, 0, 0, 0)),
                  pl.BlockSpec((None, 8, SUBLANES, ns), lambda g, b, l: (g, 0, 0, 0)),
                  pl.BlockSpec((1, LANES), lambda g, b, l: (0, g)),
                  pl.BlockSpec((None, None, 1, 2 * ns), lambda g, b, l: (b, g, 0, 0))],
        out_specs=[pl.BlockSpec((None, lc, LANES), lambda g, b, l: (b, l, g)),
                   pl.BlockSpec((None, None, 1, 2 * ns), lambda g, b, l: (b, g, 0, 0))],
        out_shape=[jax.ShapeDtypeStruct((bsz, seq, S5_WIDTH), F32),
                   jax.ShapeDtypeStruct((bsz, nb, 1, 2 * ns), F32)],
        scratch_shapes=[pltpu.VMEM((lc, 2 * ns), F32), pltpu.VMEM((1, 2 * ns), F32)],
        compiler_params=_params("parallel", "parallel", "arbitrary"),
        name="s5_scan",
    )(proj, bd, cd, tab, d_skip.reshape(1, S5_WIDTH), h0)
    hl_re = hl[..., :ns].reshape(bsz, S5_GROUPS, S5_STATE)
    hl_im = hl[..., ns:].reshape(bsz, S5_GROUPS, S5_STATE)
    return y, hl_re, hl_im


def _glu_kernel(y_ref, w_ref, b_ref, o_ref):
    y = jax.nn.gelu(y_ref[...])
    z = jnp.dot(y.astype(BF16), w_ref[...], preferred_element_type=F32) + b_ref[...]
    o_ref[...] = y * jax.nn.sigmoid(z)


def s5_glu(y, w, b, tm):
    t, k = y.shape
    return pl.pallas_call(
        _glu_kernel, grid=(t // tm,),
        in_specs=[pl.BlockSpec((tm, k), lambda i: (i, 0)), pl.BlockSpec((k, k), lambda i: (0, 0)),
                  pl.BlockSpec((1, k), lambda i: (0, 0))],
        out_specs=pl.BlockSpec((tm, k), lambda i: (i, 0)),
        out_shape=jax.ShapeDtypeStruct((t, k), F32),
        compiler_params=_params("parallel"),
        name="s5_glu",
    )(y, w, b.reshape(1, k))


def _gmlp_kernel(zu_ref, zv_ref, g_ref, b_ref, w_ref, bs_ref, o_ref, v_ref, *, lc):
    u = jax.nn.gelu(zu_ref[...])
    v = _layer_norm(jax.nn.gelu(zv_ref[...]), g_ref[...], b_ref[...])
    v_ref[...] = v
    row = lax.broadcasted_iota(jnp.int32, (lc, lc), 0)
    col = lax.broadcasted_iota(jnp.int32, (lc, lc), 1)
    causal = row >= col
    vb = v.astype(BF16)
    bs = bs_ref[...]
    for h in range(GM_HEADS):
        cols = slice(h * GM_HEAD_DIM, (h + 1) * GM_HEAD_DIM)
        w = jnp.where(causal, w_ref[h], 0.0).astype(BF16)
        mixed = jnp.dot(w, vb[:, cols], preferred_element_type=F32) + bs[:, h:h + 1]
        o_ref[:, cols] = u[:, cols] * mixed


def chunk_gmlp(proj, ln_g, ln_b, w_s, b_s, lc):
    bsz, seq = proj.shape[0], proj.shape[1]
    w = w_s[:, :lc, :lc]
    bs = b_s[:, :lc].T
    fixed2 = lambda b, n: (0, 0)
    return pl.pallas_call(
        functools.partial(_gmlp_kernel, lc=lc),
        grid=(bsz, seq // lc),
        in_specs=[pl.BlockSpec((None, lc, GM_WIDTH), lambda b, n: (b, n, 1)),
                  pl.BlockSpec((None, lc, GM_WIDTH), lambda b, n: (b, n, 2)),
                  pl.BlockSpec((1, GM_WIDTH), fixed2), pl.BlockSpec((1, GM_WIDTH), fixed2),
                  pl.BlockSpec((GM_HEADS, lc, lc), lambda b, n: (0, 0, 0)),
                  pl.BlockSpec((lc, GM_HEADS), fixed2)],
        out_specs=[pl.BlockSpec((None, lc, GM_WIDTH), lambda b, n: (b, n, 0)),
                   pl.BlockSpec((None, lc, GM_WIDTH), lambda b, n: (b, n, 0))],
        out_shape=[jax.ShapeDtypeStruct((bsz, seq, GM_WIDTH), F32),
                   jax.ShapeDtypeStruct((bsz, seq, GM_WIDTH), F32)],
        compiler_params=_params("parallel", "parallel"),
        name="chunk_gmlp",
    )(proj, proj, ln_g.reshape(1, GM_WIDTH), ln_b.reshape(1, GM_WIDTH), w, bs)


def _logf_kernel(x_ref, w_ref, b_ref, o_ref):
    z = jnp.dot(x_ref[...].astype(BF16), w_ref[...].astype(BF16), preferred_element_type=F32)
    o_ref[...] = jax.nn.log_sigmoid(z + b_ref[...])


def forget_log_gate(x, w_f, b_f, tm):
    t, k = x.shape
    return pl.pallas_call(
        _logf_kernel, grid=(t // tm,),
        in_specs=[pl.BlockSpec((tm, k), lambda i: (i, 0)), pl.BlockSpec((k, N_HEADS), lambda i: (0, 0)),
                  pl.BlockSpec((1, N_HEADS), lambda i: (0, 0))],
        out_specs=pl.BlockSpec((tm, N_HEADS), lambda i: (i, 0)),
        out_shape=jax.ShapeDtypeStruct((t, N_HEADS), F32),
        compiler_params=_params("parallel"),
        name="forget_log_gate",
    )(x, w_f, b_f.reshape(1, N_HEADS))


def _fox_prompt_kernel(q_ref, k_ref, v_ref, fq_ref, fk_ref, o_ref, m_sc, l_sc, acc_sc, *, tq):
    hg = pl.program_id(1)
    qi = pl.program_id(2)
    ki = pl.program_id(3)

    @pl.when(ki == 0)
    def _():
        m_sc[...] = jnp.full_like(m_sc, NEG_INF)
        l_sc[...] = jnp.zeros_like(l_sc)
        acc_sc[...] = jnp.zeros_like(acc_sc)

    def block(on_diagonal):
        for hl in range(FOX_HEADS_PER_STEP):
            cols = slice(hl * HEAD_DIM, (hl + 1) * HEAD_DIM)
            q = (q_ref[:, cols] * ATTN_SCALE).astype(BF16)
            s = lax.dot_general(q, k_ref[:, cols].astype(BF16), (((1,), (1,)), ((), ())),
                                preferred_element_type=F32)
            head = lax.broadcasted_iota(jnp.int32, (tq, N_HEADS), 1) == hg * FOX_HEADS_PER_STEP + hl
            fq = jnp.sum(jnp.where(head, fq_ref[...], 0.0), axis=-1, keepdims=True)
            s = s + fq - fk_ref[hl]
            if on_diagonal:
                causal = (lax.broadcasted_iota(jnp.int32, (tq, tq), 0)
                          >= lax.broadcasted_iota(jnp.int32, (tq, tq), 1))
                s = jnp.where(causal, s, NEG_INF)
            m_old = m_sc[hl]
            m_new = jnp.maximum(m_old, jnp.max(s, axis=-1, keepdims=True))
            alpha = jnp.exp(m_old - m_new)
            p = jnp.exp(s - m_new)
            l_sc[hl] = alpha * l_sc[hl] + jnp.sum(p, axis=-1, keepdims=True)
            acc_sc[:, cols] = alpha * acc_sc[:, cols] + jnp.dot(p.astype(BF16), v_ref[:, cols].astype(BF16),
                                                                 preferred_element_type=F32)
            m_sc[hl] = m_new

    @pl.when(ki < qi)
    def _():
        block(False)

    @pl.when(ki == qi)
    def _():
        block(True)

    @pl.when(ki == pl.num_programs(3) - 1)
    def _():
        for hl in range(FOX_HEADS_PER_STEP):
            cols = slice(hl * HEAD_DIM, (hl + 1) * HEAD_DIM)
            o_ref[:, cols] = acc_sc[:, cols] / l_sc[hl]


def fox_prompt(q, k, v, f_cum, tq):
    bsz, seq = q.shape[0], q.shape[1]
    nq = seq // tq
    hps = FOX_HEADS_PER_STEP
    width = hps * HEAD_DIM
    f_rows = f_cum.transpose(0, 2, 1).reshape(bsz, N_HEADS, 1, seq)
    kv_map = lambda b, h, i, j: (b, jnp.minimum(j, i), h)
    return pl.pallas_call(
        functools.partial(_fox_prompt_kernel, tq=tq),
        grid=(bsz, N_HEADS // hps, nq, nq),
        in_specs=[pl.BlockSpec((None, tq, width), lambda b, h, i, j: (b, i, h)),
                  pl.BlockSpec((None, tq, width), kv_map),
                  pl.BlockSpec((None, tq, width), kv_map),
                  pl.BlockSpec((None, tq, N_HEADS), lambda b, h, i, j: (b, i, 0)),
                  pl.BlockSpec((None, hps, 1, tq), lambda b, h, i, j: (b, h, 0, jnp.minimum(j, i)))],
        out_specs=pl.BlockSpec((None, tq, width), lambda b, h, i, j: (b, i, h)),
        out_shape=jax.ShapeDtypeStruct((bsz, seq, D_MODEL), F32),
        scratch_shapes=[pltpu.VMEM((hps, tq, 1), F32), pltpu.VMEM((hps, tq, 1), F32),
                        pltpu.VMEM((tq, width), F32)],
        compiler_params=_params("parallel", "parallel", "parallel", "arbitrary"),
        name="fox_prompt",
    )(q, k, v, f_cum, f_rows)


def _fox_sample_kernel(pt_ref, q_ref, kn_ref, vn_ref, gc_ref, gr_ref, *rest, n_new, pps):
    kp_refs, vp_refs, lp_refs = rest[:pps], rest[pps:2 * pps], rest[2 * pps:3 * pps]
    o_ref, m_sc, l_sc, acc_sc, carry_sc = rest[3 * pps:]
    step = pl.program_id(1)
    nt = (((1,), (1,)), ((), ()))
    head_rows = lambda h: slice(h * n_new, (h + 1) * n_new)
    head_cols = lambda h: slice(h * HEAD_DIM, (h + 1) * HEAD_DIM)
    q_heads = [(q_ref[:, head_cols(h)] * ATTN_SCALE).astype(BF16) for h in range(N_HEADS)]

    def online_update(s, values):
        m_old = m_sc[...]
        m_new = jnp.maximum(m_old, jnp.max(s, axis=-1, keepdims=True))
        alpha = jnp.exp(m_old - m_new)
        p = jnp.exp(s - m_new)
        l_sc[...] = alpha * l_sc[...] + jnp.sum(p, axis=-1, keepdims=True)
        pv = [jnp.dot(p[head_rows(h), :].astype(BF16), values(h), preferred_element_type=F32)
              for h in range(N_HEADS)]
        acc_sc[...] = alpha * acc_sc[...] + jnp.concatenate(pv, axis=0)
        m_sc[...] = m_new

    @pl.when(step == 0)
    def _():
        m_sc[...] = jnp.full_like(m_sc, NEG_INF)
        l_sc[...] = jnp.zeros_like(l_sc)
        acc_sc[...] = jnp.zeros_like(acc_sc)
        carry_sc[...] = jnp.zeros_like(carry_sc)
        s = jnp.concatenate([lax.dot_general(q_heads[h], kn_ref[:, head_cols(h)].astype(BF16), nt,
                                             preferred_element_type=F32) for h in range(N_HEADS)], axis=0)
        s = s + gc_ref[...] - gr_ref[...]
        rows = N_HEADS * n_new
        causal = (lax.broadcasted_iota(jnp.int32, (rows, n_new), 0) % n_new
                  >= lax.broadcasted_iota(jnp.int32, (rows, n_new), 1))
        online_update(jnp.where(causal, s, NEG_INF), lambda h: vn_ref[:, head_cols(h)].astype(BF16))

    @pl.when(step > 0)
    def _():
        later = (lax.broadcasted_iota(jnp.int32, (PAGE_SIZE, PAGE_SIZE), 0)
                 > lax.broadcasted_iota(jnp.int32, (PAGE_SIZE, PAGE_SIZE), 1)).astype(F32)
        carry = carry_sc[...]
        page_scores = []
        for pg in range(pps):
            lp = lp_refs[pg][...]
            suffix = jnp.dot(lp, later, precision=HIGHEST, preferred_element_type=F32) + carry
            carry = carry + jnp.sum(lp, axis=-1, keepdims=True)
            qk = [lax.dot_general(q_heads[h], kp_refs[pg][pl.ds(h, PAGE_SIZE, stride=N_HEADS), :].astype(BF16),
                                  nt, preferred_element_type=F32)
                  + jnp.broadcast_to(suffix[h:h + 1, :], (n_new, PAGE_SIZE)) for h in range(N_HEADS)]
            page_scores.append(jnp.concatenate(qk, axis=0))
        carry_sc[...] = carry
        s = jnp.concatenate(page_scores, axis=1) + gc_ref[...]

        def values(h):
            return jnp.concatenate([vp_refs[pg][pl.ds(h, PAGE_SIZE, stride=N_HEADS), :].astype(BF16)
                                    for pg in range(pps)], axis=0)

        online_update(s, values)

    @pl.when(step == pl.num_programs(1) - 1)
    def _():
        out = acc_sc[...] / l_sc[...]
        for h in range(N_HEADS):
            o_ref[:, head_cols(h)] = out[head_rows(h), :]


def fox_sample(q, k, v, g_new, cache_k, cache_v, cache_logf_t, page_table, layer):
    dbs, n_new = q.shape[0], q.shape[1]
    n_pages = page_table.shape[1]
    pps = math.gcd(n_pages, FOX_PAGES_PER_STEP)
    rows = N_HEADS * n_new
    g_t = g_new.transpose(0, 2, 1)
    g_col = g_t.reshape(dbs, rows, 1)
    g_row = jnp.repeat(g_t, n_new, axis=1)

    def page_map(pg):
        def index(b, s, pt):
            first = jnp.maximum(s - 1, 0) * pps
            return (layer, pt[b, n_pages - 1 - (first + pg)], 0, 0)
        return index

    seq_map = lambda b, s, pt: (b, 0, 0)
    page_specs = lambda shape: [pl.BlockSpec((None, None) + shape, page_map(pg)) for pg in range(pps)]
    grid_spec = pltpu.PrefetchScalarGridSpec(
        num_scalar_prefetch=1,
        grid=(dbs, n_pages // pps + 1),
        in_specs=([pl.BlockSpec((None, n_new, D_MODEL), seq_map)] * 3
                  + [pl.BlockSpec((None, rows, 1), seq_map), pl.BlockSpec((None, rows, n_new), seq_map)]
                  + page_specs((PAGE_SIZE * N_HEADS, HEAD_DIM)) * 2
                  + page_specs((N_HEADS, PAGE_SIZE))),
        out_specs=pl.BlockSpec((None, n_new, D_MODEL), seq_map),
        scratch_shapes=[pltpu.VMEM((rows, 1), F32), pltpu.VMEM((rows, 1), F32),
                        pltpu.VMEM((rows, HEAD_DIM), F32), pltpu.VMEM((N_HEADS, 1), F32)],
    )
    return pl.pallas_call(
        functools.partial(_fox_sample_kernel, n_new=n_new, pps=pps),
        grid_spec=grid_spec,
        out_shape=jax.ShapeDtypeStruct((dbs, n_new, D_MODEL), F32),
        compiler_params=_params("parallel", "arbitrary"),
        name="fox_sample",
    )(page_table, q, k, v, g_col, g_row, *([cache_k] * pps), *([cache_v] * pps), *([cache_logf_t] * pps))


def _take_top(s, count):
    return _take_top_ranked(s, count, with_rank=False)[0]


def _take_top_ranked(s, count, with_rank=True):
    tops = []
    rank = jnp.full_like(s, float(count)) if with_rank else None
    for i in range(count):
        m = jnp.max(s, axis=0, keepdims=True)
        tops.append(m)
        if with_rank or i + 1 < count:
            hit = s == m
            if with_rank:
                rank = jnp.where(hit, float(i), rank)
            if i + 1 < count:
                s = jnp.where(hit, NEG_INF, s)
    return tops, rank


def _candidate_sums(top1, top2):
    limit = PEER_TOPK + 1
    t2_head = jnp.concatenate(top2[:SUBLANES], axis=0)
    row = lax.broadcasted_iota(jnp.int32, t2_head.shape, 0)
    pieces = [top1[0] + jnp.concatenate(top2, axis=0)]
    for i in range(1, SUBLANES):
        pieces.append(jnp.where(row < limit // (i + 1), top1[i] + t2_head, NEG_INF))
    assert limit // (SUBLANES + 1) == 1
    pieces.append(jnp.concatenate(top1[SUBLANES:], axis=0) + top2[0])
    return jnp.concatenate(pieces, axis=0)


def _peer_route_kernel(q_ref, keys_ref, r2_ref, e2_ref, n2_ref, e1_ref, *, n_sub):
    nt = (((1,), (1,)), ((), ()))
    for sub in range(n_sub):
        tok = slice(sub * LANES, (sub + 1) * LANES) if n_sub > 1 else slice(None)
        q = q_ref[tok, :].astype(BF16)
        s1 = lax.dot_general(keys_ref[0].astype(BF16), q[:, :PEER_KEYS], nt, preferred_element_type=F32)
        s2 = lax.dot_general(keys_ref[1].astype(BF16), q[:, PEER_KEYS:], nt, preferred_element_type=F32)
        top1 = _take_top(s1, PEER_TOPK)
        top2, r2 = _take_top_ranked(s2, PEER_TOPK)
        best = _take_top(_candidate_sums(top1, top2), PEER_TOPK + 1)
        z = jnp.ones_like(best[0])
        for m in best[1:PEER_TOPK]:
            z += jnp.exp(m - best[0])
        cut = 0.5 * (best[PEER_TOPK - 1] + best[PEER_TOPK])
        n2 = jnp.zeros_like(s1)
        for t2 in top2:
            n2 += jnp.where(s1 + t2 > cut, 1.0, 0.0)
        r2_ref[:, tok] = r2.astype(BF16)
        e2_ref[:, tok] = jnp.where(s2 >= top2[PEER_TOPK - 1], jnp.exp(s2 - top2[0]), 0.0).astype(BF16)
        n2_ref[:, tok] = n2
        e1_ref[:, tok] = jnp.where(s1 >= top1[PEER_TOPK - 1], jnp.exp(s1 - top1[0]) / z, 0.0)


def peer_route(q, sub_keys, tm):
    t = q.shape[0]
    n_sub = max(tm // LANES, 1)
    out = lambda dtype: jax.ShapeDtypeStruct((PEER_HEADS, PEER_KEYS, t), dtype)
    ospec = pl.BlockSpec((None, PEER_KEYS, tm), lambda i, h: (h, 0, i))
    return pl.pallas_call(
        functools.partial(_peer_route_kernel, n_sub=n_sub),
        grid=(t // tm, PEER_HEADS),
        in_specs=[pl.BlockSpec((tm, 2 * PEER_KEYS), lambda i, h: (i, h)),
                  pl.BlockSpec((None, 2, PEER_KEYS, PEER_KEYS), lambda i, h: (h, 0, 0, 0))],
        out_specs=[ospec, ospec, ospec, ospec],
        out_shape=[out(BF16), out(BF16), out(F32), out(F32)],
        compiler_params=_params("parallel", "parallel"),
        name="peer_route",
    )(q, sub_keys)


def _peer_dense_kernel(x_ref, u_ref, vt_ref, r2_ref, e2_ref, n2_ref, e1_ref, res_ref, ln_g_ref, ln_b_ref,
                       o_ref, ob_ref, acc_sc, z_sc, g_sc, *, te, tm):
    k = pl.program_id(1)
    a_per_tile = te // PEER_KEYS
    rows_per_chunk = 2 * SUBLANES
    lanes_per_chunk = min(tm, 2 * LANES)
    n_chunks = PEER_KEYS // rows_per_chunk
    chunk_shape = (rows_per_chunk, lanes_per_chunk)

    @pl.when(k == 0)
    def _():
        acc_sc[...] = jnp.zeros_like(acc_sc)

    z_sc[...] = lax.dot_general(u_ref[...], x_ref[...], (((1,), (1,)), ((), ())), preferred_element_type=F32)
    for al in range(a_per_tile):
        for t0 in range(0, tm, lanes_per_chunk):
            tok = slice(t0, t0 + lanes_per_chunk)
            gates = [jnp.zeros(chunk_shape, BF16) for _ in range(n_chunks)]
            for h in range(PEER_HEADS):
                n2 = jnp.broadcast_to(n2_ref[h, al:al + 1, tok], chunk_shape).astype(BF16)
                e1 = jnp.broadcast_to(e1_ref[h, al:al + 1, tok], chunk_shape).astype(BF16)
                for c in range(n_chunks):
                    rows = slice(c * rows_per_chunk, (c + 1) * rows_per_chunk)
                    keep = r2_ref[h, rows, tok] < n2
                    gates[c] += jnp.where(keep, e2_ref[h, rows, tok] * e1, jnp.zeros(chunk_shape, BF16))
            for c in range(n_chunks):
                zrows = slice(al * PEER_KEYS + c * rows_per_chunk, al * PEER_KEYS + (c + 1) * rows_per_chunk)
                g_sc[zrows, tok] = gates[c] * _gelu_tanh(z_sc[zrows, tok].astype(BF16))
    acc_sc[...] += jnp.dot(vt_ref[...], g_sc[...], preferred_element_type=F32)

    @pl.when(k == pl.num_programs(1) - 1)
    def _():
        _store_norm(DN_ALPHA * res_ref[...] + acc_sc[...].T, ln_g_ref, ln_b_ref, o_ref, ob_ref)


def peer_dense(x, x_bf, u_bf, vt_bf, routes, ln_g, ln_b, tm, te):
    t, d = x_bf.shape
    n_tiles = vt_bf.shape[0]
    assert vt_bf.shape == (n_tiles, d, te)
    r2, e2, n2, e1 = routes
    a_per_tile = te // PEER_KEYS
    n2, e1 = (r.reshape(PEER_HEADS, n_tiles, a_per_tile, t) for r in (n2, e1))
    bspec = pl.BlockSpec((PEER_HEADS, PEER_KEYS, tm), lambda i, k: (0, 0, i))
    aspec = pl.BlockSpec((PEER_HEADS, None, a_per_tile, tm), lambda i, k: (0, k, 0, i))
    once_per_tile = pl.Buffered(1)
    return pl.pallas_call(
        functools.partial(_peer_dense_kernel, te=te, tm=tm),
        grid=(t // tm, n_tiles),
        in_specs=[pl.BlockSpec((tm, d), lambda i, k: (i, 0), pipeline_mode=once_per_tile),
                  pl.BlockSpec((te, d), lambda i, k: (k, 0)),
                  pl.BlockSpec((None, d, te), lambda i, k: (k, 0, 0)),
                  bspec, bspec, aspec, aspec,
                  pl.BlockSpec((tm, d), lambda i, k: (i, 0), pipeline_mode=once_per_tile),
                  pl.BlockSpec((1, d), lambda i, k: (0, 0)), pl.BlockSpec((1, d), lambda i, k: (0, 0))],
        out_specs=[pl.BlockSpec((tm, d), lambda i, k: (i, 0)), pl.BlockSpec((tm, d), lambda i, k: (i, 0))],
        out_shape=[jax.ShapeDtypeStruct((t, d), F32), jax.ShapeDtypeStruct((t, d), BF16)],
        scratch_shapes=[pltpu.VMEM((d, tm), F32), pltpu.VMEM((te, tm), F32), pltpu.VMEM((te, tm), BF16)],
        compiler_params=_params("parallel", "arbitrary"),
        name="peer_dense",
    )(x_bf, u_bf, vt_bf, r2, e2, n2, e1, x, ln_g, ln_b)


def peer_block(x, x_bf, wq_bf, sub_keys, u_bf, vt_bf, ln_g, ln_b, tm, tm_route):
    q = matmul(x_bf, wq_bf, 0, wq_bf.shape[1], tm)
    routes = peer_route(q, sub_keys, tm_route)
    return peer_dense(x, x_bf, u_bf, vt_bf, routes, ln_g, ln_b, tm, PEER_EXPERT_TILE)


def _row_tile(t):
    tm = min(t, TOKEN_TILE)
    assert t % tm == 0
    return tm


def kernel(x_prompt, x_sample, cache_k, cache_v, cache_logf, page_table, state_s5_re, state_s5_im, w_in_even, s5_lambda_re, s5_lambda_im, s5_log_dt, s5_b_re, s5_b_im, s5_c_re, s5_c_im, s5_d, s5_w_glu, s5_b_glu, gm_ln_g, gm_ln_b, gm_w_s, gm_b_s, w_out_even, w_in_odd, b_f, w_out_odd, ln1_g, ln1_b, ln2_g, ln2_b, peer_w_q, peer_sub_keys, peer_u, peer_v):
    bsz, seq, d = x_prompt.shape
    dbs, dseq, _ = x_sample.shape
    n_odd, n_phys = cache_k.shape[0], cache_k.shape[1]
    tp, ts = bsz * seq, dbs * dseq
    xp = x_prompt.reshape(tp, d)
    xs = x_sample.reshape(ts, d)
    xp_bf, xs_bf = xp.astype(BF16), xs.astype(BF16)
    tm_p = _row_tile(tp)
    tm_s = _row_tile(ts)
    ck = cache_k.reshape(n_odd, n_phys, PAGE_SIZE * N_HEADS, HEAD_DIM)
    cv = cache_v.reshape(n_odd, n_phys, PAGE_SIZE * N_HEADS, HEAD_DIM)
    clf = cache_logf.transpose(0, 1, 3, 2)
    zero_state = jnp.zeros((bsz, S5_GROUPS, S5_STATE), F32)
    peer_rep = max(LANES // ts, 1)

    outs = {k: [] for k in ("kp", "vp", "lp", "ks", "vs", "ls", "s5p_re", "s5p_im", "s5s_re", "s5s_im", "gmv")}
    for li in range(DEPTH):
        j = li // 2
        g1, b1 = ln1_g[li].reshape(1, d), ln1_b[li].reshape(1, d)
        g2, b2 = ln2_g[li].reshape(1, d), ln2_b[li].reshape(1, d)
        if li % 2 == 0:
            tables = _s5_tables(s5_lambda_re[j], s5_lambda_im[j], s5_log_dt[j], s5_b_re[j], s5_b_im[j],
                                s5_c_re[j], s5_c_im[j])
            n_in = w_in_even.shape[2]
            w_in, w_glu, w_out = (w.astype(BF16) for w in (w_in_even[j], s5_w_glu[j], w_out_even[j]))

            def even(x, x_bf, nb, nl, h0_re, h0_im, tm):
                proj = matmul(x_bf, w_in, 0, n_in, tm).reshape(nb, nl, n_in)
                y, hl_re, hl_im = s5_scan(proj, h0_re, h0_im, tables, s5_d[j], min(nl, S5_CHUNK))
                a_out = s5_glu(y.reshape(nb * nl, S5_WIDTH), w_glu, s5_b_glu[j], tm)
                g_out, v_rows = chunk_gmlp(proj, gm_ln_g[j], gm_ln_b[j], gm_w_s[j], gm_b_s[j], min(nl, GM_CHUNK))
                x, x_bf = out_proj_dnln(a_out, g_out.reshape(nb * nl, GM_WIDTH), w_out, x, g1, b1, tm)
                return x, x_bf, hl_re, hl_im, v_rows

            xp, xp_bf, hp_re, hp_im, _ = even(xp, xp_bf, bsz, seq, zero_state, zero_state, tm_p)
            xs, xs_bf, hs_re, hs_im, v_rows = even(xs, xs_bf, dbs, dseq, state_s5_re[j].astype(F32),
                                                   state_s5_im[j].astype(F32), tm_s)
            outs["s5p_re"].append(hp_re)
            outs["s5p_im"].append(hp_im)
            outs["s5s_re"].append(hs_re)
            outs["s5s_im"].append(hs_im)
            outs["gmv"].append(v_rows)
        else:
            w_f = w_in_odd[j][:, 3 * d:]
            w_qkv, w_out = w_in_odd[j].astype(BF16), w_out_odd[j].astype(BF16)

            def project(x_bf, tm):
                q, k, v = (matmul(x_bf, w_qkv, part, d, tm) for part in range(3))
                return q, k, v, forget_log_gate(x_bf, w_f, b_f[j], tm)

            qp, kp, vp, lfp = project(xp_bf, tm_p)
            f_cum = jnp.cumsum(lfp.reshape(bsz, seq, N_HEADS), axis=1)
            ap = fox_prompt(qp.reshape(bsz, seq, d), kp.reshape(bsz, seq, d), vp.reshape(bsz, seq, d), f_cum,
                            min(seq, FOX_BLOCK))
            xp, xp_bf = out_proj_dnln(ap.reshape(tp, d), None, w_out, xp, g1, b1, tm_p)
            qs, kss, vss, lfs = project(xs_bf, tm_s)
            g_new = jnp.cumsum(lfs.reshape(dbs, dseq, N_HEADS), axis=1)
            a_s = fox_sample(qs.reshape(dbs, dseq, d), kss.reshape(dbs, dseq, d), vss.reshape(dbs, dseq, d),
                             g_new, ck, cv, clf, page_table, j)
            xs, xs_bf = out_proj_dnln(a_s.reshape(ts, d), None, w_out, xs, g1, b1, tm_s)
            outs["kp"].append(kp.reshape(bsz, seq, N_HEADS, HEAD_DIM))
            outs["vp"].append(vp.reshape(bsz, seq, N_HEADS, HEAD_DIM))
            outs["lp"].append(lfp.reshape(bsz, seq, N_HEADS))
            outs["ks"].append(kss.reshape(dbs, dseq, N_HEADS, HEAD_DIM))
            outs["vs"].append(vss.reshape(dbs, dseq, N_HEADS, HEAD_DIM))
            outs["ls"].append(lfs.reshape(dbs, dseq, N_HEADS))
        u_bf = peer_u[li].astype(BF16)
        vt_bf = peer_v[li].reshape(-1, PEER_EXPERT_TILE, d).transpose(0, 2, 1).astype(BF16)
        wq_bf = peer_w_q[li].astype(BF16)
        xp, xp_bf = peer_block(xp, xp_bf, wq_bf, peer_sub_keys[li], u_bf, vt_bf, g2, b2, tm_p,
                               min(tm_p, PEER_ROUTE_TILE))
        t_rep = peer_rep * ts
        xs, xs_bf = (r[:ts] for r in peer_block(jnp.tile(xs, (peer_rep, 1)), jnp.tile(xs_bf, (peer_rep, 1)), wq_bf,
                                                peer_sub_keys[li], u_bf, vt_bf, g2, b2, t_rep, t_rep))

    stack = lambda name, dtype: jnp.stack(outs[name]).astype(dtype)
    return (xp.reshape(bsz, seq, d), xs.reshape(dbs, dseq, d),
            stack("kp", cache_k.dtype), stack("vp", cache_v.dtype), stack("lp", cache_logf.dtype),
            stack("ks", cache_k.dtype), stack("vs", cache_v.dtype), stack("ls", cache_logf.dtype),
            stack("s5p_re", state_s5_re.dtype), stack("s5p_im", state_s5_im.dtype),
            stack("s5s_re", state_s5_re.dtype), stack("s5s_im", state_s5_im.dtype),
            jnp.stack(outs["gmv"]))
```

```python
import functools
import math

import jax
import jax.numpy as jnp
from jax import lax
from jax.experimental import pallas as pl
from jax.experimental.pallas import tpu as pltpu

F32 = jnp.float32
BF16 = jnp.bfloat16
HIGHEST = lax.Precision.HIGHEST

D_MODEL = 2048
DEPTH = 4
PAGE_SIZE = 128
S5_WIDTH = 1024
S5_GROUP = 16
S5_GROUPS = 64
S5_STATE = 64
S5_GROUP_BLOCK = 8
S5_BLOCKS = S5_GROUPS // S5_GROUP_BLOCK
S5_BLOCK_STATE = S5_GROUP_BLOCK * S5_STATE
GM_WIDTH = 1024
GM_CHUNK = 128
GM_HEADS = 8
GM_HEAD_DIM = 128
N_HEADS = 16
HEAD_DIM = 128
ATTN_SCALE = HEAD_DIM ** -0.5
TOKEN_TILE = 512
S5_CHUNK = 512
PEER_EXPERT_TILE = 1024
PEER_ROUTE_TILE = 512
FOX_BLOCK = 512
FOX_HEADS_PER_STEP = 8
FOX_PAGES_PER_STEP = 8
PEER_HEADS = 8
PEER_KEYS = 128
PEER_TOPK = 16
DN_ALPHA = (2.0 * DEPTH) ** 0.25
LN_EPS = 1e-5
SUBLANES = 8
LANES = 128
VMEM_LIMIT = 56 * 1024 * 1024
NEG_INF = float("-inf")


def _params(*sem):
    return pltpu.CompilerParams(dimension_semantics=sem, vmem_limit_bytes=VMEM_LIMIT)


def _gelu_tanh(x):
    c0 = math.sqrt(2.0 / math.pi)
    return (0.5 * x) * (1.0 + jnp.tanh(x * (c0 + (c0 * 0.044715) * (x * x))))


def _layer_norm(y, g, b):
    mu = jnp.mean(y, axis=-1, keepdims=True)
    yc = y - mu
    var = jnp.mean(yc * yc, axis=-1, keepdims=True)
    return yc * lax.rsqrt(var + LN_EPS) * g + b


def _mm_kernel(x_ref, w_ref, o_ref):
    o_ref[...] = jnp.dot(x_ref[...].astype(BF16), w_ref[...], preferred_element_type=F32)


def matmul(x, w_bf, col_block, n_out, tm):
    t, k = x.shape
    assert t % tm == 0
    return pl.pallas_call(
        _mm_kernel,
        grid=(t // tm,),
        in_specs=[pl.BlockSpec((tm, k), lambda i: (i, 0)),
                  pl.BlockSpec((k, n_out), lambda i: (0, col_block))],
        out_specs=pl.BlockSpec((tm, n_out), lambda i: (i, 0)),
        out_shape=jax.ShapeDtypeStruct((t, n_out), F32),
        compiler_params=_params("parallel"),
        name="matmul",
    )(x, w_bf)


def _store_norm(y, g_ref, beta_ref, o_ref, ob_ref):
    out = _layer_norm(y, g_ref[...], beta_ref[...])
    o_ref[...] = out
    ob_ref[...] = out.astype(BF16)


def _mm1_dnln_kernel(a_ref, w_ref, x_ref, g_ref, beta_ref, o_ref, ob_ref):
    m = jnp.dot(a_ref[...].astype(BF16), w_ref[...], preferred_element_type=F32)
    _store_norm(DN_ALPHA * x_ref[...] + m, g_ref, beta_ref, o_ref, ob_ref)


def _mm2_dnln_kernel(a_ref, b_ref, wa_ref, wb_ref, x_ref, g_ref, beta_ref, o_ref, ob_ref):
    m = jnp.dot(a_ref[...].astype(BF16), wa_ref[...], preferred_element_type=F32)
    m += jnp.dot(b_ref[...].astype(BF16), wb_ref[...], preferred_element_type=F32)
    _store_norm(DN_ALPHA * x_ref[...] + m, g_ref, beta_ref, o_ref, ob_ref)


def _norm_outputs(t, d, tm):
    row = lambda i: (i, 0)
    return ([pl.BlockSpec((tm, d), row), pl.BlockSpec((tm, d), row)],
            [jax.ShapeDtypeStruct((t, d), F32), jax.ShapeDtypeStruct((t, d), BF16)])


def out_proj_dnln(a, b, w_bf, x, g, beta, tm):
    t, ka = a.shape
    d = w_bf.shape[1]
    row = lambda i: (i, 0)
    fixed = lambda i: (0, 0)
    tail = [pl.BlockSpec((tm, d), row), pl.BlockSpec((1, d), fixed), pl.BlockSpec((1, d), fixed)]
    if b is None:
        kern = _mm1_dnln_kernel
        ins = [a, w_bf, x, g, beta]
        specs = [pl.BlockSpec((tm, ka), row), pl.BlockSpec((ka, d), fixed)] + tail
    else:
        assert b.shape[1] == ka
        kern = _mm2_dnln_kernel
        ins = [a, b, w_bf, w_bf, x, g, beta]
        specs = [pl.BlockSpec((tm, ka), row), pl.BlockSpec((tm, ka), row),
                 pl.BlockSpec((ka, d), fixed), pl.BlockSpec((ka, d), lambda i: (1, 0))] + tail
    out_specs, out_shape = _norm_outputs(t, d, tm)
    return pl.pallas_call(
        kern, grid=(t // tm,), in_specs=specs, out_specs=out_specs, out_shape=out_shape,
        compiler_params=_params("parallel"),
        name="out_proj_dnln",
    )(*ins)


def _split_bf16(x):
    hi = lax.bitcast_convert_type(lax.bitcast_convert_type(x, jnp.uint32) & jnp.uint32(0xFFFF0000), F32)
    return hi.astype(BF16), (x - hi).astype(BF16)


def _dot_split(a, b_ref):
    a_hi, a_lo = _split_bf16(a)
    dot = functools.partial(jnp.dot, preferred_element_type=F32)
    return dot(a_hi, b_ref[0]) + (dot(a_lo, b_ref[0]) + dot(a_hi, b_ref[1]))


def _s5_kernel(u_ref, bd_ref, cd_ref, tab_ref, d_ref, h0_ref, y_ref, hl_ref, hs_sc, carry_sc, *, lc):
    ns = S5_BLOCK_STATE

    @pl.when(pl.program_id(2) == 0)
    def _():
        carry_sc[...] = h0_ref[...]

    u = u_ref[...]
    hs_sc[...] = _dot_split(u, bd_ref)

    def row_group(r, carry):
        cr, ci = carry
        rows = pl.ds(pl.multiple_of(r * SUBLANES, SUBLANES), SUBLANES)
        xr = hs_sc[rows, :ns]
        xi = hs_sc[rows, ns:]
        for k in range(3):
            ar = tab_ref[2 * k]
            ai = tab_ref[2 * k + 1]
            sr = pltpu.roll(xr, 1 << k, 0)
            si = pltpu.roll(xi, 1 << k, 0)
            xr, xi = xr + (ar * sr - ai * si), xi + (ar * si + ai * sr)
        pr = tab_ref[6]
        pi = tab_ref[7]
        xr, xi = xr + (pr * cr - pi * ci), xi + (pr * ci + pi * cr)
        hs_sc[rows, :ns] = xr
        hs_sc[rows, ns:] = xi
        return xr[SUBLANES - 1:, :], xi[SUBLANES - 1:, :]

    cr, ci = lax.fori_loop(0, lc // SUBLANES, row_group, (carry_sc[:, :ns], carry_sc[:, ns:]))
    carry_sc[:, :ns] = cr
    carry_sc[:, ns:] = ci
    hl_ref[...] = carry_sc[...]
    y_ref[...] = _dot_split(hs_sc[...], cd_ref) + d_ref[...] * u


def _s5_tables(lam_re, lam_im, log_dt, b_re, b_im, c_re, c_im):
    dt = jnp.exp(log_dt.astype(F32))[:, None]
    mag = jnp.exp(lam_re * dt)
    lbr = mag * jnp.cos(lam_im * dt)
    lbi = mag * jnp.sin(lam_im * dt)
    den = lam_re * lam_re + lam_im * lam_im
    qr = ((lbr - 1.0) * lam_re + lbi * lam_im) / den
    qi = (lbi * lam_re - (lbr - 1.0) * lam_im) / den
    bbr = qr[..., None] * b_re - qi[..., None] * b_im
    bbi = qr[..., None] * b_im + qi[..., None] * b_re
    nb, gb = S5_BLOCKS, S5_GROUP_BLOCK
    eye = jnp.eye(gb, dtype=F32)

    def bdiag(m):
        m = m.reshape(nb, gb, S5_STATE, S5_GROUP)
        return jnp.einsum('ngpc,gh->ngchp', m, eye).reshape(nb, gb * S5_GROUP, gb * S5_STATE)

    def cdiag(m):
        m = m.reshape(nb, gb, S5_GROUP, S5_STATE)
        return jnp.einsum('ngcp,gh->ngphc', m, eye).reshape(nb, gb * S5_STATE, gb * S5_GROUP)

    bd = jnp.stack(_split_bf16(jnp.concatenate([bdiag(bbr), bdiag(bbi)], axis=-1)), axis=1)
    cd = jnp.stack(_split_bf16(jnp.concatenate([cdiag(c_re), cdiag(-c_im)], axis=1)), axis=1)

    def cmul(a, b):
        return a[0] * b[0] - a[1] * b[1], a[0] * b[1] + a[1] * b[0]

    powers = [(lbr, lbi)]
    for _ in range(SUBLANES - 1):
        powers.append(cmul(powers[-1], (lbr, lbi)))
    row = jnp.arange(SUBLANES)[:, None, None]
    tabs = []
    for k in range(3):
        sh = 1 << k
        for part in powers[sh - 1]:
            tabs.append(jnp.where(row >= sh, part[None], 0.0))
    tabs.append(jnp.stack([p[0] for p in powers]))
    tabs.append(jnp.stack([p[1] for p in powers]))
    tab = jnp.stack(tabs)
    tab = tab.reshape(8, SUBLANES, nb, S5_BLOCK_STATE).transpose(2, 0, 1, 3)
    return bd, cd, tab


def s5_scan(proj, h0_re, h0_im, tables, d_skip, lc):
    bd, cd, tab = tables
    bsz, seq = proj.shape[0], proj.shape[1]
    nb, ns = S5_BLOCKS, S5_BLOCK_STATE
    h0 = jnp.concatenate([h0_re.reshape(bsz, nb, 1, ns), h0_im.reshape(bsz, nb, 1, ns)], axis=-1)
    y, hl = pl.pallas_call(
        functools.partial(_s5_kernel, lc=lc),
        grid=(nb, bsz, seq // lc),
        in_specs=[pl.BlockSpec((None, lc, LANES), lambda g, b, l: (b, l, g)),
                  pl.BlockSpec((None, 2, LANES, 2 * ns), lambda g, b, l: (g, 0, 0, 0)),
                  pl.BlockSpec((None, 2, 2 * ns, LANES), lambda g, b, l: (g, 0, 0, 0)),
                  pl.BlockSpec((None, 8, SUBLANES, ns), lambda g, b, l: (g, 0, 0, 0)),
                  pl.BlockSpec((1, LANES), lambda g, b, l: (0, g)),
                  pl.BlockSpec((None, None, 1, 2 * ns), lambda g, b, l: (b, g, 0, 0))],
        out_specs=[pl.BlockSpec((None, lc, LANES), lambda g, b, l: (b, l, g)),
                   pl.BlockSpec((None, None, 1, 2 * ns), lambda g, b, l: (b, g, 0, 0))],
        out_shape=[jax.ShapeDtypeStruct((bsz, seq, S5_WIDTH), F32),
                   jax.ShapeDtypeStruct((bsz, nb, 1, 2 * ns), F32)],
        scratch_shapes=[pltpu.VMEM((lc, 2 * ns), F32), pltpu.VMEM((1, 2 * ns), F32)],
        compiler_params=_params("parallel", "parallel", "arbitrary"),
        name="s5_scan",
    )(proj, bd, cd, tab, d_skip.reshape(1, S5_WIDTH), h0)
    hl_re = hl[..., :ns].reshape(bsz, S5_GROUPS, S5_STATE)
    hl_im = hl[..., ns:].reshape(bsz, S5_GROUPS, S5_STATE)
    return y, hl_re, hl_im


def _glu_kernel(y_ref, w_ref, b_ref, o_ref):
    y = jax.nn.gelu(y_ref[...])
    z = jnp.dot(y.astype(BF16), w_ref[...], preferred_element_type=F32) + b_ref[...]
    o_ref[...] = y * jax.nn.sigmoid(z)


def s5_glu(y, w, b, tm):
    t, k = y.shape
    return pl.pallas_call(
        _glu_kernel, grid=(t // tm,),
        in_specs=[pl.BlockSpec((tm, k), lambda i: (i, 0)), pl.BlockSpec((k, k), lambda i: (0, 0)),
                  pl.BlockSpec((1, k), lambda i: (0, 0))],
        out_specs=pl.BlockSpec((tm, k), lambda i: (i, 0)),
        out_shape=jax.ShapeDtypeStruct((t, k), F32),
        compiler_params=_params("parallel"),
        name="s5_glu",
    )(y, w, b.reshape(1, k))


def _gmlp_kernel(zu_ref, zv_ref, g_ref, b_ref, w_ref, bs_ref, o_ref, v_ref, *, lc):
    u = jax.nn.gelu(zu_ref[...])
    v = _layer_norm(jax.nn.gelu(zv_ref[...]), g_ref[...], b_ref[...])
    v_ref[...] = v
    row = lax.broadcasted_iota(jnp.int32, (lc, lc), 0)
    col = lax.broadcasted_iota(jnp.int32, (lc, lc), 1)
    causal = row >= col
    vb = v.astype(BF16)
    bs = bs_ref[...]
    for h in range(GM_HEADS):
        cols = slice(h * GM_HEAD_DIM, (h + 1) * GM_HEAD_DIM)
        w = jnp.where(causal, w_ref[h], 0.0).astype(BF16)
        mixed = jnp.dot(w, vb[:, cols], preferred_element_type=F32) + bs[:, h:h + 1]
        o_ref[:, cols] = u[:, cols] * mixed


def chunk_gmlp(proj, ln_g, ln_b, w_s, b_s, lc):
    bsz, seq = proj.shape[0], proj.shape[1]
    w = w_s[:, :lc, :lc]
    bs = b_s[:, :lc].T
    fixed2 = lambda b, n: (0, 0)
    return pl.pallas_call(
        functools.partial(_gmlp_kernel, lc=lc),
        grid=(bsz, seq // lc),
        in_specs=[pl.BlockSpec((None, lc, GM_WIDTH), lambda b, n: (b, n, 1)),
                  pl.BlockSpec((None, lc, GM_WIDTH), lambda b, n: (b, n, 2)),
                  pl.BlockSpec((1, GM_WIDTH), fixed2), pl.BlockSpec((1, GM_WIDTH), fixed2),
                  pl.BlockSpec((GM_HEADS, lc, lc), lambda b, n: (0, 0, 0)),
                  pl.BlockSpec((lc, GM_HEADS), fixed2)],
        out_specs=[pl.BlockSpec((None, lc, GM_WIDTH), lambda b, n: (b, n, 0)),
                   pl.BlockSpec((None, lc, GM_WIDTH), lambda b, n: (b, n, 0))],
        out_shape=[jax.ShapeDtypeStruct((bsz, seq, GM_WIDTH), F32),
                   jax.ShapeDtypeStruct((bsz, seq, GM_WIDTH), F32)],
        compiler_params=_params("parallel", "parallel"),
        name="chunk_gmlp",
    )(proj, proj, ln_g.reshape(1, GM_WIDTH), ln_b.reshape(1, GM_WIDTH), w, bs)


def _logf_kernel(x_ref, w_ref, b_ref, o_ref):
    z = jnp.dot(x_ref[...].astype(BF16), w_ref[...].astype(BF16), preferred_element_type=F32)
    o_ref[...] = jax.nn.log_sigmoid(z + b_ref[...])


def forget_log_gate(x, w_f, b_f, tm):
    t, k = x.shape
    return pl.pallas_call(
        _logf_kernel, grid=(t // tm,),
        in_specs=[pl.BlockSpec((tm, k), lambda i: (i, 0)), pl.BlockSpec((k, N_HEADS), lambda i: (0, 0)),
                  pl.BlockSpec((1, N_HEADS), lambda i: (0, 0))],
        out_specs=pl.BlockSpec((tm, N_HEADS), lambda i: (i, 0)),
        out_shape=jax.ShapeDtypeStruct((t, N_HEADS), F32),
        compiler_params=_params("parallel"),
        name="forget_log_gate",
    )(x, w_f, b_f.reshape(1, N_HEADS))


def _fox_prompt_kernel(q_ref, k_ref, v_ref, fq_ref, fk_ref, o_ref, m_sc, l_sc, acc_sc, *, tq):
    hg = pl.program_id(1)
    qi = pl.program_id(2)
    ki = pl.program_id(3)

    @pl.when(ki == 0)
    def _():
        m_sc[...] = jnp.full_like(m_sc, NEG_INF)
        l_sc[...] = jnp.zeros_like(l_sc)
        acc_sc[...] = jnp.zeros_like(acc_sc)

    def block(on_diagonal):
        for hl in range(FOX_HEADS_PER_STEP):
            cols = slice(hl * HEAD_DIM, (hl + 1) * HEAD_DIM)
            q = (q_ref[:, cols] * ATTN_SCALE).astype(BF16)
            s = lax.dot_general(q, k_ref[:, cols].astype(BF16), (((1,), (1,)), ((), ())),
                                preferred_element_type=F32)
            head = lax.broadcasted_iota(jnp.int32, (tq, N_HEADS), 1) == hg * FOX_HEADS_PER_STEP + hl
            fq = jnp.sum(jnp.where(head, fq_ref[...], 0.0), axis=-1, keepdims=True)
            s = s + fq - fk_ref[hl]
            if on_diagonal:
                causal = (lax.broadcasted_iota(jnp.int32, (tq, tq), 0)
                          >= lax.broadcasted_iota(jnp.int32, (tq, tq), 1))
                s = jnp.where(causal, s, NEG_INF)
            m_old = m_sc[hl]
            m_new = jnp.maximum(m_old, jnp.max(s, axis=-1, keepdims=True))
            alpha = jnp.exp(m_old - m_new)
            p = jnp.exp(s - m_new)
            l_sc[hl] = alpha * l_sc[hl] + jnp.sum(p, axis=-1, keepdims=True)
            acc_sc[:, cols] = alpha * acc_sc[:, cols] + jnp.dot(p.astype(BF16), v_ref[:, cols].astype(BF16),
                                                                 preferred_element_type=F32)
            m_sc[hl] = m_new

    @pl.when(ki < qi)
    def _():
        block(False)

    @pl.when(ki == qi)
    def _():
        block(True)

    @pl.when(ki == pl.num_programs(3) - 1)
    def _():
        for hl in range(FOX_HEADS_PER_STEP):
            cols = slice(hl * HEAD_DIM, (hl + 1) * HEAD_DIM)
            o_ref[:, cols] = acc_sc[:, cols] / l_sc[hl]


def fox_prompt(q, k, v, f_cum, tq):
    bsz, seq = q.shape[0], q.shape[1]
    nq = seq // tq
    hps = FOX_HEADS_PER_STEP
    width = hps * HEAD_DIM
    f_rows = f_cum.transpose(0, 2, 1).reshape(bsz, N_HEADS, 1, seq)
    kv_map = lambda b, h, i, j: (b, jnp.minimum(j, i), h)
    return pl.pallas_call(
        functools.partial(_fox_prompt_kernel, tq=tq),
        grid=(bsz, N_HEADS // hps, nq, nq),
        in_specs=[pl.BlockSpec((None, tq, width), lambda b, h, i, j: (b, i, h)),
                  pl.BlockSpec((None, tq, width), kv_map),
                  pl.BlockSpec((None, tq, width), kv_map),
                  pl.BlockSpec((None, tq, N_HEADS), lambda b, h, i, j: (b, i, 0)),
                  pl.BlockSpec((None, hps, 1, tq), lambda b, h, i, j: (b, h, 0, jnp.minimum(j, i)))],
        out_specs=pl.BlockSpec((None, tq, width), lambda b, h, i, j: (b, i, h)),
        out_shape=jax.ShapeDtypeStruct((bsz, seq, D_MODEL), F32),
        scratch_shapes=[pltpu.VMEM((hps, tq, 1), F32), pltpu.VMEM((hps, tq, 1), F32),
                        pltpu.VMEM((tq, width), F32)],
        compiler_params=_params("parallel", "parallel", "parallel", "arbitrary"),
        name="fox_prompt",
    )(q, k, v, f_cum, f_rows)


def _fox_sample_kernel(pt_ref, q_ref, kn_ref, vn_ref, gc_ref, gr_ref, *rest, n_new, pps):
    kp_refs, vp_refs, lp_refs = rest[:pps], rest[pps:2 * pps], rest[2 * pps:3 * pps]
    o_ref, m_sc, l_sc, acc_sc, carry_sc = rest[3 * pps:]
    step = pl.program_id(1)
    nt = (((1,), (1,)), ((), ()))
    head_rows = lambda h: slice(h * n_new, (h + 1) * n_new)
    head_cols = lambda h: slice(h * HEAD_DIM, (h + 1) * HEAD_DIM)
    q_heads = [(q_ref[:, head_cols(h)] * ATTN_SCALE).astype(BF16) for h in range(N_HEADS)]

    def online_update(s, values):
        m_old = m_sc[...]
        m_new = jnp.maximum(m_old, jnp.max(s, axis=-1, keepdims=True))
        alpha = jnp.exp(m_old - m_new)
        p = jnp.exp(s - m_new)
        l_sc[...] = alpha * l_sc[...] + jnp.sum(p, axis=-1, keepdims=True)
        pv = [jnp.dot(p[head_rows(h), :].astype(BF16), values(h), preferred_element_type=F32)
              for h in range(N_HEADS)]
        acc_sc[...] = alpha * acc_sc[...] + jnp.concatenate(pv, axis=0)
        m_sc[...] = m_new

    @pl.when(step == 0)
    def _():
        m_sc[...] = jnp.full_like(m_sc, NEG_INF)
        l_sc[...] = jnp.zeros_like(l_sc)
        acc_sc[...] = jnp.zeros_like(acc_sc)
        carry_sc[...] = jnp.zeros_like(carry_sc)
        s = jnp.concatenate([lax.dot_general(q_heads[h], kn_ref[:, head_cols(h)].astype(BF16), nt,
                                             preferred_element_type=F32) for h in range(N_HEADS)], axis=0)
        s = s + gc_ref[...] - gr_ref[...]
        rows = N_HEADS * n_new
        causal = (lax.broadcasted_iota(jnp.int32, (rows, n_new), 0) % n_new
                  >= lax.broadcasted_iota(jnp.int32, (rows, n_new), 1))
        online_update(jnp.where(causal, s, NEG_INF), lambda h: vn_ref[:, head_cols(h)].astype(BF16))

    @pl.when(step > 0)
    def _():
        later = (lax.broadcasted_iota(jnp.int32, (PAGE_SIZE, PAGE_SIZE), 0)
                 > lax.broadcasted_iota(jnp.int32, (PAGE_SIZE, PAGE_SIZE), 1)).astype(F32)
        carry = carry_sc[...]
        page_scores = []
        for pg in range(pps):
            lp = lp_refs[pg][...]
            suffix = jnp.dot(lp, later, precision=HIGHEST, preferred_element_type=F32) + carry
            carry = carry + jnp.sum(lp, axis=-1, keepdims=True)
            qk = [lax.dot_general(q_heads[h], kp_refs[pg][pl.ds(h, PAGE_SIZE, stride=N_HEADS), :].astype(BF16),
                                  nt, preferred_element_type=F32)
                  + jnp.broadcast_to(suffix[h:h + 1, :], (n_new, PAGE_SIZE)) for h in range(N_HEADS)]
            page_scores.append(jnp.concatenate(qk, axis=0))
        carry_sc[...] = carry
        s = jnp.concatenate(page_scores, axis=1) + gc_ref[...]

        def values(h):
            return jnp.concatenate([vp_refs[pg][pl.ds(h, PAGE_SIZE, stride=N_HEADS), :].astype(BF16)
                                    for pg in range(pps)], axis=0)

        online_update(s, values)

    @pl.when(step == pl.num_programs(1) - 1)
    def _():
        out = acc_sc[...] / l_sc[...]
        for h in range(N_HEADS):
            o_ref[:, head_cols(h)] = out[head_rows(h), :]


def fox_sample(q, k, v, g_new, cache_k, cache_v, cache_logf_t, page_table, layer):
    dbs, n_new = q.shape[0], q.shape[1]
    n_pages = page_table.shape[1]
    pps = math.gcd(n_pages, FOX_PAGES_PER_STEP)
    rows = N_HEADS * n_new
    g_t = g_new.transpose(0, 2, 1)
    g_col = g_t.reshape(dbs, rows, 1)
    g_row = jnp.repeat(g_t, n_new, axis=1)

    def page_map(pg):
        def index(b, s, pt):
            first = jnp.maximum(s - 1, 0) * pps
            return (layer, pt[b, n_pages - 1 - (first + pg)], 0, 0)
        return index

    seq_map = lambda b, s, pt: (b, 0, 0)
    page_specs = lambda shape: [pl.BlockSpec((None, None) + shape, page_map(pg)) for pg in range(pps)]
    grid_spec = pltpu.PrefetchScalarGridSpec(
        num_scalar_prefetch=1,
        grid=(dbs, n_pages // pps + 1),
        in_specs=([pl.BlockSpec((None, n_new, D_MODEL), seq_map)] * 3
                  + [pl.BlockSpec((None, rows, 1), seq_map), pl.BlockSpec((None, rows, n_new), seq_map)]
                  + page_specs((PAGE_SIZE * N_HEADS, HEAD_DIM)) * 2
                  + page_specs((N_HEADS, PAGE_SIZE))),
        out_specs=pl.BlockSpec((None, n_new, D_MODEL), seq_map),
        scratch_shapes=[pltpu.VMEM((rows, 1), F32), pltpu.VMEM((rows, 1), F32),
                        pltpu.VMEM((rows, HEAD_DIM), F32), pltpu.VMEM((N_HEADS, 1), F32)],
    )
    return pl.pallas_call(
        functools.partial(_fox_sample_kernel, n_new=n_new, pps=pps),
        grid_spec=grid_spec,
        out_shape=jax.ShapeDtypeStruct((dbs, n_new, D_MODEL), F32),
        compiler_params=_params("parallel", "arbitrary"),
        name="fox_sample",
    )(page_table, q, k, v, g_col, g_row, *([cache_k] * pps), *([cache_v] * pps), *([cache_logf_t] * pps))


def _take_top(s, count):
    return _take_top_ranked(s, count, with_rank=False)[0]


def _take_top_ranked(s, count, with_rank=True):
    tops = []
    rank = jnp.full_like(s, float(count)) if with_rank else None
    for i in range(count):
        m = jnp.max(s, axis=0, keepdims=True)
        tops.append(m)
        if with_rank or i + 1 < count:
            hit = s == m
            if with_rank:
                rank = jnp.where(hit, float(i), rank)
            if i + 1 < count:
                s = jnp.where(hit, NEG_INF, s)
    return tops, rank


def _candidate_sums(top1, top2):
    limit = PEER_TOPK + 1
    t2_head = jnp.concatenate(top2[:SUBLANES], axis=0)
    row = lax.broadcasted_iota(jnp.int32, t2_head.shape, 0)
    pieces = [top1[0] + jnp.concatenate(top2, axis=0)]
    for i in range(1, SUBLANES):
        pieces.append(jnp.where(row < limit // (i + 1), top1[i] + t2_head, NEG_INF))
    assert limit // (SUBLANES + 1) == 1
    pieces.append(jnp.concatenate(top1[SUBLANES:], axis=0) + top2[0])
    return jnp.concatenate(pieces, axis=0)


def _peer_route_kernel(q_ref, keys_ref, r2_ref, e2_ref, n2_ref, e1_ref, *, n_sub):
    nt = (((1,), (1,)), ((), ()))
    for sub in range(n_sub):
        tok = slice(sub * LANES, (sub + 1) * LANES) if n_sub > 1 else slice(None)
        q = q_ref[tok, :].astype(BF16)
        s1 = lax.dot_general(keys_ref[0].astype(BF16), q[:, :PEER_KEYS], nt, preferred_element_type=F32)
        s2 = lax.dot_general(keys_ref[1].astype(BF16), q[:, PEER_KEYS:], nt, preferred_element_type=F32)
        top1 = _take_top(s1, PEER_TOPK)
        top2, r2 = _take_top_ranked(s2, PEER_TOPK)
        best = _take_top(_candidate_sums(top1, top2), PEER_TOPK + 1)
        z = jnp.ones_like(best[0])
        for m in best[1:PEER_TOPK]:
            z += jnp.exp(m - best[0])
        cut = 0.5 * (best[PEER_TOPK - 1] + best[PEER_TOPK])
        n2 = jnp.zeros_like(s1)
        for t2 in top2:
            n2 += jnp.where(s1 + t2 > cut, 1.0, 0.0)
        r2_ref[:, tok] = r2.astype(BF16)
        e2_ref[:, tok] = jnp.where(s2 >= top2[PEER_TOPK - 1], jnp.exp(s2 - top2[0]), 0.0).astype(BF16)
        n2_ref[:, tok] = n2
        e1_ref[:, tok] = jnp.where(s1 >= top1[PEER_TOPK - 1], jnp.exp(s1 - top1[0]) / z, 0.0)


def peer_route(q, sub_keys, tm):
    t = q.shape[0]
    n_sub = max(tm // LANES, 1)
    out = lambda dtype: jax.ShapeDtypeStruct((PEER_HEADS, PEER_KEYS, t), dtype)
    ospec = pl.BlockSpec((None, PEER_KEYS, tm), lambda i, h: (h, 0, i))
    return pl.pallas_call(
        functools.partial(_peer_route_kernel, n_sub=n_sub),
        grid=(t // tm, PEER_HEADS),
        in_specs=[pl.BlockSpec((tm, 2 * PEER_KEYS), lambda i, h: (i, h)),
                  pl.BlockSpec((None, 2, PEER_KEYS, PEER_KEYS), lambda i, h: (h, 0, 0, 0))],
        out_specs=[ospec, ospec, ospec, ospec],
        out_shape=[out(BF16), out(BF16), out(F32), out(F32)],
        compiler_params=_params("parallel", "parallel"),
        name="peer_route",
    )(q, sub_keys)


def _peer_dense_kernel(x_ref, u_ref, vt_ref, r2_ref, e2_ref, n2_ref, e1_ref, res_ref, ln_g_ref, ln_b_ref,
                       o_ref, ob_ref, acc_sc, z_sc, g_sc, *, te, tm):
    k = pl.program_id(1)
    a_per_tile = te // PEER_KEYS
    rows_per_chunk = 2 * SUBLANES
    lanes_per_chunk = min(tm, 2 * LANES)
    n_chunks = PEER_KEYS // rows_per_chunk
    chunk_shape = (rows_per_chunk, lanes_per_chunk)

    @pl.when(k == 0)
    def _():
        acc_sc[...] = jnp.zeros_like(acc_sc)

    z_sc[...] = lax.dot_general(u_ref[...], x_ref[...], (((1,), (1,)), ((), ())), preferred_element_type=F32)
    for al in range(a_per_tile):
        for t0 in range(0, tm, lanes_per_chunk):
            tok = slice(t0, t0 + lanes_per_chunk)
            gates = [jnp.zeros(chunk_shape, BF16) for _ in range(n_chunks)]
            for h in range(PEER_HEADS):
                n2 = jnp.broadcast_to(n2_ref[h, al:al + 1, tok], chunk_shape).astype(BF16)
                e1 = jnp.broadcast_to(e1_ref[h, al:al + 1, tok], chunk_shape).astype(BF16)
                for c in range(n_chunks):
                    rows = slice(c * rows_per_chunk, (c + 1) * rows_per_chunk)
                    keep = r2_ref[h, rows, tok] < n2
                    gates[c] += jnp.where(keep, e2_ref[h, rows, tok] * e1, jnp.zeros(chunk_shape, BF16))
            for c in range(n_chunks):
                zrows = slice(al * PEER_KEYS + c * rows_per_chunk, al * PEER_KEYS + (c + 1) * rows_per_chunk)
                g_sc[zrows, tok] = gates[c] * _gelu_tanh(z_sc[zrows, tok].astype(BF16))
    acc_sc[...] += jnp.dot(vt_ref[...], g_sc[...], preferred_element_type=F32)

    @pl.when(k == pl.num_programs(1) - 1)
    def _():
        _store_norm(DN_ALPHA * res_ref[...] + acc_sc[...].T, ln_g_ref, ln_b_ref, o_ref, ob_ref)


def peer_dense(x, x_bf, u_bf, vt_bf, routes, ln_g, ln_b, tm, te):
    t, d = x_bf.shape
    n_tiles = vt_bf.shape[0]
    assert vt_bf.shape == (n_tiles, d, te)
    r2, e2, n2, e1 = routes
    a_per_tile = te // PEER_KEYS
    n2, e1 = (r.reshape(PEER_HEADS, n_tiles, a_per_tile, t) for r in (n2, e1))
    bspec = pl.BlockSpec((PEER_HEADS, PEER_KEYS, tm), lambda i, k: (0, 0, i))
    aspec = pl.BlockSpec((PEER_HEADS, None, a_per_tile, tm), lambda i, k: (0, k, 0, i))
    once_per_tile = pl.Buffered(1)
    return pl.pallas_call(
        functools.partial(_peer_dense_kernel, te=te, tm=tm),
        grid=(t // tm, n_tiles),
        in_specs=[pl.BlockSpec((tm, d), lambda i, k: (i, 0), pipeline_mode=once_per_tile),
                  pl.BlockSpec((te, d), lambda i, k: (k, 0)),
                  pl.BlockSpec((None, d, te), lambda i, k: (k, 0, 0)),
                  bspec, bspec, aspec, aspec,
                  pl.BlockSpec((tm, d), lambda i, k: (i, 0), pipeline_mode=once_per_tile),
                  pl.BlockSpec((1, d), lambda i, k: (0, 0)), pl.BlockSpec((1, d), lambda i, k: (0, 0))],
        out_specs=[pl.BlockSpec((tm, d), lambda i, k: (i, 0)), pl.BlockSpec((tm, d), lambda i, k: (i, 0))],
        out_shape=[jax.ShapeDtypeStruct((t, d), F32), jax.ShapeDtypeStruct((t, d), BF16)],
        scratch_shapes=[pltpu.VMEM((d, tm), F32), pltpu.VMEM((te, tm), F32), pltpu.VMEM((te, tm), BF16)],
        compiler_params=_params("parallel", "arbitrary"),
        name="peer_dense",
    )(x_bf, u_bf, vt_bf, r2, e2, n2, e1, x, ln_g, ln_b)


def peer_block(x, x_bf, wq_bf, sub_keys, u_bf, vt_bf, ln_g, ln_b, tm, tm_route):
    q = matmul(x_bf, wq_bf, 0, wq_bf.shape[1], tm)
    routes = peer_route(q, sub_keys, tm_route)
    return peer_dense(x, x_bf, u_bf, vt_bf, routes, ln_g, ln_b, tm, PEER_EXPERT_TILE)


def _row_tile(t):
    tm = min(t, TOKEN_TILE)
    assert t % tm == 0
    return tm


def kernel(x_prompt, x_sample, cache_k, cache_v, cache_logf, page_table, state_s5_re, state_s5_im, w_in_even, s5_lambda_re, s5_lambda_im, s5_log_dt, s5_b_re, s5_b_im, s5_c_re, s5_c_im, s5_d, s5_w_glu, s5_b_glu, gm_ln_g, gm_ln_b, gm_w_s, gm_b_s, w_out_even, w_in_odd, b_f, w_out_odd, ln1_g, ln1_b, ln2_g, ln2_b, peer_w_q, peer_sub_keys, peer_u, peer_v):
    bsz, seq, d = x_prompt.shape
    dbs, dseq, _ = x_sample.shape
    n_odd, n_phys = cache_k.shape[0], cache_k.shape[1]
    tp, ts = bsz * seq, dbs * dseq
    xp = x_prompt.reshape(tp, d)
    xs = x_sample.reshape(ts, d)
    xp_bf, xs_bf = xp.astype(BF16), xs.astype(BF16)
    tm_p = _row_tile(tp)
    tm_s = _row_tile(ts)
    ck = cache_k.reshape(n_odd, n_phys, PAGE_SIZE * N_HEADS, HEAD_DIM)
    cv = cache_v.reshape(n_odd, n_phys, PAGE_SIZE * N_HEADS, HEAD_DIM)
    clf = cache_logf.transpose(0, 1, 3, 2)
    zero_state = jnp.zeros((bsz, S5_GROUPS, S5_STATE), F32)
    peer_rep = max(LANES // ts, 1)

    outs = {k: [] for k in ("kp", "vp", "lp", "ks", "vs", "ls", "s5p_re", "s5p_im", "s5s_re", "s5s_im", "gmv")}
    for li in range(DEPTH):
        j = li // 2
        g1, b1 = ln1_g[li].reshape(1, d), ln1_b[li].reshape(1, d)
        g2, b2 = ln2_g[li].reshape(1, d), ln2_b[li].reshape(1, d)
        if li % 2 == 0:
            tables = _s5_tables(s5_lambda_re[j], s5_lambda_im[j], s5_log_dt[j], s5_b_re[j], s5_b_im[j],
                                s5_c_re[j], s5_c_im[j])
            n_in = w_in_even.shape[2]
            w_in, w_glu, w_out = (w.astype(BF16) for w in (w_in_even[j], s5_w_glu[j], w_out_even[j]))

            def even(x, x_bf, nb, nl, h0_re, h0_im, tm):
                proj = matmul(x_bf, w_in, 0, n_in, tm).reshape(nb, nl, n_in)
                y, hl_re, hl_im = s5_scan(proj, h0_re, h0_im, tables, s5_d[j], min(nl, S5_CHUNK))
                a_out = s5_glu(y.reshape(nb * nl, S5_WIDTH), w_glu, s5_b_glu[j], tm)
                g_out, v_rows = chunk_gmlp(proj, gm_ln_g[j], gm_ln_b[j], gm_w_s[j], gm_b_s[j], min(nl, GM_CHUNK))
                x, x_bf = out_proj_dnln(a_out, g_out.reshape(nb * nl, GM_WIDTH), w_out, x, g1, b1, tm)
                return x, x_bf, hl_re, hl_im, v_rows

            xp, xp_bf, hp_re, hp_im, _ = even(xp, xp_bf, bsz, seq, zero_state, zero_state, tm_p)
            xs, xs_bf, hs_re, hs_im, v_rows = even(xs, xs_bf, dbs, dseq, state_s5_re[j].astype(F32),
                                                   state_s5_im[j].astype(F32), tm_s)
            outs["s5p_re"].append(hp_re)
            outs["s5p_im"].append(hp_im)
            outs["s5s_re"].append(hs_re)
            outs["s5s_im"].append(hs_im)
            outs["gmv"].append(v_rows)
        else:
            w_f = w_in_odd[j][:, 3 * d:]
            w_qkv, w_out = w_in_odd[j].astype(BF16), w_out_odd[j].astype(BF16)

            def project(x_bf, tm):
                q, k, v = (matmul(x_bf, w_qkv, part, d, tm) for part in range(3))
                return q, k, v, forget_log_gate(x_bf, w_f, b_f[j], tm)

            qp, kp, vp, lfp = project(xp_bf, tm_p)
            f_cum = jnp.cumsum(lfp.reshape(bsz, seq, N_HEADS), axis=1)
            ap = fox_prompt(qp.reshape(bsz, seq, d), kp.reshape(bsz, seq, d), vp.reshape(bsz, seq, d), f_cum,
                            min(seq, FOX_BLOCK))
            xp, xp_bf = out_proj_dnln(ap.reshape(tp, d), None, w_out, xp, g1, b1, tm_p)
            qs, kss, vss, lfs = project(xs_bf, tm_s)
            g_new = jnp.cumsum(lfs.reshape(dbs, dseq, N_HEADS), axis=1)
            a_s = fox_sample(qs.reshape(dbs, dseq, d), kss.reshape(dbs, dseq, d), vss.reshape(dbs, dseq, d),
                             g_new, ck, cv, clf, page_table, j)
            xs, xs_bf = out_proj_dnln(a_s.reshape(ts, d), None, w_out, xs, g1, b1, tm_s)
            outs["kp"].append(kp.reshape(bsz, seq, N_HEADS, HEAD_DIM))
            outs["vp"].append(vp.reshape(bsz, seq, N_HEADS, HEAD_DIM))
            outs["lp"].append(lfp.reshape(bsz, seq, N_HEADS))
            outs["ks"].append(kss.reshape(dbs, dseq, N_HEADS, HEAD_DIM))
            outs["vs"].append(vss.reshape(dbs, dseq, N_HEADS, HEAD_DIM))
            outs["ls"].append(lfs.reshape(dbs, dseq, N_HEADS))
        u_bf = peer_u[li].astype(BF16)
        vt_bf = peer_v[li].reshape(-1, PEER_EXPERT_TILE, d).transpose(0, 2, 1).astype(BF16)
        wq_bf = peer_w_q[li].astype(BF16)
        xp, xp_bf = peer_block(xp, xp_bf, wq_bf, peer_sub_keys[li], u_bf, vt_bf, g2, b2, tm_p,
                               min(tm_p, PEER_ROUTE_TILE))
        t_rep = peer_rep * ts
        xs, xs_bf = (r[:ts] for r in peer_block(jnp.tile(xs, (peer_rep, 1)), jnp.tile(xs_bf, (peer_rep, 1)), wq_bf,
                                                peer_sub_keys[li], u_bf, vt_bf, g2, b2, t_rep, t_rep))

    stack = lambda name, dtype: jnp.stack(outs[name]).astype(dtype)
    return (xp.reshape(bsz, seq, d), xs.reshape(dbs, dseq, d),
            stack("kp", cache_k.dtype), stack("vp", cache_v.dtype), stack("lp", cache_logf.dtype),
            stack("ks", cache_k.dtype), stack("vs", cache_v.dtype), stack("ls", cache_logf.dtype),
            stack("s5p_re", state_s5_re.dtype), stack("s5p_im", state_s5_im.dtype),
            stack("s5s_re", state_s5_re.dtype), stack("s5s_im", state_s5_im.dtype),
            jnp.stack(outs["gmv"]))
```

```python
import functools
import math

import jax
import jax.numpy as jnp
from jax import lax
from jax.experimental import pallas as pl
from jax.experimental.pallas import tpu as pltpu

F32 = jnp.float32
BF16 = jnp.bfloat16
HIGHEST = lax.Precision.HIGHEST

D_MODEL = 2048
DEPTH = 4
PAGE_SIZE = 128
S5_WIDTH = 1024
S5_GROUP = 16
S5_GROUPS = 64
S5_STATE = 64
S5_GROUP_BLOCK = 8
S5_BLOCKS = S5_GROUPS // S5_GROUP_BLOCK
S5_BLOCK_STATE = S5_GROUP_BLOCK * S5_STATE
GM_WIDTH = 1024
GM_CHUNK = 128
GM_HEADS = 8
GM_HEAD_DIM = 128
N_HEADS = 16
HEAD_DIM = 128
ATTN_SCALE = HEAD_DIM ** -0.5
TOKEN_TILE = 512
S5_CHUNK = 512
PEER_EXPERT_TILE = 1024
PEER_ROUTE_TILE = 512
FOX_BLOCK = 512
FOX_HEADS_PER_STEP = 8
FOX_PAGES_PER_STEP = 8
PEER_HEADS = 8
PEER_KEYS = 128
PEER_TOPK = 16
DN_ALPHA = (2.0 * DEPTH) ** 0.25
LN_EPS = 1e-5
SUBLANES = 8
LANES = 128
VMEM_LIMIT = 56 * 1024 * 1024
NEG_INF = float("-inf")


def _params(*sem):
    return pltpu.CompilerParams(dimension_semantics=sem, vmem_limit_bytes=VMEM_LIMIT)


def _gelu_tanh(x):
    c0 = math.sqrt(2.0 / math.pi)
    return (0.5 * x) * (1.0 + jnp.tanh(x * (c0 + (c0 * 0.044715) * (x * x))))


def _layer_norm(y, g, b):
    mu = jnp.mean(y, axis=-1, keepdims=True)
    yc = y - mu
    var = jnp.mean(yc * yc, axis=-1, keepdims=True)
    return yc * lax.rsqrt(var + LN_EPS) * g + b


def _mm_kernel(x_ref, w_ref, o_ref):
    o_ref[...] = jnp.dot(x_ref[...].astype(BF16), w_ref[...], preferred_element_type=F32)


def matmul(x, w_bf, col_block, n_out, tm):
    t, k = x.shape
    assert t % tm == 0
    return pl.pallas_call(
        _mm_kernel,
        grid=(t // tm,),
        in_specs=[pl.BlockSpec((tm, k), lambda i: (i, 0)),
                  pl.BlockSpec((k, n_out), lambda i: (0, col_block))],
        out_specs=pl.BlockSpec((tm, n_out), lambda i: (i, 0)),
        out_shape=jax.ShapeDtypeStruct((t, n_out), F32),
        compiler_params=_params("parallel"),
        name="matmul",
    )(x, w_bf)


def _store_norm(y, g_ref, beta_ref, o_ref, ob_ref):
    out = _layer_norm(y, g_ref[...], beta_ref[...])
    o_ref[...] = out
    ob_ref[...] = out.astype(BF16)


def _mm1_dnln_kernel(a_ref, w_ref, x_ref, g_ref, beta_ref, o_ref, ob_ref):
    m = jnp.dot(a_ref[...].astype(BF16), w_ref[...], preferred_element_type=F32)
    _store_norm(DN_ALPHA * x_ref[...] + m, g_ref, beta_ref, o_ref, ob_ref)


def _mm2_dnln_kernel(a_ref, b_ref, wa_ref, wb_ref, x_ref, g_ref, beta_ref, o_ref, ob_ref):
    m = jnp.dot(a_ref[...].astype(BF16), wa_ref[...], preferred_element_type=F32)
    m += jnp.dot(b_ref[...].astype(BF16), wb_ref[...], preferred_element_type=F32)
    _store_norm(DN_ALPHA * x_ref[...] + m, g_ref, beta_ref, o_ref, ob_ref)


def _norm_outputs(t, d, tm):
    row = lambda i: (i, 0)
    return ([pl.BlockSpec((tm, d), row), pl.BlockSpec((tm, d), row)],
            [jax.ShapeDtypeStruct((t, d), F32), jax.ShapeDtypeStruct((t, d), BF16)])


def out_proj_dnln(a, b, w_bf, x, g, beta, tm):
    t, ka = a.shape
    d = w_bf.shape[1]
    row = lambda i: (i, 0)
    fixed = lambda i: (0, 0)
    tail = [pl.BlockSpec((tm, d), row), pl.BlockSpec((1, d), fixed), pl.BlockSpec((1, d), fixed)]
    if b is None:
        kern = _mm1_dnln_kernel
        ins = [a, w_bf, x, g, beta]
        specs = [pl.BlockSpec((tm, ka), row), pl.BlockSpec((ka, d), fixed)] + tail
    else:
        assert b.shape[1] == ka
        kern = _mm2_dnln_kernel
        ins = [a, b, w_bf, w_bf, x, g, beta]
        specs = [pl.BlockSpec((tm, ka), row), pl.BlockSpec((tm, ka), row),
                 pl.BlockSpec((ka, d), fixed), pl.BlockSpec((ka, d), lambda i: (1, 0))] + tail
    out_specs, out_shape = _norm_outputs(t, d, tm)
    return pl.pallas_call(
        kern, grid=(t // tm,), in_specs=specs, out_specs=out_specs, out_shape=out_shape,
        compiler_params=_params("parallel"),
        name="out_proj_dnln",
    )(*ins)


def _split_bf16(x):
    hi = lax.bitcast_convert_type(lax.bitcast_convert_type(x, jnp.uint32) & jnp.uint32(0xFFFF0000), F32)
    return hi.astype(BF16), (x - hi).astype(BF16)


def _dot_split(a, b_ref):
    a_hi, a_lo = _split_bf16(a)
    dot = functools.partial(jnp.dot, preferred_element_type=F32)
    return dot(a_hi, b_ref[0]) + (dot(a_lo, b_ref[0]) + dot(a_hi, b_ref[1]))


def _s5_kernel(u_ref, bd_ref, cd_ref, tab_ref, d_ref, h0_ref, y_ref, hl_ref, hs_sc, carry_sc, *, lc):
    ns = S5_BLOCK_STATE

    @pl.when(pl.program_id(2) == 0)
    def _():
        carry_sc[...] = h0_ref[...]

    u = u_ref[...]
    hs_sc[...] = _dot_split(u, bd_ref)

    def row_group(r, carry):
        cr, ci = carry
        rows = pl.ds(pl.multiple_of(r * SUBLANES, SUBLANES), SUBLANES)
        xr = hs_sc[rows, :ns]
        xi = hs_sc[rows, ns:]
        for k in range(3):
            ar = tab_ref[2 * k]
            ai = tab_ref[2 * k + 1]
            sr = pltpu.roll(xr, 1 << k, 0)
            si = pltpu.roll(xi, 1 << k, 0)
            xr, xi = xr + (ar * sr - ai * si), xi + (ar * si + ai * sr)
        pr = tab_ref[6]
        pi = tab_ref[7]
        xr, xi = xr + (pr * cr - pi * ci), xi + (pr * ci + pi * cr)
        hs_sc[rows, :ns] = xr
        hs_sc[rows, ns:] = xi
        return xr[SUBLANES - 1:, :], xi[SUBLANES - 1:, :]

    cr, ci = lax.fori_loop(0, lc // SUBLANES, row_group, (carry_sc[:, :ns], carry_sc[:, ns:]))
    carry_sc[:, :ns] = cr
    carry_sc[:, ns:] = ci
    hl_ref[...] = carry_sc[...]
    y_ref[...] = _dot_split(hs_sc[...], cd_ref) + d_ref[...] * u


def _s5_tables(lam_re, lam_im, log_dt, b_re, b_im, c_re, c_im):
    dt = jnp.exp(log_dt.astype(F32))[:, None]
    mag = jnp.exp(lam_re * dt)
    lbr = mag * jnp.cos(lam_im * dt)
    lbi = mag * jnp.sin(lam_im * dt)
    den = lam_re * lam_re + lam_im * lam_im
    qr = ((lbr - 1.0) * lam_re + lbi * lam_im) / den
    qi = (lbi * lam_re - (lbr - 1.0) * lam_im) / den
    bbr = qr[..., None] * b_re - qi[..., None] * b_im
    bbi = qr[..., None] * b_im + qi[..., None] * b_re
    nb, gb = S5_BLOCKS, S5_GROUP_BLOCK
    eye = jnp.eye(gb, dtype=F32)

    def bdiag(m):
        m = m.reshape(nb, gb, S5_STATE, S5_GROUP)
        return jnp.einsum('ngpc,gh->ngchp', m, eye).reshape(nb, gb * S5_GROUP, gb * S5_STATE)

    def cdiag(m):
        m = m.reshape(nb, gb, S5_GROUP, S5_STATE)
        return jnp.einsum('ngcp,gh->ngphc', m, eye).reshape(nb, gb * S5_STATE, gb * S5_GROUP)

    bd = jnp.stack(_split_bf16(jnp.concatenate([bdiag(bbr), bdiag(bbi)], axis=-1)), axis=1)
    cd = jnp.stack(_split_bf16(jnp.concatenate([cdiag(c_re), cdiag(-c_im)], axis=1)), axis=1)

    def cmul(a, b):
        return a[0] * b[0] - a[1] * b[1], a[0] * b[1] + a[1] * b[0]

    powers = [(lbr, lbi)]
    for _ in range(SUBLANES - 1):
        powers.append(cmul(powers[-1], (lbr, lbi)))
    row = jnp.arange(SUBLANES)[:, None, None]
    tabs = []
    for k in range(3):
        sh = 1 << k
        for part in powers[sh - 1]:
            tabs.append(jnp.where(row >= sh, part[None], 0.0))
    tabs.append(jnp.stack([p[0] for p in powers]))
    tabs.append(jnp.stack([p[1] for p in powers]))
    tab = jnp.stack(tabs)
    tab = tab.reshape(8, SUBLANES, nb, S5_BLOCK_STATE).transpose(2, 0, 1, 3)
    return bd, cd, tab


def s5_scan(proj, h0_re, h0_im, tables, d_skip, lc):
    bd, cd, tab = tables
    bsz, seq = proj.shape[0], proj.shape[1]
    nb, ns = S5_BLOCKS, S5_BLOCK_STATE
    h0 = jnp.concatenate([h0_re.reshape(bsz, nb, 1, ns), h0_im.reshape(bsz, nb, 1, ns)], axis=-1)
    y, hl = pl.pallas_call(
        functools.partial(_s5_kernel, lc=lc),
        grid=(nb, bsz, seq // lc),
        in_specs=[pl.BlockSpec((None, lc, LANES), lambda g, b, l: (b, l, g)),
                  pl.BlockSpec((None, 2, LANES, 2 * ns), lambda g, b, l: (g, 0, 0, 0)),
                  pl.BlockSpec((None, 2, 2 * ns, LANES), lambda g, b, l: (g, 0, 0, 0)),
                  pl.BlockSpec((None, 8, SUBLANES, ns), lambda g, b, l: (g, 0, 0, 0)),
                  pl.BlockSpec((1, LANES), lambda g, b, l: (0, g)),
                  pl.BlockSpec((None, None, 1, 2 * ns), lambda g, b, l: (b, g, 0, 0))],
        out_specs=[pl.BlockSpec((None, lc, LANES), lambda g, b, l: (b, l, g)),
                   pl.BlockSpec((None, None, 1, 2 * ns), lambda g, b, l: (b, g, 0, 0))],
        out_shape=[jax.ShapeDtypeStruct((bsz, seq, S5_WIDTH), F32),
                   jax.ShapeDtypeStruct((bsz, nb, 1, 2 * ns), F32)],
        scratch_shapes=[pltpu.VMEM((lc, 2 * ns), F32), pltpu.VMEM((1, 2 * ns), F32)],
        compiler_params=_params("parallel", "parallel", "arbitrary"),
        name="s5_scan",
    )(proj, bd, cd, tab, d_skip.reshape(1, S5_WIDTH), h0)
    hl_re = hl[..., :ns].reshape(bsz, S5_GROUPS, S5_STATE)
    hl_im = hl[..., ns:].reshape(bsz, S5_GROUPS, S5_STATE)
    return y, hl_re, hl_im


def _glu_kernel(y_ref, w_ref, b_ref, o_ref):
    y = jax.nn.gelu(y_ref[...])
    z = jnp.dot(y.astype(BF16), w_ref[...], preferred_element_type=F32) + b_ref[...]
    o_ref[...] = y * jax.nn.sigmoid(z)


def s5_glu(y, w, b, tm):
    t, k = y.shape
    return pl.pallas_call(
        _glu_kernel, grid=(t // tm,),
        in_specs=[pl.BlockSpec((tm, k), lambda i: (i, 0)), pl.BlockSpec((k, k), lambda i: (0, 0)),
                  pl.BlockSpec((1, k), lambda i: (0, 0))],
        out_specs=pl.BlockSpec((tm, k), lambda i: (i, 0)),
        out_shape=jax.ShapeDtypeStruct((t, k), F32),
        compiler_params=_params("parallel"),
        name="s5_glu",
    )(y, w, b.reshape(1, k))


def _gmlp_kernel(zu_ref, zv_ref, g_ref, b_ref, w_ref, bs_ref, o_ref, v_ref, *, lc):
    u = jax.nn.gelu(zu_ref[...])
    v = _layer_norm(jax.nn.gelu(zv_ref[...]), g_ref[...], b_ref[...])
    v_ref[...] = v
    row = lax.broadcasted_iota(jnp.int32, (lc, lc), 0)
    col = lax.broadcasted_iota(jnp.int32, (lc, lc), 1)
    causal = row >= col
    vb = v.astype(BF16)
    bs = bs_ref[...]
    for h in range(GM_HEADS):
        cols = slice(h * GM_HEAD_DIM, (h + 1) * GM_HEAD_DIM)
        w = jnp.where(causal, w_ref[h], 0.0).astype(BF16)
        mixed = jnp.dot(w, vb[:, cols], preferred_element_type=F32) + bs[:, h:h + 1]
        o_ref[:, cols] = u[:, cols] * mixed


def chunk_gmlp(proj, ln_g, ln_b, w_s, b_s, lc):
    bsz, seq = proj.shape[0], proj.shape[1]
    w = w_s[:, :lc, :lc]
    bs = b_s[:, :lc].T
    fixed2 = lambda b, n: (0, 0)
    return pl.pallas_call(
        functools.partial(_gmlp_kernel, lc=lc),
        grid=(bsz, seq // lc),
        in_specs=[pl.BlockSpec((None, lc, GM_WIDTH), lambda b, n: (b, n, 1)),
                  pl.BlockSpec((None, lc, GM_WIDTH), lambda b, n: (b, n, 2)),
                  pl.BlockSpec((1, GM_WIDTH), fixed2), pl.BlockSpec((1, GM_WIDTH), fixed2),
                  pl.BlockSpec((GM_HEADS, lc, lc), lambda b, n: (0, 0, 0)),
                  pl.BlockSpec((lc, GM_HEADS), fixed2)],
        out_specs=[pl.BlockSpec((None, lc, GM_WIDTH), lambda b, n: (b, n, 0)),
                   pl.BlockSpec((None, lc, GM_WIDTH), lambda b, n: (b, n, 0))],
        out_shape=[jax.ShapeDtypeStruct((bsz, seq, GM_WIDTH), F32),
                   jax.ShapeDtypeStruct((bsz, seq, GM_WIDTH), F32)],
        compiler_params=_params("parallel", "parallel"),
        name="chunk_gmlp",
    )(proj, proj, ln_g.reshape(1, GM_WIDTH), ln_b.reshape(1, GM_WIDTH), w, bs)


def _logf_kernel(x_ref, w_ref, b_ref, o_ref):
    z = jnp.dot(x_ref[...].astype(BF16), w_ref[...].astype(BF16), preferred_element_type=F32)
    o_ref[...] = jax.nn.log_sigmoid(z + b_ref[...])


def forget_log_gate(x, w_f, b_f, tm):
    t, k = x.shape
    return pl.pallas_call(
        _logf_kernel, grid=(t // tm,),
        in_specs=[pl.BlockSpec((tm, k), lambda i: (i, 0)), pl.BlockSpec((k, N_HEADS), lambda i: (0, 0)),
                  pl.BlockSpec((1, N_HEADS), lambda i: (0, 0))],
        out_specs=pl.BlockSpec((tm, N_HEADS), lambda i: (i, 0)),
        out_shape=jax.ShapeDtypeStruct((t, N_HEADS), F32),
        compiler_params=_params("parallel"),
        name="forget_log_gate",
    )(x, w_f, b_f.reshape(1, N_HEADS))


def _fox_prompt_kernel(q_ref, k_ref, v_ref, fq_ref, fk_ref, o_ref, m_sc, l_sc, acc_sc, *, tq):
    hg = pl.program_id(1)
    qi = pl.program_id(2)
    ki = pl.program_id(3)

    @pl.when(ki == 0)
    def _():
        m_sc[...] = jnp.full_like(m_sc, NEG_INF)
        l_sc[...] = jnp.zeros_like(l_sc)
        acc_sc[...] = jnp.zeros_like(acc_sc)

    def block(on_diagonal):
        for hl in range(FOX_HEADS_PER_STEP):
            cols = slice(hl * HEAD_DIM, (hl + 1) * HEAD_DIM)
            q = (q_ref[:, cols] * ATTN_SCALE).astype(BF16)
            s = lax.dot_general(q, k_ref[:, cols].astype(BF16), (((1,), (1,)), ((), ())),
                                preferred_element_type=F32)
            head = lax.broadcasted_iota(jnp.int32, (tq, N_HEADS), 1) == hg * FOX_HEADS_PER_STEP + hl
            fq = jnp.sum(jnp.where(head, fq_ref[...], 0.0), axis=-1, keepdims=True)
            s = s + fq - fk_ref[hl]
            if on_diagonal:
                causal = (lax.broadcasted_iota(jnp.int32, (tq, tq), 0)
                          >= lax.broadcasted_iota(jnp.int32, (tq, tq), 1))
                s = jnp.where(causal, s, NEG_INF)
            m_old = m_sc[hl]
            m_new = jnp.maximum(m_old, jnp.max(s, axis=-1, keepdims=True))
            alpha = jnp.exp(m_old - m_new)
            p = jnp.exp(s - m_new)
            l_sc[hl] = alpha * l_sc[hl] + jnp.sum(p, axis=-1, keepdims=True)
            acc_sc[:, cols] = alpha * acc_sc[:, cols] + jnp.dot(p.astype(BF16), v_ref[:, cols].astype(BF16),
                                                                 preferred_element_type=F32)
            m_sc[hl] = m_new

    @pl.when(ki < qi)
    def _():
        block(False)

    @pl.when(ki == qi)
    def _():
        block(True)

    @pl.when(ki == pl.num_programs(3) - 1)
    def _():
        for hl in range(FOX_HEADS_PER_STEP):
            cols = slice(hl * HEAD_DIM, (hl + 1) * HEAD_DIM)
            o_ref[:, cols] = acc_sc[:, cols] / l_sc[hl]


def fox_prompt(q, k, v, f_cum, tq):
    bsz, seq = q.shape[0], q.shape[1]
    nq = seq // tq
    hps = FOX_HEADS_PER_STEP
    width = hps * HEAD_DIM
    f_rows = f_cum.transpose(0, 2, 1).reshape(bsz, N_HEADS, 1, seq)
    kv_map = lambda b, h, i, j: (b, jnp.minimum(j, i), h)
    return pl.pallas_call(
        functools.partial(_fox_prompt_kernel, tq=tq),
        grid=(bsz, N_HEADS // hps, nq, nq),
        in_specs=[pl.BlockSpec((None, tq, width), lambda b, h, i, j: (b, i, h)),
                  pl.BlockSpec((None, tq, width), kv_map),
                  pl.BlockSpec((None, tq, width), kv_map),
                  pl.BlockSpec((None, tq, N_HEADS), lambda b, h, i, j: (b, i, 0)),
                  pl.BlockSpec((None, hps, 1, tq), lambda b, h, i, j: (b, h, 0, jnp.minimum(j, i)))],
        out_specs=pl.BlockSpec((None, tq, width), lambda b, h, i, j: (b, i, h)),
        out_shape=jax.ShapeDtypeStruct((bsz, seq, D_MODEL), F32),
        scratch_shapes=[pltpu.VMEM((hps, tq, 1), F32), pltpu.VMEM((hps, tq, 1), F32),
                        pltpu.VMEM((tq, width), F32)],
        compiler_params=_params("parallel", "parallel", "parallel", "arbitrary"),
        name="fox_prompt",
    )(q, k, v, f_cum, f_rows)


def _fox_sample_kernel(pt_ref, q_ref, kn_ref, vn_ref, gc_ref, gr_ref, *rest, n_new, pps):
    kp_refs, vp_refs, lp_refs = rest[:pps], rest[pps:2 * pps], rest[2 * pps:3 * pps]
    o_ref, m_sc, l_sc, acc_sc, carry_sc = rest[3 * pps:]
    step = pl.program_id(1)
    nt = (((1,), (1,)), ((), ()))
    head_rows = lambda h: slice(h * n_new, (h + 1) * n_new)
    head_cols = lambda h: slice(h * HEAD_DIM, (h + 1) * HEAD_DIM)
    q_heads = [(q_ref[:, head_cols(h)] * ATTN_SCALE).astype(BF16) for h in range(N_HEADS)]

    def online_update(s, values):
        m_old = m_sc[...]
        m_new = jnp.maximum(m_old, jnp.max(s, axis=-1, keepdims=True))
        alpha = jnp.exp(m_old - m_new)
        p = jnp.exp(s - m_new)
        l_sc[...] = alpha * l_sc[...] + jnp.sum(p, axis=-1, keepdims=True)
        pv = [jnp.dot(p[head_rows(h), :].astype(BF16), values(h), preferred_element_type=F32)
              for h in range(N_HEADS)]
        acc_sc[...] = alpha * acc_sc[...] + jnp.concatenate(pv, axis=0)
        m_sc[...] = m_new

    @pl.when(step == 0)
    def _():
        m_sc[...] = jnp.full_like(m_sc, NEG_INF)
        l_sc[...] = jnp.zeros_like(l_sc)
        acc_sc[...] = jnp.zeros_like(acc_sc)
        carry_sc[...] = jnp.zeros_like(carry_sc)
        s = jnp.concatenate([lax.dot_general(q_heads[h], kn_ref[:, head_cols(h)].astype(BF16), nt,
                                             preferred_element_type=F32) for h in range(N_HEADS)], axis=0)
        s = s + gc_ref[...] - gr_ref[...]
        rows = N_HEADS * n_new
        causal = (lax.broadcasted_iota(jnp.int32, (rows, n_new), 0) % n_new
                  >= lax.broadcasted_iota(jnp.int32, (rows, n_new), 1))
        online_update(jnp.where(causal, s, NEG_INF), lambda h: vn_ref[:, head_cols(h)].astype(BF16))

    @pl.when(step > 0)
    def _():
        later = (lax.broadcasted_iota(jnp.int32, (PAGE_SIZE, PAGE_SIZE), 0)
                 > lax.broadcasted_iota(jnp.int32, (PAGE_SIZE, PAGE_SIZE), 1)).astype(F32)
        carry = carry_sc[...]
        page_scores = []
        for pg in range(pps):
            lp = lp_refs[pg][...]
            suffix = jnp.dot(lp, later, precision=HIGHEST, preferred_element_type=F32) + carry
            carry = carry + jnp.sum(lp, axis=-1, keepdims=True)
            qk = [lax.dot_general(q_heads[h], kp_refs[pg][pl.ds(h, PAGE_SIZE, stride=N_HEADS), :].astype(BF16),
                                  nt, preferred_element_type=F32)
                  + jnp.broadcast_to(suffix[h:h + 1, :], (n_new, PAGE_SIZE)) for h in range(N_HEADS)]
            page_scores.append(jnp.concatenate(qk, axis=0))
        carry_sc[...] = carry
        s = jnp.concatenate(page_scores, axis=1) + gc_ref[...]

        def values(h):
            return jnp.concatenate([vp_refs[pg][pl.ds(h, PAGE_SIZE, stride=N_HEADS), :].astype(BF16)
                                    for pg in range(pps)], axis=0)

        online_update(s, values)

    @pl.when(step == pl.num_programs(1) - 1)
    def _():
        out = acc_sc[...] / l_sc[...]
        for h in range(N_HEADS):
            o_ref[:, head_cols(h)] = out[head_rows(h), :]


def fox_sample(q, k, v, g_new, cache_k, cache_v, cache_logf_t, page_table, layer):
    dbs, n_new = q.shape[0], q.shape[1]
    n_pages = page_table.shape[1]
    pps = math.gcd(n_pages, FOX_PAGES_PER_STEP)
    rows = N_HEADS * n_new
    g_t = g_new.transpose(0, 2, 1)
    g_col = g_t.reshape(dbs, rows, 1)
    g_row = jnp.repeat(g_t, n_new, axis=1)

    def page_map(pg):
        def index(b, s, pt):
            first = jnp.maximum(s - 1, 0) * pps
            return (layer, pt[b, n_pages - 1 - (first + pg)], 0, 0)
        return index

    seq_map = lambda b, s, pt: (b, 0, 0)
    page_specs = lambda shape: [pl.BlockSpec((None, None) + shape, page_map(pg)) for pg in range(pps)]
    grid_spec = pltpu.PrefetchScalarGridSpec(
        num_scalar_prefetch=1,
        grid=(dbs, n_pages // pps + 1),
        in_specs=([pl.BlockSpec((None, n_new, D_MODEL), seq_map)] * 3
                  + [pl.BlockSpec((None, rows, 1), seq_map), pl.BlockSpec((None, rows, n_new), seq_map)]
                  + page_specs((PAGE_SIZE * N_HEADS, HEAD_DIM)) * 2
                  + page_specs((N_HEADS, PAGE_SIZE))),
        out_specs=pl.BlockSpec((None, n_new, D_MODEL), seq_map),
        scratch_shapes=[pltpu.VMEM((rows, 1), F32), pltpu.VMEM((rows, 1), F32),
                        pltpu.VMEM((rows, HEAD_DIM), F32), pltpu.VMEM((N_HEADS, 1), F32)],
    )
    return pl.pallas_call(
        functools.partial(_fox_sample_kernel, n_new=n_new, pps=pps),
        grid_spec=grid_spec,
        out_shape=jax.ShapeDtypeStruct((dbs, n_new, D_MODEL), F32),
        compiler_params=_params("parallel", "arbitrary"),
        name="fox_sample",
    )(page_table, q, k, v, g_col, g_row, *([cache_k] * pps), *([cache_v] * pps), *([cache_logf_t] * pps))


def _take_top(s, count):
    return _take_top_ranked(s, count, with_rank=False)[0]


def _take_top_ranked(s, count, with_rank=True):
    tops = []
    rank = jnp.full_like(s, float(count)) if with_rank else None
    for i in range(count):
        m = jnp.max(s, axis=0, keepdims=True)
        tops.append(m)
        if with_rank or i + 1 < count:
            hit = s == m
            if with_rank:
                rank = jnp.where(hit, float(i), rank)
            if i + 1 < count:
                s = jnp.where(hit, NEG_INF, s)
    return tops, rank


def _candidate_sums(top1, top2):
    limit = PEER_TOPK + 1
    t2_head = jnp.concatenate(top2[:SUBLANES], axis=0)
    row = lax.broadcasted_iota(jnp.int32, t2_head.shape, 0)
    pieces = [top1[0] + jnp.concatenate(top2, axis=0)]
    for i in range(1, SUBLANES):
        pieces.append(jnp.where(row < limit // (i + 1), top1[i] + t2_head, NEG_INF))
    assert limit // (SUBLANES + 1) == 1
    pieces.append(jnp.concatenate(top1[SUBLANES:], axis=0) + top2[0])
    return jnp.concatenate(pieces, axis=0)


def _peer_route_kernel(q_ref, keys_ref, r2_ref, e2_ref, n2_ref, e1_ref, *, n_sub):
    nt = (((1,), (1,)), ((), ()))
    for sub in range(n_sub):
        tok = slice(sub * LANES, (sub + 1) * LANES) if n_sub > 1 else slice(None)
        q = q_ref[tok, :].astype(BF16)
        s1 = lax.dot_general(keys_ref[0].astype(BF16), q[:, :PEER_KEYS], nt, preferred_element_type=F32)
        s2 = lax.dot_general(keys_ref[1].astype(BF16), q[:, PEER_KEYS:], nt, preferred_element_type=F32)
        top1 = _take_top(s1, PEER_TOPK)
        top2, r2 = _take_top_ranked(s2, PEER_TOPK)
        best = _take_top(_candidate_sums(top1, top2), PEER_TOPK + 1)
        z = jnp.ones_like(best[0])
        for m in best[1:PEER_TOPK]:
            z += jnp.exp(m - best[0])
        cut = 0.5 * (best[PEER_TOPK - 1] + best[PEER_TOPK])
        n2 = jnp.zeros_like(s1)
        for t2 in top2:
            n2 += jnp.where(s1 + t2 > cut, 1.0, 0.0)
        r2_ref[:, tok] = r2.astype(BF16)
        e2_ref[:, tok] = jnp.where(s2 >= top2[PEER_TOPK - 1], jnp.exp(s2 - top2[0]), 0.0).astype(BF16)
        n2_ref[:, tok] = n2
        e1_ref[:, tok] = jnp.where(s1 >= top1[PEER_TOPK - 1], jnp.exp(s1 - top1[0]) / z, 0.0)


def peer_route(q, sub_keys, tm):
    t = q.shape[0]
    n_sub = max(tm // LANES, 1)
    out = lambda dtype: jax.ShapeDtypeStruct((PEER_HEADS, PEER_KEYS, t), dtype)
    ospec = pl.BlockSpec((None, PEER_KEYS, tm), lambda i, h: (h, 0, i))
    return pl.pallas_call(
        functools.partial(_peer_route_kernel, n_sub=n_sub),
        grid=(t // tm, PEER_HEADS),
        in_specs=[pl.BlockSpec((tm, 2 * PEER_KEYS), lambda i, h: (i, h)),
                  pl.BlockSpec((None, 2, PEER_KEYS, PEER_KEYS), lambda i, h: (h, 0, 0, 0))],
        out_specs=[ospec, ospec, ospec, ospec],
        out_shape=[out(BF16), out(BF16), out(F32), out(F32)],
        compiler_params=_params("parallel", "parallel"),
        name="peer_route",
    )(q, sub_keys)


def _peer_dense_kernel(x_ref, u_ref, vt_ref, r2_ref, e2_ref, n2_ref, e1_ref, res_ref, ln_g_ref, ln_b_ref,
                       o_ref, ob_ref, acc_sc, z_sc, g_sc, *, te, tm):
    k = pl.program_id(1)
    a_per_tile = te // PEER_KEYS
    rows_per_chunk = 2 * SUBLANES
    lanes_per_chunk = min(tm, 2 * LANES)
    n_chunks = PEER_KEYS // rows_per_chunk
    chunk_shape = (rows_per_chunk, lanes_per_chunk)

    @pl.when(k == 0)
    def _():
        acc_sc[...] = jnp.zeros_like(acc_sc)

    z_sc[...] = jnp.dot(u_ref[...], x_ref[...], preferred_element_type=F32)
    for al in range(a_per_tile):
        for t0 in range(0, tm, lanes_per_chunk):
            tok = slice(t0, t0 + lanes_per_chunk)
            gates = [jnp.zeros(chunk_shape, BF16) for _ in range(n_chunks)]
            for h in range(PEER_HEADS):
                n2 = jnp.broadcast_to(n2_ref[h, al:al + 1, tok], chunk_shape).astype(BF16)
                e1 = jnp.broadcast_to(e1_ref[h, al:al + 1, tok], chunk_shape).astype(BF16)
                for c in range(n_chunks):
                    rows = slice(c * rows_per_chunk, (c + 1) * rows_per_chunk)
                    keep = r2_ref[h, rows, tok] < n2
                    gates[c] += jnp.where(keep, e2_ref[h, rows, tok] * e1, jnp.zeros(chunk_shape, BF16))
            for c in range(n_chunks):
                zrows = slice(al * PEER_KEYS + c * rows_per_chunk, al * PEER_KEYS + (c + 1) * rows_per_chunk)
                g_sc[zrows, tok] = gates[c] * _gelu_tanh(z_sc[zrows, tok].astype(BF16))
    acc_sc[...] += jnp.dot(vt_ref[...], g_sc[...], preferred_element_type=F32)

    @pl.when(k == pl.num_programs(1) - 1)
    def _():
        _store_norm(DN_ALPHA * res_ref[...] + acc_sc[...].T, ln_g_ref, ln_b_ref, o_ref, ob_ref)


def peer_dense(x, x_bf, u_bf, vt_bf, routes, ln_g, ln_b, tm, te):
    t, d = x_bf.shape
    n_tiles = vt_bf.shape[0]
    assert vt_bf.shape == (n_tiles, d, te)
    r2, e2, n2, e1 = routes
    a_per_tile = te // PEER_KEYS
    n2, e1 = (r.reshape(PEER_HEADS, n_tiles, a_per_tile, t) for r in (n2, e1))
    bspec = pl.BlockSpec((PEER_HEADS, PEER_KEYS, tm), lambda i, k: (0, 0, i))
    aspec = pl.BlockSpec((PEER_HEADS, None, a_per_tile, tm), lambda i, k: (0, k, 0, i))
    once_per_tile = pl.Buffered(1)
    return pl.pallas_call(
        functools.partial(_peer_dense_kernel, te=te, tm=tm),
        grid=(t // tm, n_tiles),
        in_specs=[pl.BlockSpec((d, tm), lambda i, k: (0, i), pipeline_mode=once_per_tile),
                  pl.BlockSpec((te, d), lambda i, k: (k, 0)),
                  pl.BlockSpec((None, d, te), lambda i, k: (k, 0, 0)),
                  bspec, bspec, aspec, aspec,
                  pl.BlockSpec((tm, d), lambda i, k: (i, 0), pipeline_mode=once_per_tile),
                  pl.BlockSpec((1, d), lambda i, k: (0, 0)), pl.BlockSpec((1, d), lambda i, k: (0, 0))],
        out_specs=[pl.BlockSpec((tm, d), lambda i, k: (i, 0)), pl.BlockSpec((tm, d), lambda i, k: (i, 0))],
        out_shape=[jax.ShapeDtypeStruct((t, d), F32), jax.ShapeDtypeStruct((t, d), BF16)],
        scratch_shapes=[pltpu.VMEM((d, tm), F32), pltpu.VMEM((te, tm), F32), pltpu.VMEM((te, tm), BF16)],
        compiler_params=_params("parallel", "arbitrary"),
        name="peer_dense",
    )(x_bf.T, u_bf, vt_bf, r2, e2, n2, e1, x, ln_g, ln_b)


def peer_block(x, x_bf, wq_bf, sub_keys, u_bf, vt_bf, ln_g, ln_b, tm, tm_route):
    q = matmul(x_bf, wq_bf, 0, wq_bf.shape[1], tm)
    routes = peer_route(q, sub_keys, tm_route)
    return peer_dense(x, x_bf, u_bf, vt_bf, routes, ln_g, ln_b, tm, PEER_EXPERT_TILE)


def _row_tile(t):
    tm = min(t, TOKEN_TILE)
    assert t % tm == 0
    return tm


def kernel(x_prompt, x_sample, cache_k, cache_v, cache_logf, page_table, state_s5_re, state_s5_im, w_in_even, s5_lambda_re, s5_lambda_im, s5_log_dt, s5_b_re, s5_b_im, s5_c_re, s5_c_im, s5_d, s5_w_glu, s5_b_glu, gm_ln_g, gm_ln_b, gm_w_s, gm_b_s, w_out_even, w_in_odd, b_f, w_out_odd, ln1_g, ln1_b, ln2_g, ln2_b, peer_w_q, peer_sub_keys, peer_u, peer_v):
    bsz, seq, d = x_prompt.shape
    dbs, dseq, _ = x_sample.shape
    n_odd, n_phys = cache_k.shape[0], cache_k.shape[1]
    tp, ts = bsz * seq, dbs * dseq
    xp = x_prompt.reshape(tp, d)
    xs = x_sample.reshape(ts, d)
    xp_bf, xs_bf = xp.astype(BF16), xs.astype(BF16)
    tm_p = _row_tile(tp)
    tm_s = _row_tile(ts)
    ck = cache_k.reshape(n_odd, n_phys, PAGE_SIZE * N_HEADS, HEAD_DIM)
    cv = cache_v.reshape(n_odd, n_phys, PAGE_SIZE * N_HEADS, HEAD_DIM)
    clf = cache_logf.transpose(0, 1, 3, 2)
    zero_state = jnp.zeros((bsz, S5_GROUPS, S5_STATE), F32)
    peer_rep = max(LANES // ts, 1)

    outs = {k: [] for k in ("kp", "vp", "lp", "ks", "vs", "ls", "s5p_re", "s5p_im", "s5s_re", "s5s_im", "gmv")}
    for li in range(DEPTH):
        j = li // 2
        g1, b1 = ln1_g[li].reshape(1, d), ln1_b[li].reshape(1, d)
        g2, b2 = ln2_g[li].reshape(1, d), ln2_b[li].reshape(1, d)
        if li % 2 == 0:
            tables = _s5_tables(s5_lambda_re[j], s5_lambda_im[j], s5_log_dt[j], s5_b_re[j], s5_b_im[j],
                                s5_c_re[j], s5_c_im[j])
            n_in = w_in_even.shape[2]
            w_in, w_glu, w_out = (w.astype(BF16) for w in (w_in_even[j], s5_w_glu[j], w_out_even[j]))

            def even(x, x_bf, nb, nl, h0_re, h0_im, tm):
                proj = matmul(x_bf, w_in, 0, n_in, tm).reshape(nb, nl, n_in)
                y, hl_re, hl_im = s5_scan(proj, h0_re, h0_im, tables, s5_d[j], min(nl, S5_CHUNK))
                a_out = s5_glu(y.reshape(nb * nl, S5_WIDTH), w_glu, s5_b_glu[j], tm)
                g_out, v_rows = chunk_gmlp(proj, gm_ln_g[j], gm_ln_b[j], gm_w_s[j], gm_b_s[j], min(nl, GM_CHUNK))
                x, x_bf = out_proj_dnln(a_out, g_out.reshape(nb * nl, GM_WIDTH), w_out, x, g1, b1, tm)
                return x, x_bf, hl_re, hl_im, v_rows

            xp, xp_bf, hp_re, hp_im, _ = even(xp, xp_bf, bsz, seq, zero_state, zero_state, tm_p)
            xs, xs_bf, hs_re, hs_im, v_rows = even(xs, xs_bf, dbs, dseq, state_s5_re[j].astype(F32),
                                                   state_s5_im[j].astype(F32), tm_s)
            outs["s5p_re"].append(hp_re)
            outs["s5p_im"].append(hp_im)
            outs["s5s_re"].append(hs_re)
            outs["s5s_im"].append(hs_im)
            outs["gmv"].append(v_rows)
        else:
            w_f = w_in_odd[j][:, 3 * d:]
            w_qkv, w_out = w_in_odd[j].astype(BF16), w_out_odd[j].astype(BF16)

            def project(x_bf, tm):
                q, k, v = (matmul(x_bf, w_qkv, part, d, tm) for part in range(3))
                return q, k, v, forget_log_gate(x_bf, w_f, b_f[j], tm)

            qp, kp, vp, lfp = project(xp_bf, tm_p)
            f_cum = jnp.cumsum(lfp.reshape(bsz, seq, N_HEADS), axis=1)
            ap = fox_prompt(qp.reshape(bsz, seq, d), kp.reshape(bsz, seq, d), vp.reshape(bsz, seq, d), f_cum,
                            min(seq, FOX_BLOCK))
            xp, xp_bf = out_proj_dnln(ap.reshape(tp, d), None, w_out, xp, g1, b1, tm_p)
            qs, kss, vss, lfs = project(xs_bf, tm_s)
            g_new = jnp.cumsum(lfs.reshape(dbs, dseq, N_HEADS), axis=1)
            a_s = fox_sample(qs.reshape(dbs, dseq, d), kss.reshape(dbs, dseq, d), vss.reshape(dbs, dseq, d),
                             g_new, ck, cv, clf, page_table, j)
            xs, xs_bf = out_proj_dnln(a_s.reshape(ts, d), None, w_out, xs, g1, b1, tm_s)
            outs["kp"].append(kp.reshape(bsz, seq, N_HEADS, HEAD_DIM))
            outs["vp"].append(vp.reshape(bsz, seq, N_HEADS, HEAD_DIM))
            outs["lp"].append(lfp.reshape(bsz, seq, N_HEADS))
            outs["ks"].append(kss.reshape(dbs, dseq, N_HEADS, HEAD_DIM))
            outs["vs"].append(vss.reshape(dbs, dseq, N_HEADS, HEAD_DIM))
            outs["ls"].append(lfs.reshape(dbs, dseq, N_HEADS))
        u_bf = peer_u[li].astype(BF16)
        vt_bf = peer_v[li].reshape(-1, PEER_EXPERT_TILE, d).transpose(0, 2, 1).astype(BF16)
        wq_bf = peer_w_q[li].astype(BF16)
        xp, xp_bf = peer_block(xp, xp_bf, wq_bf, peer_sub_keys[li], u_bf, vt_bf, g2, b2, tm_p,
                               min(tm_p, PEER_ROUTE_TILE))
        t_rep = peer_rep * ts
        xs, xs_bf = (r[:ts] for r in peer_block(jnp.tile(xs, (peer_rep, 1)), jnp.tile(xs_bf, (peer_rep, 1)), wq_bf,
                                                peer_sub_keys[li], u_bf, vt_bf, g2, b2, t_rep, t_rep))

    stack = lambda name, dtype: jnp.stack(outs[name]).astype(dtype)
    return (xp.reshape(bsz, seq, d), xs.reshape(dbs, dseq, d),
            stack("kp", cache_k.dtype), stack("vp", cache_v.dtype), stack("lp", cache_logf.dtype),
            stack("ks", cache_k.dtype), stack("vs", cache_v.dtype), stack("ls", cache_logf.dtype),
            stack("s5p_re", state_s5_re.dtype), stack("s5p_im", state_s5_im.dtype),
            stack("s5s_re", state_s5_re.dtype), stack("s5s_im", state_s5_im.dtype),
            jnp.stack(outs["gmv"]))
```
